```python
import jax
import jax.numpy as jnp
from jax import lax
import numpy as np

D_MODEL = 2048
BATCH = 8
SEQ = 2048
DEPTH = 1
DEC_BATCH = 32
DEC_SEQ = 1
PAST_LEN = 16384
PAGE_SIZE = 128

N_HEADS = 16
HEAD_DIM = 128
N_KV_HEADS = 4
GROUP = N_HEADS // N_KV_HEADS
Q_W = N_HEADS * HEAD_DIM
KV_W = N_KV_HEADS * HEAD_DIM
CMP_BLOCK = 32
CMP_STRIDE = 16
CMP_HIDDEN = 128
SEL_BLOCK = 64
N_SEL = 16
WINDOW = 512
WIN_QBLOCK = 128
Q_CHUNK = 16
ROPE_THETA = 10000.0
CONV_CH = 1024
CONV_WIDTH = 31
D_FF = 6144
FFN_CONV_WIDTH = 3
PLE_DIM = 256
N_IN = Q_W + 6 * KV_W + 3 * N_HEADS + 2 * CONV_CH + 2 * D_MODEL
EPS = 1e-6
NEG_INF = -1e30
SEL_FORCE = 1e6

kernel_name = 'nsa_conformer_hybrid_decode_step'


def rms_norm(x, g):
    xf = x.astype(jnp.float32)
    y = xf * lax.rsqrt(jnp.mean(xf * xf, axis=-1, keepdims=True) + EPS)
    return (y * g.astype(jnp.float32)).astype(x.dtype)


def layer_norm(x, g, b):
    xf = x.astype(jnp.float32)
    xc = xf - jnp.mean(xf, axis=-1, keepdims=True)
    var = jnp.mean(xc * xc, axis=-1, keepdims=True)
    return (xc * lax.rsqrt(var + EPS) * g.astype(jnp.float32) + b.astype(jnp.float32)).astype(x.dtype)


def rope(x, pos):
    half = HEAD_DIM // 2
    inv = ROPE_THETA ** (-jnp.arange(half, dtype=jnp.float32) / half)
    ang = pos.astype(jnp.float32)[:, None] * inv
    cos = jnp.cos(ang)[:, None, :]
    sin = jnp.sin(ang)[:, None, :]
    xf = x.astype(jnp.float32)
    x1, x2 = xf[..., :half], xf[..., half:]
    return jnp.concatenate([x1 * cos - x2 * sin, x2 * cos + x1 * sin], axis=-1).astype(x.dtype)


def split_cols(z):
    sizes = (Q_W, 2 * KV_W, 2 * KV_W, 2 * KV_W, 3 * N_HEADS, 2 * CONV_CH, 2 * D_MODEL)
    cuts = [int(c) for c in np.cumsum(sizes)[:-1]]
    return jnp.split(z, cuts, axis=-1)


def causal_dwconv(full, w):
    return lax.conv_general_dilated(full, w[:, None, :], window_strides=(1,), padding='VALID',
                                    dimension_numbers=('NWC', 'WIO', 'NWC'),
                                    feature_group_count=full.shape[-1])


def compress(rows, w1, b1, w2, b2, pe):
    B, T = rows.shape[:2]
    r = CMP_BLOCK // CMP_STRIDE
    n_sub = T // CMP_STRIDE
    n_cmp = n_sub - r + 1
    sub = rows[:, :n_sub * CMP_STRIDE].reshape(B, n_sub, CMP_STRIDE, N_KV_HEADS, HEAD_DIM)
    w1r = w1.reshape(r, CMP_STRIDE, HEAD_DIM, CMP_HIDDEN)
    parts = jnp.einsum('bnphd,rpdf->rbnhf', sub, w1r)
    hid = parts[0][:, :n_cmp]
    for j in range(1, r):
        hid = hid + parts[j][:, j:j + n_cmp]
    hid = jax.nn.silu(hid + (pe.reshape(-1) @ w1 + b1))
    return hid @ w2 + b2


def nsa_global(q, q_r, pos, k_cmp, v_cmp, sel_blk, n_sel):
    B, Tq = q.shape[:2]
    n_cmp = k_cmp.shape[1]
    nb = sel_blk.shape[1]
    scale = HEAD_DIM ** -0.5
    s = jnp.einsum('bthgd,bnhd->bthgn', q, k_cmp).astype(jnp.float32) * scale
    cmp_end = jnp.arange(n_cmp) * CMP_STRIDE + CMP_BLOCK
    valid = (cmp_end[None, :] <= pos[:, None] + 1)[None, :, None, None, :]
    p = jax.nn.softmax(jnp.where(valid, s, NEG_INF), axis=-1) * valid
    o_cmp = jnp.einsum('bthgn,bnhd->bthgd', p.astype(v_cmp.dtype), v_cmp)
    ratio = SEL_BLOCK // CMP_STRIDE
    imp = jnp.pad(p.sum(axis=3), ((0, 0), (0, 0), (0, 0), (1, ratio * nb - n_cmp)))
    imp = imp[..., :ratio * nb].reshape(B, Tq, N_KV_HEADS, nb, ratio).sum(-1) + imp[..., ratio::ratio]
    blk = jnp.arange(nb)[None, :]
    cur = (pos // SEL_BLOCK)[:, None]
    forced = ((blk == 0) | (blk == cur) | (blk == cur - 1))[None, :, None, :]
    causal = (blk * SEL_BLOCK <= pos[:, None])[None, :, None, :]
    score = jnp.where(causal, jnp.where(forced, SEL_FORCE, imp), -1.0)
    _, idx = lax.top_k(score, n_sel)
    b_i = jnp.arange(B)[:, None, None, None]
    h_i = jnp.arange(N_KV_HEADS)[None, None, :, None]
    kv = sel_blk[b_i, idx, :, :, h_i]
    s2 = jnp.einsum('bthgd,bthnkd->bthgnk', q_r, kv[..., 0, :]).astype(jnp.float32) * scale
    kpos = idx[..., None] * SEL_BLOCK + jnp.arange(SEL_BLOCK)
    m2 = (kpos <= pos[None, :, None, None, None])[:, :, :, None]
    s2 = jnp.where(m2, s2, NEG_INF)
    p2 = jax.nn.softmax(s2.reshape(B, Tq, N_KV_HEADS, GROUP, -1), axis=-1).reshape(s2.shape)
    o_sel = jnp.einsum('bthgnk,bthnkd->bthgd', p2.astype(kv.dtype), kv[..., 1, :])
    return o_cmp, o_sel


def window_prompt(q_r, rows_win):
    B, T = q_r.shape[:2]
    nq = T // WIN_QBLOCK
    nback = WINDOW // WIN_QBLOCK
    kb_len = (nback + 1) * WIN_QBLOCK
    kvp = jnp.pad(rows_win, ((0, 0), (WINDOW, 0), (0, 0), (0, 0), (0, 0)))
    kvp = kvp.reshape(B, nq + nback, WIN_QBLOCK, 2, N_KV_HEADS, HEAD_DIM)
    kvb = jnp.concatenate([kvp[:, j:j + nq] for j in range(nback + 1)], axis=2)
    qb = q_r.reshape(B, nq, WIN_QBLOCK, N_KV_HEADS, GROUP, HEAD_DIM)
    s = jnp.einsum('bnqhgd,bnkhd->bnhgqk', qb, kvb[:, :, :, 0]).astype(jnp.float32) * HEAD_DIM ** -0.5
    qpos = (jnp.arange(nq) * WIN_QBLOCK)[:, None] + jnp.arange(WIN_QBLOCK)
    kpos = (jnp.arange(nq) * WIN_QBLOCK - WINDOW)[:, None] + jnp.arange(kb_len)
    diff = qpos[:, :, None] - kpos[:, None, :]
    m = ((diff >= 0) & (diff < WINDOW) & (kpos[:, None, :] >= 0))[None, :, None, None]
    p = jax.nn.softmax(jnp.where(m, s, NEG_INF), axis=-1)
    o = jnp.einsum('bnhgqk,bnkhd->bnqhgd', p.astype(kvb.dtype), kvb[:, :, :, 1])
    return o.reshape(B, T, N_KV_HEADS, GROUP, HEAD_DIM)


def window_sample(q_r, pos, win_buf, rows_win):
    wb = win_buf.shape[1]
    kv = jnp.concatenate([win_buf, rows_win], axis=1)
    kpos = pos[0] - wb + jnp.arange(kv.shape[1])
    diff = pos[:, None] - kpos[None, :]
    m = ((diff >= 0) & (diff < WINDOW))[None, :, None, None, :]
    s = jnp.einsum('bthgd,bkhd->bthgk', q_r, kv[:, :, 0]).astype(jnp.float32) * HEAD_DIM ** -0.5
    p = jax.nn.softmax(jnp.where(m, s, NEG_INF), axis=-1)
    o = jnp.einsum('bthgk,bkhd->bthgd', p.astype(kv.dtype), kv[:, :, 1])
    return o, kv[:, -wb:]


def conv_module(glu_in, buf, w_dw, b_dw, g_ln, b_ln, w_pw):
    a, b = jnp.split(glu_in, 2, axis=-1)
    u = a * jax.nn.sigmoid(b)
    full = jnp.concatenate([buf, u], axis=1)
    y = causal_dwconv(full, w_dw) + b_dw
    y = jax.nn.silu(layer_norm(y, g_ln, b_ln))
    return y @ w_pw, full[:, -(CONV_WIDTH - 1):]


def conv_ffn(hn, buf, w_up, w_dw, b_dw, w_down):
    a, b = jnp.split(hn @ w_up, 2, axis=-1)
    full = jnp.concatenate([buf, a], axis=1)
    a = causal_dwconv(full, w_dw) + b_dw
    y = jax.nn.gelu(a, approximate=False) * b
    return y @ w_down, full[:, -(FFN_CONV_WIDTH - 1):]


def decoder_layer(h, pe, pos, lw, past_cmp=None, past_sel=None, win_buf=None, conv_buf=None, ffn_buf=None):
    B, T, _ = h.shape
    prompt = past_cmp is None
    z = rms_norm(h, lw['g_mix']) @ lw['w_in']
    q, kv_c, kv_s, kv_w, g_nsa, glu_in, g_merge = split_cols(z)
    q = rms_norm(q.reshape(B, T, N_HEADS, HEAD_DIM), lw['g_q'])
    q_r = rope(q, pos).reshape(B, T, N_KV_HEADS, GROUP, HEAD_DIM)
    q = q.reshape(B, T, N_KV_HEADS, GROUP, HEAD_DIM)
    rows_cmp = kv_c.reshape(B, T, 2, N_KV_HEADS, HEAD_DIM)
    kv_s = kv_s.reshape(B, T, 2, N_KV_HEADS, HEAD_DIM)
    rows_sel = jnp.stack([rope(rms_norm(kv_s[:, :, 0], lw['g_k'][1]), pos), kv_s[:, :, 1]], axis=2)
    kv_w = kv_w.reshape(B, T, 2, N_KV_HEADS, HEAD_DIM)
    rows_win = jnp.stack([rope(rms_norm(kv_w[:, :, 0], lw['g_k'][2]), pos), kv_w[:, :, 1]], axis=2)
    full_cmp = rows_cmp if prompt else jnp.concatenate([past_cmp, rows_cmp], axis=1)
    full_sel = rows_sel if prompt else jnp.concatenate([past_sel, rows_sel], axis=1)
    k_cmp = rms_norm(compress(full_cmp[:, :, 0], lw['w_cmp1'][0], lw['b_cmp1'][0], lw['w_cmp2'][0],
                              lw['b_cmp2'][0], lw['pe_cmp'][0]), lw['g_k'][0])
    v_cmp = compress(full_cmp[:, :, 1], lw['w_cmp1'][1], lw['b_cmp1'][1], lw['w_cmp2'][1],
                     lw['b_cmp2'][1], lw['pe_cmp'][1])
    t_keys = full_sel.shape[1]
    nb = -(-t_keys // SEL_BLOCK)
    sel_blk = jnp.pad(full_sel, ((0, 0), (0, nb * SEL_BLOCK - t_keys), (0, 0), (0, 0), (0, 0)))
    sel_blk = sel_blk.reshape(B, nb, SEL_BLOCK, 2, N_KV_HEADS, HEAD_DIM)
    n_sel = min(N_SEL, nb)
    if prompt:
        nc = T // Q_CHUNK
        def to_chunks(a):
            return jnp.moveaxis(a.reshape(B, nc, Q_CHUNK, *a.shape[2:]), 1, 0)
        def from_chunks(a):
            return jnp.moveaxis(a, 0, 1).reshape(B, T, *a.shape[3:])
        def chunk_fn(args):
            qc, qrc, pc = args
            return nsa_global(qc, qrc, pc, k_cmp, v_cmp, sel_blk, n_sel)
        o_c, o_s = lax.map(chunk_fn, (to_chunks(q), to_chunks(q_r), pos.reshape(nc, Q_CHUNK)))
        o_cmp, o_sel = from_chunks(o_c), from_chunks(o_s)
        o_win = window_prompt(q_r, rows_win)
        new_win = rows_win[:, -min(WINDOW, T):]
        conv_buf = jnp.zeros((B, CONV_WIDTH - 1, CONV_CH), h.dtype)
        ffn_buf = jnp.zeros((B, FFN_CONV_WIDTH - 1, D_FF), h.dtype)
    else:
        o_cmp, o_sel = nsa_global(q, q_r, pos, k_cmp, v_cmp, sel_blk, n_sel)
        o_win, new_win = window_sample(q_r, pos, win_buf, rows_win)
    g = jax.nn.sigmoid(g_nsa.astype(jnp.float32)).reshape(B, T, 3, N_KV_HEADS, GROUP, 1)
    o = (g[:, :, 0] * o_cmp + g[:, :, 1] * o_sel + g[:, :, 2] * o_win).astype(h.dtype)
    y_attn = o.reshape(B, T, Q_W) @ lw['w_attn_out']
    y_conv, new_conv = conv_module(glu_in, conv_buf, lw['w_conv_dw'], lw['b_conv_dw'],
                                   lw['g_conv_ln'], lw['b_conv_ln'], lw['w_conv_out'])
    gm = jax.nn.sigmoid(g_merge.astype(jnp.float32))
    mixed = (gm[..., :D_MODEL] * y_attn + gm[..., D_MODEL:] * y_conv).astype(h.dtype)
    h = h + mixed @ lw['w_out']
    y_ffn, new_ffn = conv_ffn(rms_norm(h, lw['g_ffn']), ffn_buf, lw['w_up'], lw['w_ffn_dw'],
                              lw['b_ffn_dw'], lw['w_down'])
    h = h + y_ffn
    gate = jax.nn.sigmoid((rms_norm(h, lw['g_ple']) @ lw['w_ple_gate']).astype(jnp.float32))
    h = h + (gate * (pe @ lw['w_ple_proj'])).astype(h.dtype)
    return h, (rows_cmp, rows_sel, new_win, new_conv, new_ffn)


def setup_inputs(seed: int = 0) -> dict:
    key = jax.random.key(seed)
    ks = jax.random.split(key, 40)

    def nrm(i, shape, scale):
        return jax.random.normal(ks[i], shape, jnp.float32) * scale

    n_pages = PAST_LEN // PAGE_SIZE
    n_used = DEC_BATCH * n_pages
    n_pool = n_used + n_used // 4
    win_len = min(WINDOW, PAST_LEN)
    page_table = jax.random.permutation(ks[0], n_pool)[:n_used].reshape(DEC_BATCH, n_pages).astype(jnp.int32)
    return {
        'x_prompt': nrm(1, (BATCH, SEQ, D_MODEL), 1.0),
        'x_sample': nrm(2, (DEC_BATCH, DEC_SEQ, D_MODEL), 1.0),
        'p_prompt': nrm(3, (DEPTH, BATCH, SEQ, PLE_DIM), 1.0),
        'p_sample': nrm(4, (DEPTH, DEC_BATCH, DEC_SEQ, PLE_DIM), 1.0),
        'cache_cmp_kv': nrm(5, (DEPTH, n_pool, PAGE_SIZE, 2, N_KV_HEADS, HEAD_DIM), 1.0),
        'cache_sel_kv': nrm(6, (DEPTH, n_pool, PAGE_SIZE, 2, N_KV_HEADS, HEAD_DIM), 1.0),
        'cache_win_kv': nrm(7, (DEPTH, DEC_BATCH, win_len, 2, N_KV_HEADS, HEAD_DIM), 1.0),
        'state_conv': nrm(8, (DEPTH, DEC_BATCH, CONV_WIDTH - 1, CONV_CH), 0.5),
        'state_ffn_conv': nrm(9, (DEPTH, DEC_BATCH, FFN_CONV_WIDTH - 1, D_FF), 1.0),
        'page_table': page_table,
        'g_mix': 1.0 + nrm(10, (DEPTH, D_MODEL), 0.02),
        'w_in': nrm(11, (DEPTH, D_MODEL, N_IN), D_MODEL ** -0.5),
        'g_q': 1.0 + nrm(12, (DEPTH, HEAD_DIM), 0.02),
        'g_k': 1.0 + nrm(13, (DEPTH, 3, HEAD_DIM), 0.02),
        'w_cmp1': nrm(14, (DEPTH, 2, CMP_BLOCK * HEAD_DIM, CMP_HIDDEN), (CMP_BLOCK * HEAD_DIM) ** -0.5),
        'b_cmp1': nrm(15, (DEPTH, 2, CMP_HIDDEN), 0.02),
        'w_cmp2': nrm(16, (DEPTH, 2, CMP_HIDDEN, HEAD_DIM), CMP_HIDDEN ** -0.5),
        'b_cmp2': nrm(17, (DEPTH, 2, HEAD_DIM), 0.02),
        'pe_cmp': nrm(18, (DEPTH, 2, CMP_BLOCK, HEAD_DIM), 0.1),
        'w_attn_out': nrm(19, (DEPTH, Q_W, D_MODEL), Q_W ** -0.5),
        'w_conv_dw': nrm(20, (DEPTH, CONV_WIDTH, CONV_CH), CONV_WIDTH ** -0.5),
        'b_conv_dw': nrm(21, (DEPTH, CONV_CH), 0.02),
        'g_conv_ln': 1.0 + nrm(22, (DEPTH, CONV_CH), 0.02),
        'b_conv_ln': nrm(23, (DEPTH, CONV_CH), 0.02),
        'w_conv_out': nrm(24, (DEPTH, CONV_CH, D_MODEL), CONV_CH ** -0.5),
        'w_out': nrm(25, (DEPTH, D_MODEL, D_MODEL), D_MODEL ** -0.5),
        'g_ffn': 1.0 + nrm(26, (DEPTH, D_MODEL), 0.02),
        'w_up': nrm(27, (DEPTH, D_MODEL, 2 * D_FF), D_MODEL ** -0.5),
        'w_ffn_dw': nrm(28, (DEPTH, FFN_CONV_WIDTH, D_FF), FFN_CONV_WIDTH ** -0.5),
        'b_ffn_dw': nrm(29, (DEPTH, D_FF), 0.02),
        'w_down': nrm(30, (DEPTH, D_FF, D_MODEL), D_FF ** -0.5),
        'g_ple': 1.0 + nrm(31, (DEPTH, D_MODEL), 0.02),
        'w_ple_gate': nrm(32, (DEPTH, D_MODEL, D_MODEL), D_MODEL ** -0.5),
        'w_ple_proj': nrm(33, (DEPTH, PLE_DIM, D_MODEL), PLE_DIM ** -0.5),
    }


def reference(x_prompt, x_sample, p_prompt, p_sample, cache_cmp_kv, cache_sel_kv, cache_win_kv,
              state_conv, state_ffn_conv, page_table, g_mix, w_in, g_q, g_k, w_cmp1, b_cmp1, w_cmp2,
              b_cmp2, pe_cmp, w_attn_out, w_conv_dw, b_conv_dw, g_conv_ln, b_conv_ln, w_conv_out, w_out,
              g_ffn, w_up, w_ffn_dw, b_ffn_dw, w_down, g_ple, w_ple_gate, w_ple_proj):
    T = x_prompt.shape[1]
    DB, DS = x_sample.shape[:2]
    past_len = page_table.shape[1] * PAGE_SIZE
    pos_p = jnp.arange(T, dtype=jnp.int32)
    pos_s = past_len + jnp.arange(DS, dtype=jnp.int32)
    hp, hs = x_prompt, x_sample
    cmp_p, cmp_s, sel_p, sel_s, win_p, win_s, conv_p, conv_s, ffn_p, ffn_s = ([] for _ in range(10))
    for i in range(DEPTH):
        lw = {'g_mix': g_mix[i], 'w_in': w_in[i], 'g_q': g_q[i], 'g_k': g_k[i],
              'w_cmp1': w_cmp1[i], 'b_cmp1': b_cmp1[i], 'w_cmp2': w_cmp2[i], 'b_cmp2': b_cmp2[i],
              'pe_cmp': pe_cmp[i], 'w_attn_out': w_attn_out[i], 'w_conv_dw': w_conv_dw[i],
              'b_conv_dw': b_conv_dw[i], 'g_conv_ln': g_conv_ln[i], 'b_conv_ln': b_conv_ln[i],
              'w_conv_out': w_conv_out[i], 'w_out': w_out[i], 'g_ffn': g_ffn[i], 'w_up': w_up[i],
              'w_ffn_dw': w_ffn_dw[i], 'b_ffn_dw': b_ffn_dw[i], 'w_down': w_down[i], 'g_ple': g_ple[i],
              'w_ple_gate': w_ple_gate[i], 'w_ple_proj': w_ple_proj[i]}
        hp, st_p = decoder_layer(hp, p_prompt[i], pos_p, lw)
        past_cmp = cache_cmp_kv[i, page_table].reshape(DB, past_len, 2, N_KV_HEADS, HEAD_DIM)
        past_sel = cache_sel_kv[i, page_table].reshape(DB, past_len, 2, N_KV_HEADS, HEAD_DIM)
        hs, st_s = decoder_layer(hs, p_sample[i], pos_s, lw, past_cmp, past_sel, cache_win_kv[i],
                                 state_conv[i], state_ffn_conv[i])
        cmp_p.append(st_p[0]); sel_p.append(st_p[1]); win_p.append(st_p[2]); conv_p.append(st_p[3]); ffn_p.append(st_p[4])
        cmp_s.append(st_s[0]); sel_s.append(st_s[1]); win_s.append(st_s[2]); conv_s.append(st_s[3]); ffn_s.append(st_s[4])
    return (hp, hs, jnp.stack(cmp_p), jnp.stack(cmp_s), jnp.stack(sel_p), jnp.stack(sel_s),
            jnp.stack(win_p), jnp.stack(win_s), jnp.stack(conv_p), jnp.stack(conv_s),
            jnp.stack(ffn_p), jnp.stack(ffn_s))
```

```python
import functools

import numpy as np
import jax
import jax.numpy as jnp
from jax import lax
from jax.experimental import pallas as pl
from jax.experimental.pallas import tpu as pltpu

N_HEADS = 16
HEAD_DIM = 128
N_KV_HEADS = 4
GROUP = N_HEADS // N_KV_HEADS
Q_W = N_HEADS * HEAD_DIM
KV_W = N_KV_HEADS * HEAD_DIM
ROW_W = 2 * KV_W
CMP_BLOCK = 32
CMP_STRIDE = 16
SEL_BLOCK = 64
SEL_SHIFT = 6
N_SEL = 16
WINDOW = 512
PAGE_SIZE = 128
ROPE_THETA = 10000.0
EPS = 1e-6
NEG_INF = -1e30
SEL_FORCE = 1e6
CONV_HALO = 32
FFN_HALO = 16

LANES = 128
SUBLANES = 8
VMEM_LIMIT = 56 * 1024 * 1024

F32 = jnp.float32
BF16 = jnp.bfloat16
NT_DIMS = (((1,), (1,)), ((), ()))


def _params(*sem):
    return pltpu.CompilerParams(dimension_semantics=sem, vmem_limit_bytes=VMEM_LIMIT)


def _rms(x):
    return x * lax.rsqrt(jnp.mean(x * x, axis=-1, keepdims=True) + EPS)


def _rope(x, cos, sin_signed):
    return x * cos + pltpu.roll(x, HEAD_DIM // 2, 1) * sin_signed


def _sigmoid(x):
    return jax.nn.sigmoid(x)


def _softmax_rows(s):
    m = jnp.max(s, axis=-1, keepdims=True)
    e = jnp.exp(s - m)
    return e, jnp.sum(e, axis=-1, keepdims=True)


def _rmsnorm_kernel(x_ref, g_ref, o_ref):
    o_ref[...] = (_rms(x_ref[...]) * g_ref[...]).astype(o_ref.dtype)


def _rmsnorm_bf16(x, g, tm):
    m, d = x.shape
    return pl.pallas_call(
        _rmsnorm_kernel,
        grid=(m // tm,),
        in_specs=[pl.BlockSpec((tm, d), lambda i: (i, 0)), pl.BlockSpec((1, d), lambda i: (0, 0))],
        out_specs=pl.BlockSpec((tm, d), lambda i: (i, 0)),
        out_shape=jax.ShapeDtypeStruct((m, d), BF16),
        compiler_params=_params("parallel"),
        name="rmsnorm_in",
    )(x, g.reshape(1, d))


def _matmul_kernel(x_ref, w_ref, o_ref):
    o_ref[...] = jnp.dot(x_ref[...], w_ref[...], preferred_element_type=F32)


def _matmul(x, w, tm, tn):
    m, k = x.shape
    n = w.shape[1]
    return pl.pallas_call(
        _matmul_kernel,
        grid=(m // tm, n // tn),
        in_specs=[pl.BlockSpec((tm, k), lambda i, j: (i, 0)), pl.BlockSpec((k, tn), lambda i, j: (0, j))],
        out_specs=pl.BlockSpec((tm, tn), lambda i, j: (i, j)),
        out_shape=jax.ShapeDtypeStruct((m, n), F32),
        compiler_params=_params("parallel", "arbitrary"),
        name="in_proj_main",
    )(x, w)


def _cmp_gate_kernel(x_ref, w_ref, rows_ref, gate_ref):
    acc = jnp.dot(x_ref[...], w_ref[...], preferred_element_type=F32)
    rows_ref[...] = acc[:, :ROW_W]
    gate_ref[...] = _sigmoid(acc[:, ROW_W:])


def _cmp_gate_proj(xn, w, tm):
    m, k = xn.shape
    n = w.shape[1]
    ng = n - ROW_W
    return pl.pallas_call(
        _cmp_gate_kernel,
        grid=(m // tm,),
        in_specs=[pl.BlockSpec((tm, k), lambda i: (i, 0)), pl.BlockSpec((k, n), lambda i: (0, 0))],
        out_specs=[pl.BlockSpec((tm, ROW_W), lambda i: (i, 0)), pl.BlockSpec((tm, ng), lambda i: (i, 0))],
        out_shape=[jax.ShapeDtypeStruct((m, ROW_W), F32), jax.ShapeDtypeStruct((m, ng), F32)],
        compiler_params=_params("parallel"),
        name="in_proj_cmp_gates",
    )(xn, w)


def _kv_kernel(x_ref, w_ref, gk_ref, cos_ref, sin_ref, rows_ref, rows_bf_ref):
    acc = jnp.dot(x_ref[...], w_ref[...], preferred_element_type=F32)
    cos = cos_ref[...]
    sin = sin_ref[...]
    gk = gk_ref[...]
    for h in range(N_KV_HEADS):
        sl = slice(h * HEAD_DIM, (h + 1) * HEAD_DIM)
        k = _rope(_rms(acc[:, sl]) * gk, cos, sin)
        rows_ref[:, sl] = k
        rows_bf_ref[:, sl] = k.astype(BF16)
    v = acc[:, KV_W:]
    rows_ref[:, KV_W:] = v
    rows_bf_ref[:, KV_W:] = v.astype(BF16)


def _kv_proj(xn, w, gk, cos, sin, tm):
    m, k = xn.shape
    nt = cos.shape[0] // tm
    return pl.pallas_call(
        _kv_kernel,
        grid=(m // tm,),
        in_specs=[
            pl.BlockSpec((tm, k), lambda i: (i, 0)),
            pl.BlockSpec((k, ROW_W), lambda i: (0, 0)),
            pl.BlockSpec((1, HEAD_DIM), lambda i: (0, 0)),
            pl.BlockSpec((tm, HEAD_DIM), lambda i: (i % nt, 0)),
            pl.BlockSpec((tm, HEAD_DIM), lambda i: (i % nt, 0)),
        ],
        out_specs=[pl.BlockSpec((tm, ROW_W), lambda i: (i, 0)), pl.BlockSpec((tm, ROW_W), lambda i: (i, 0))],
        out_shape=[jax.ShapeDtypeStruct((m, ROW_W), F32), jax.ShapeDtypeStruct((m, ROW_W), BF16)],
        compiler_params=_params("parallel"),
        name="in_proj_kv",
    )(xn, w, gk.reshape(1, HEAD_DIM), cos, sin)


def _pe_bias_kernel(pe_ref, w1_ref, b1_ref, o_ref):
    for kv in range(2):
        acc = jnp.zeros((SUBLANES, HEAD_DIM), F32)
        for r in range(CMP_BLOCK):
            row = jnp.broadcast_to(pe_ref[kv, r : r + 1, :], (SUBLANES, HEAD_DIM)).astype(BF16)
            acc = acc + jnp.dot(
                row, w1_ref[kv, r * HEAD_DIM : (r + 1) * HEAD_DIM, :].astype(BF16), preferred_element_type=F32
            )
        o_ref[kv] = acc[0:1, :] + b1_ref[kv]


def _pe_bias(pe, w1, b1):
    hid = w1.shape[-1]
    return pl.pallas_call(
        _pe_bias_kernel,
        out_shape=jax.ShapeDtypeStruct((2, 1, hid), F32),
        compiler_params=pltpu.CompilerParams(vmem_limit_bytes=VMEM_LIMIT),
        name="cmp_pe_bias",
    )(pe, w1, b1.reshape(2, 1, hid))


def _compress_kernel(pt_ref, *refs, n_pg):
    del pt_ref
    page_refs = refs[:n_pg]
    w1_ref, bias_ref, w2_ref, b2_ref, gk_ref, kc_ref, vc_ref, carry_ref = refs[n_pg:]
    sub_per_page = PAGE_SIZE // CMP_STRIDE
    heads_per_row = 2 * N_KV_HEADS
    nsub = n_pg * sub_per_page

    @pl.when(pl.program_id(1) == 0)
    def _():
        carry_ref[...] = jnp.zeros_like(carry_ref)

    row = lax.broadcasted_iota(jnp.int32, (N_KV_HEADS * nsub, HEAD_DIM), 0)
    for kv in range(2):
        acc = None
        for pp in range(0, CMP_STRIDE, 2):
            pieces = []
            for h in range(N_KV_HEADS):
                r0 = pp * heads_per_row + kv * N_KV_HEADS + h
                stride = CMP_STRIDE * heads_per_row
                for pg in range(n_pg):
                    a = page_refs[pg][0, pl.ds(r0, sub_per_page, stride=stride), :]
                    b = page_refs[pg][0, pl.ds(r0 + heads_per_row, sub_per_page, stride=stride), :]
                    pieces.append(jnp.concatenate([a, b], axis=1))
            x = jnp.concatenate(pieces, axis=0).astype(BF16)
            d = jnp.dot(x, w1_ref[kv, pp * HEAD_DIM : (pp + 2) * HEAD_DIM, :], preferred_element_type=F32)
            acc = d if acc is None else acc + d
        hid_w = acc.shape[1] // 2
        p_first = acc[:, :hid_w]
        p_second = acc[:, hid_w:]
        shifted = pltpu.roll(p_first, 1, 0)
        carry = carry_ref[kv * N_KV_HEADS : (kv + 1) * N_KV_HEADS, :]
        for h in range(N_KV_HEADS):
            shifted = jnp.where(row == h * nsub, carry[h : h + 1, :], shifted)
            carry_ref[kv * N_KV_HEADS + h : kv * N_KV_HEADS + h + 1, :] = p_first[(h + 1) * nsub - 1 : (h + 1) * nsub, :]
        hid = shifted + p_second + bias_ref[kv]
        hid = hid * _sigmoid(hid)
        out = jnp.dot(hid.astype(BF16), w2_ref[kv], preferred_element_type=F32) + b2_ref[kv]
        if kv == 0:
            out = _rms(out) * gk_ref[...]
        dst = kc_ref if kv == 0 else vc_ref
        for h in range(N_KV_HEADS):
            dst[0, h, :, :] = out[h * nsub : (h + 1) * nsub, :].astype(BF16)


def _compress(pages, page_table, w1cat, bias, w2, b2, gk):
    n_seq, n_pp = page_table.shape
    n_pg = max(d for d in range(1, 17) if n_pp % d == 0)
    steps = n_pp // n_pg
    nsub = n_pg * (PAGE_SIZE // CMP_STRIDE)
    n_slots = steps * nsub
    hid2 = w1cat.shape[-1]
    hid = hid2 // 2

    def page_spec(k):
        return pl.BlockSpec((1, PAGE_SIZE * 2 * N_KV_HEADS, HEAD_DIM),
                            lambda b, s, pt: (pt[b * n_pp + s * n_pg + k], 0, 0))

    const3 = lambda b, s, pt: (0, 0, 0)
    out_spec = pl.BlockSpec((1, N_KV_HEADS, nsub, HEAD_DIM), lambda b, s, pt: (b, 0, s, 0))
    grid_spec = pltpu.PrefetchScalarGridSpec(
        num_scalar_prefetch=1,
        grid=(n_seq, steps),
        in_specs=[page_spec(k) for k in range(n_pg)]
        + [
            pl.BlockSpec((2, CMP_STRIDE * HEAD_DIM, hid2), const3),
            pl.BlockSpec((2, 1, hid), const3),
            pl.BlockSpec((2, hid, HEAD_DIM), const3),
            pl.BlockSpec((2, 1, HEAD_DIM), const3),
            pl.BlockSpec((1, HEAD_DIM), lambda b, s, pt: (0, 0)),
        ],
        out_specs=[out_spec, out_spec],
        scratch_shapes=[pltpu.VMEM((2 * N_KV_HEADS, hid), F32)],
    )
    shape = jax.ShapeDtypeStruct((n_seq, N_KV_HEADS, n_slots, HEAD_DIM), BF16)
    return pl.pallas_call(
        functools.partial(_compress_kernel, n_pg=n_pg),
        grid_spec=grid_spec,
        out_shape=[shape, shape],
        compiler_params=_params("parallel", "arbitrary"),
        name="compress",
    )(page_table.reshape(-1), *([pages] * n_pg), w1cat, bias, w2, b2.reshape(2, 1, HEAD_DIM), gk.reshape(1, HEAD_DIM))


def _nsa_prompt_kernel(
    zq_ref, cos_ref, sin_ref, gq_ref, gate_ref, kc_ref, vc_ref, ks_ref, vs_ref, kw_ref, vw_ref, expand_ref,
    o_ref, bias_scr, osel_scr, *, tq, t_len, nb, n_sel, n_slots, kt):
    i = pl.program_id(2)
    q0 = i * tq
    rows = GROUP * tq
    scale = HEAD_DIM ** -0.5
    cos = cos_ref[...]
    sin = sin_ref[...]
    gq = gq_ref[...]
    qs, qrs = [], []
    for g in range(GROUP):
        qn = _rms(zq_ref[:, g * HEAD_DIM : (g + 1) * HEAD_DIM]) * gq
        qs.append(qn.astype(BF16))
        qrs.append(_rope(qn, cos, sin).astype(BF16))
    q = jnp.concatenate(qs, axis=0)
    qr = jnp.concatenate(qrs, axis=0)

    s_t = lax.dot_general(kc_ref[0, 0], q, NT_DIMS, preferred_element_type=F32) * scale
    slot = lax.broadcasted_iota(jnp.int32, (n_slots, rows), 0)
    tok = q0 + (lax.broadcasted_iota(jnp.int32, (n_slots, rows), 1) & (tq - 1))
    valid = (slot >= 1) & (slot * CMP_STRIDE + CMP_STRIDE <= tok + 1)
    s_m = jnp.where(valid, s_t, NEG_INF)
    e = jnp.exp(s_m - jnp.max(s_m, axis=0, keepdims=True))
    p_t = jnp.where(valid, e / jnp.sum(e, axis=0, keepdims=True), 0.0)
    o_cmp = jnp.dot(p_t.T.astype(BF16), vc_ref[0, 0], preferred_element_type=F32)

    p_sum = p_t[:, 0:tq]
    for g in range(1, GROUP):
        p_sum = p_sum + p_t[:, g * tq : (g + 1) * tq]
    ratio = SEL_BLOCK // CMP_STRIDE
    jj = lax.broadcasted_iota(jnp.int32, (nb, n_slots), 0)
    mm = lax.broadcasted_iota(jnp.int32, (nb, n_slots), 1)
    overlap = jnp.where((mm >= ratio * jj) & (mm <= ratio * jj + ratio), 1.0, 0.0)
    imp = jnp.dot(overlap, p_sum, precision=lax.Precision.HIGHEST, preferred_element_type=F32)
    jb = lax.broadcasted_iota(jnp.int32, (nb, tq), 0)
    pos = q0 + lax.broadcasted_iota(jnp.int32, (nb, tq), 1)
    cur = lax.shift_right_logical(pos, SEL_SHIFT)
    causal = jb * SEL_BLOCK <= pos
    forced = (jb == 0) | (jb == cur) | (jb == cur - 1)
    score = jnp.where(causal, jnp.where(forced, SEL_FORCE, imp), -1.0)
    cnt = jnp.zeros((nb, tq), F32)
    for i2 in range(nb):
        r = score[i2 : i2 + 1, :]
        cnt = cnt + jnp.where(r > score, 1.0, jnp.where((r == score) & (jb > i2), 1.0, 0.0))
    sel = jnp.where((cnt < n_sel) & causal, 1.0, 0.0)
    if nb < LANES:
        sel = jnp.concatenate([sel, jnp.zeros((LANES - nb, tq), F32)], axis=0)
    key_sel = jnp.dot(sel.T.astype(BF16), expand_ref[...], preferred_element_type=F32)
    kpos = lax.broadcasted_iota(jnp.int32, (tq, t_len), 1)
    qpos = q0 + lax.broadcasted_iota(jnp.int32, (tq, t_len), 0)
    bias_scr[...] = jnp.where((key_sel > 0.5) & (kpos <= qpos), 0.0, NEG_INF)

    n_tiles = (q0 + tq + kt - 1) // kt
    for nt in range(1, t_len // kt + 1):

        @pl.when(n_tiles == nt)
        def _(nt=nt):
            ln = nt * kt
            s = lax.dot_general(qr, ks_ref[0, 0:ln, :], NT_DIMS, preferred_element_type=F32) * scale
            s = (s.reshape(GROUP, tq, ln) + bias_scr[:, 0:ln][None]).reshape(rows, ln)
            e2, l2 = _softmax_rows(s)
            osel_scr[...] = jnp.dot(e2.astype(BF16), vs_ref[0, 0:ln, :], preferred_element_type=F32) / l2

    lw = WINDOW + tq
    kw0 = pl.multiple_of(jnp.maximum(q0 - WINDOW, 0), tq)
    s = lax.dot_general(qr, kw_ref[0, pl.ds(kw0, lw), :], NT_DIMS, preferred_element_type=F32) * scale
    diff = (q0 + lax.broadcasted_iota(jnp.int32, (tq, lw), 0)) - (kw0 + lax.broadcasted_iota(jnp.int32, (tq, lw), 1))
    wbias = jnp.where((diff >= 0) & (diff < WINDOW), 0.0, NEG_INF)
    s = (s.reshape(GROUP, tq, lw) + wbias[None]).reshape(rows, lw)
    e3, l3 = _softmax_rows(s)
    o_win = jnp.dot(e3.astype(BF16), vw_ref[0, pl.ds(kw0, lw), :], preferred_element_type=F32) / l3

    o_sel = osel_scr[...]
    gt = gate_ref[...]
    outs = []
    for g in range(GROUP):
        sl = slice(g * tq, (g + 1) * tq)
        outs.append(
            gt[:, g : g + 1] * o_cmp[sl]
            + gt[:, GROUP + g : GROUP + g + 1] * o_sel[sl]
            + gt[:, 2 * GROUP + g : 2 * GROUP + g + 1] * o_win[sl]
        )
    o_ref[...] = jnp.concatenate(outs, axis=1).astype(o_ref.dtype)


def _nsa_prompt(zmain, cos, sin, gq, gates, kc, vc, sel_bf, win_bf, n_b, t_len):
    tq = 128
    kt = min(512, t_len)
    nb = t_len // SEL_BLOCK
    n_slots = kc.shape[2]
    n_sel = min(N_SEL, nb)
    assert t_len % kt == 0 and t_len >= WINDOW + tq and nb <= LANES and n_slots == t_len // CMP_STRIDE
    ntq = t_len // tq
    m = n_b * t_len
    expand = (np.arange(t_len)[None, :] // SEL_BLOCK == np.arange(LANES)[:, None]).astype(np.float32)
    sel3 = sel_bf.reshape(n_b, t_len, ROW_W)
    win3 = win_bf.reshape(n_b, t_len, ROW_W)
    kspec = pl.BlockSpec((1, t_len, HEAD_DIM), lambda b, h, i: (b, 0, h))
    vspec = pl.BlockSpec((1, t_len, HEAD_DIM), lambda b, h, i: (b, 0, N_KV_HEADS + h))
    cspec = pl.BlockSpec((1, 1, n_slots, HEAD_DIM), lambda b, h, i: (b, h, 0, 0))
    tspec = pl.BlockSpec((tq, HEAD_DIM), lambda b, h, i: (i, 0))
    kern = functools.partial(
        _nsa_prompt_kernel, tq=tq, t_len=t_len, nb=nb, n_sel=n_sel, n_slots=n_slots, kt=kt)
    return pl.pallas_call(
        kern,
        grid=(n_b, N_KV_HEADS, ntq),
        in_specs=[
            pl.BlockSpec((tq, GROUP * HEAD_DIM), lambda b, h, i: (b * ntq + i, h)),
            tspec,
            tspec,
            pl.BlockSpec((1, HEAD_DIM), lambda b, h, i: (0, 0)),
            pl.BlockSpec((tq, LANES), lambda b, h, i: (b * ntq + i, h)),
            cspec,
            cspec,
            kspec,
            vspec,
            kspec,
            vspec,
            pl.BlockSpec((LANES, t_len), lambda b, h, i: (0, 0)),
        ],
        out_specs=pl.BlockSpec((tq, GROUP * HEAD_DIM), lambda b, h, i: (b * ntq + i, h)),
        out_shape=jax.ShapeDtypeStruct((m, Q_W), BF16),
        scratch_shapes=[pltpu.VMEM((tq, t_len), F32), pltpu.VMEM((GROUP * tq, HEAD_DIM), F32)],
        compiler_params=_params("parallel", "parallel", "arbitrary"),
        name="nsa_prompt",
    )(zmain, cos, sin, gq.reshape(1, HEAD_DIM), gates, kc, vc, sel3, sel3, win3, win3, jnp.asarray(expand, BF16))


def _layer_norm(y, g, b):
    yc = y - jnp.mean(y, axis=-1, keepdims=True)
    var = jnp.mean(yc * yc, axis=-1, keepdims=True)
    return yc * lax.rsqrt(var + EPS) * g + b


def _glu(z):
    c = z.shape[1] // 2
    return z[:, :c] * _sigmoid(z[:, c:])


def _conv_prompt_kernel(z_ref, halo_ref, wdw_ref, bdw_ref, gln_ref, bln_ref, wout_ref, y_ref, tail_ref,
                        ufull, yconv, *, tc, width):
    i = pl.program_id(1)
    u = _glu(z_ref[...])
    uh = jnp.where(i == 0, 0.0, _glu(halo_ref[...]))
    ufull[0:CONV_HALO, :] = uh
    ufull[CONV_HALO:, :] = u
    tail_ref[0] = u[tc - CONV_HALO :, :]
    ch = u.shape[1]
    rc, cc = 64, 512
    off = CONV_HALO - (width - 1)
    for r0 in range(0, tc, rc):
        for c0 in range(0, ch, cc):
            acc = jnp.zeros((rc, cc), F32)
            for k in range(width):
                acc = acc + wdw_ref[k : k + 1, c0 : c0 + cc] * ufull[r0 + off + k : r0 + off + k + rc, c0 : c0 + cc]
            yconv[r0 : r0 + rc, c0 : c0 + cc] = acc + bdw_ref[:, c0 : c0 + cc]
    y = _layer_norm(yconv[...], gln_ref[...], bln_ref[...])
    y = y * _sigmoid(y)
    y_ref[...] = jnp.dot(y.astype(BF16), wout_ref[...], preferred_element_type=F32)


def _conv_prompt(zmain, w_dw, b_dw, g_ln, b_ln, w_out, n_b, t_len):
    tc = 256
    width, ch = w_dw.shape
    d_out = w_out.shape[1]
    ntc = t_len // tc
    hb = tc // CONV_HALO
    m = n_b * t_len
    glu_blk = Q_W // (2 * ch)
    row1 = lambda b, i: (0, 0)
    kern = functools.partial(_conv_prompt_kernel, tc=tc, width=width)
    return pl.pallas_call(
        kern,
        grid=(n_b, ntc),
        in_specs=[
            pl.BlockSpec((tc, 2 * ch), lambda b, i: (b * ntc + i, glu_blk)),
            pl.BlockSpec((CONV_HALO, 2 * ch), lambda b, i: (jnp.maximum((b * ntc + i) * hb - 1, 0), glu_blk)),
            pl.BlockSpec((width, ch), row1),
            pl.BlockSpec((1, ch), row1),
            pl.BlockSpec((1, ch), row1),
            pl.BlockSpec((1, ch), row1),
            pl.BlockSpec((ch, d_out), row1),
        ],
        out_specs=[
            pl.BlockSpec((tc, d_out), lambda b, i: (b * ntc + i, 0)),
            pl.BlockSpec((1, CONV_HALO, ch), lambda b, i: (b, 0, 0)),
        ],
        out_shape=[jax.ShapeDtypeStruct((m, d_out), F32), jax.ShapeDtypeStruct((n_b, CONV_HALO, ch), F32)],
        scratch_shapes=[pltpu.VMEM((tc + CONV_HALO, ch), F32), pltpu.VMEM((tc, ch), F32)],
        compiler_params=_params("parallel", "arbitrary"),
        name="conv_prompt",
    )(zmain, zmain, w_dw, b_dw.reshape(1, ch), g_ln.reshape(1, ch), b_ln.reshape(1, ch), w_out)


def _conv_sample_kernel(z_ref, st_ref, wdw_ref, bdw_ref, gln_ref, bln_ref, wout_ref, y_ref, u_ref, *, width):
    u = _glu(z_ref[...])
    u_ref[...] = u
    y = jnp.sum(st_ref[...] * wdw_ref[0 : width - 1, :][None], axis=1) + wdw_ref[width - 1 : width, :] * u
    y = _layer_norm(y + bdw_ref[...], gln_ref[...], bln_ref[...])
    y = y * _sigmoid(y)
    y_ref[...] = jnp.dot(y.astype(BF16), wout_ref[...], preferred_element_type=F32)


def _conv_sample(zmain, state, w_dw, b_dw, g_ln, b_ln, w_out):
    width, ch = w_dw.shape
    n_b = state.shape[0]
    d_out = w_out.shape[1]
    glu_blk = Q_W // (2 * ch)
    c2 = lambda i: (0, 0)
    return pl.pallas_call(
        functools.partial(_conv_sample_kernel, width=width),
        grid=(1,),
        in_specs=[
            pl.BlockSpec((n_b, 2 * ch), lambda i: (0, glu_blk)),
            pl.BlockSpec((n_b, width - 1, ch), lambda i: (0, 0, 0)),
            pl.BlockSpec((width, ch), c2),
            pl.BlockSpec((1, ch), c2),
            pl.BlockSpec((1, ch), c2),
            pl.BlockSpec((1, ch), c2),
            pl.BlockSpec((ch, d_out), c2),
        ],
        out_specs=[pl.BlockSpec((n_b, d_out), c2), pl.BlockSpec((n_b, ch), c2)],
        out_shape=[jax.ShapeDtypeStruct((n_b, d_out), F32), jax.ShapeDtypeStruct((n_b, ch), F32)],
        compiler_params=_params("arbitrary"),
        name="conv_sample",
    )(zmain, state, w_dw, b_dw.reshape(1, ch), g_ln.reshape(1, ch), b_ln.reshape(1, ch), w_out)


def _attn_mix_kernel(o_ref, w_ref, gm_a_ref, gm_c_ref, yc_ref, mixed_ref):
    ya = jnp.dot(o_ref[...], w_ref[...], preferred_element_type=F32)
    mixed_ref[...] = (_sigmoid(gm_a_ref[...]) * ya + _sigmoid(gm_c_ref[...]) * yc_ref[...]).astype(mixed_ref.dtype)


def _attn_mix(o, w_attn, zmain, y_conv, tm):
    m, k = o.shape
    d = w_attn.shape[1]
    gm_blk = (zmain.shape[1] - 2 * d) // d
    return pl.pallas_call(
        _attn_mix_kernel,
        grid=(m // tm,),
        in_specs=[
            pl.BlockSpec((tm, k), lambda i: (i, 0)),
            pl.BlockSpec((k, d), lambda i: (0, 0)),
            pl.BlockSpec((tm, d), lambda i: (i, gm_blk)),
            pl.BlockSpec((tm, d), lambda i: (i, gm_blk + 1)),
            pl.BlockSpec((tm, d), lambda i: (i, 0)),
        ],
        out_specs=pl.BlockSpec((tm, d), lambda i: (i, 0)),
        out_shape=jax.ShapeDtypeStruct((m, d), BF16),
        compiler_params=_params("parallel"),
        name="attn_out_mix",
    )(o, w_attn, zmain, zmain, y_conv)


def _out_proj_kernel(mixed_ref, w_ref, x_ref, h_ref):
    h_ref[...] = x_ref[...] + jnp.dot(mixed_ref[...], w_ref[...], preferred_element_type=F32)


def _out_proj(mixed, w_out, x, tm):
    m, k = mixed.shape
    d = w_out.shape[1]
    return pl.pallas_call(
        _out_proj_kernel,
        grid=(m // tm,),
        in_specs=[
            pl.BlockSpec((tm, k), lambda i: (i, 0)),
            pl.BlockSpec((k, d), lambda i: (0, 0)),
            pl.BlockSpec((tm, d), lambda i: (i, 0)),
        ],
        out_specs=pl.BlockSpec((tm, d), lambda i: (i, 0)),
        out_shape=jax.ShapeDtypeStruct((m, d), F32),
        compiler_params=_params("parallel"),
        name="out_proj",
    )(mixed, w_out, x)


def _gelu(x):
    return 0.5 * x * (1.0 + lax.erf(x * np.float32(np.sqrt(0.5))))


def _ffn_kernel(*refs, tm, tiles_per_seq, sample):
    if sample:
        h_ref, g_ref, wa_ref, wb_ref, wdw_ref, bdw_ref, wd_ref, s0_ref, s1_ref, out_ref, a_ref, hn, acc = refs
    else:
        h_ref, halo_ref, g_ref, wa_ref, wb_ref, wdw_ref, bdw_ref, wd_ref, out_ref, a_ref, hn, acc = refs
    i = pl.program_id(0)
    j = pl.program_id(1)
    lead = 0 if sample else FFN_HALO

    @pl.when(j == 0)
    def _():
        hn[lead:, :] = (_rms(h_ref[...]) * g_ref[...]).astype(BF16)
        if not sample:
            hn[0:lead, :] = (_rms(halo_ref[...]) * g_ref[...]).astype(BF16)
        acc[...] = jnp.zeros_like(acc)

    a_ext = jnp.dot(hn[...], wa_ref[...], preferred_element_type=F32)
    gate = jnp.dot(hn[lead:, :], wb_ref[...], preferred_element_type=F32)
    a0 = a_ext[lead:, :]
    if sample:
        a_m2 = s0_ref[...]
        a_m1 = s1_ref[...]
        a_ref[...] = a0
    else:
        first = (i % tiles_per_seq) == 0
        r = lax.broadcasted_iota(jnp.int32, a0.shape, 0)
        a_m2 = jnp.where(first & (r < 2), 0.0, a_ext[lead - 2 : lead - 2 + tm, :])
        a_m1 = jnp.where(first & (r < 1), 0.0, a_ext[lead - 1 : lead - 1 + tm, :])
        a_ref[0] = a0[tm - SUBLANES :, :]
    conv = wdw_ref[0:1, :] * a_m2 + wdw_ref[1:2, :] * a_m1 + wdw_ref[2:3, :] * a0 + bdw_ref[...]
    y = (_gelu(conv) * gate).astype(BF16)
    acc[...] += jnp.dot(y, wd_ref[...], preferred_element_type=F32)

    @pl.when(j == pl.num_programs(1) - 1)
    def _():
        out_ref[...] = h_ref[...] + acc[...]


def _conv_ffn(h, g, w_up_a, w_up_b, w_dw, b_dw, w_down, *, tm, t_len=None, state=None):
    m, d = h.shape
    dff = w_up_a.shape[1]
    tf = 512
    nf = dff // tf
    sample = state is not None
    assert w_dw.shape[0] == 3
    col = lambda i, j: (0, j)
    common = [
        pl.BlockSpec((1, d), lambda i, j: (0, 0)),
        pl.BlockSpec((d, tf), col),
        pl.BlockSpec((d, tf), col),
        pl.BlockSpec((3, tf), col),
        pl.BlockSpec((1, tf), col),
        pl.BlockSpec((tf, d), lambda i, j: (j, 0)),
    ]
    args = [g.reshape(1, d), w_up_a, w_up_b, w_dw, b_dw.reshape(1, dff), w_down]
    hspec = pl.BlockSpec((tm, d), lambda i, j: (i, 0))
    if sample:
        assert m == tm
        st = state.reshape(m, 2 * dff)
        in_specs = [hspec] + common + [pl.BlockSpec((tm, tf), col), pl.BlockSpec((tm, tf), lambda i, j: (0, nf + j))]
        args = [h] + args + [st, st]
        a_spec = pl.BlockSpec((tm, tf), col)
        a_shape = jax.ShapeDtypeStruct((m, dff), F32)
        tiles_per_seq = 1
        lead = 0
    else:
        tiles_per_seq = t_len // tm
        hb = tm // FFN_HALO
        in_specs = [hspec, pl.BlockSpec((FFN_HALO, d), lambda i, j: (jnp.maximum(i * hb - 1, 0), 0))] + common
        args = [h, h] + args
        a_spec = pl.BlockSpec((1, SUBLANES, tf), lambda i, j: (i, 0, j))
        a_shape = jax.ShapeDtypeStruct((m // tm, SUBLANES, dff), F32)
        lead = FFN_HALO
    kern = functools.partial(_ffn_kernel, tm=tm, tiles_per_seq=tiles_per_seq, sample=sample)
    return pl.pallas_call(
        kern,
        grid=(m // tm, nf),
        in_specs=in_specs,
        out_specs=[hspec, a_spec],
        out_shape=[jax.ShapeDtypeStruct((m, d), F32), a_shape],
        scratch_shapes=[pltpu.VMEM((tm + lead, d), BF16), pltpu.VMEM((tm, d), F32)],
        compiler_params=_params("parallel", "arbitrary"),
        name="conv_ffn",
    )(*args)


def _ple_kernel(h_ref, g_ref, wg_ref, pe_ref, wp_ref, o_ref):
    h = h_ref[...]
    hn = (_rms(h) * g_ref[...]).astype(BF16)
    gate = _sigmoid(jnp.dot(hn, wg_ref[...], preferred_element_type=F32))
    proj = jnp.dot(pe_ref[...], wp_ref[...], preferred_element_type=F32)
    o_ref[...] = h + gate * proj


def _ple(h, g, w_gate, pe, w_proj, tm):
    m, d = h.shape
    pd = pe.shape[1]
    return pl.pallas_call(
        _ple_kernel,
        grid=(m // tm,),
        in_specs=[
            pl.BlockSpec((tm, d), lambda i: (i, 0)),
            pl.BlockSpec((1, d), lambda i: (0, 0)),
            pl.BlockSpec((d, d), lambda i: (0, 0)),
            pl.BlockSpec((tm, pd), lambda i: (i, 0)),
            pl.BlockSpec((pd, d), lambda i: (0, 0)),
        ],
        out_specs=pl.BlockSpec((tm, d), lambda i: (i, 0)),
        out_shape=jax.ShapeDtypeStruct((m, d), F32),
        compiler_params=_params("parallel"),
        name="ple",
    )(h, g.reshape(1, d), w_gate, pe, w_proj)


def _cmp_sample_kernel(q_ref, cos_ref, sin_ref, gq_ref, kc_ref, vc_ref, ov_ref, score_ref, ocmp_ref, qr_ref,
                       *, pos, nb, n_slots):
    scale = HEAD_DIM ** -0.5
    qn = _rms(q_ref[0]) * gq_ref[...]
    qr_ref[0] = _rope(qn, cos_ref[...], sin_ref[...])
    slot = lax.broadcasted_iota(jnp.int32, (GROUP, n_slots), 1)
    valid = (slot >= 1) & (slot * CMP_STRIDE + CMP_STRIDE <= pos + 1)
    sums = []
    for h in range(N_KV_HEADS):
        qh = qn[h * GROUP : (h + 1) * GROUP, :].astype(BF16)
        s = lax.dot_general(qh, kc_ref[0, h], NT_DIMS, preferred_element_type=F32) * scale
        e, l = _softmax_rows(jnp.where(valid, s, NEG_INF))
        p = jnp.where(valid, e / l, 0.0)
        ocmp_ref[0, h * GROUP : (h + 1) * GROUP, :] = jnp.dot(p.astype(BF16), vc_ref[0, h], preferred_element_type=F32)
        sums.append(jnp.sum(p, axis=0, keepdims=True))
    p_sum = jnp.concatenate(sums + [jnp.zeros((SUBLANES - N_KV_HEADS, n_slots), F32)], axis=0)
    imp = jnp.dot(p_sum, ov_ref[...], precision=lax.Precision.HIGHEST, preferred_element_type=F32)
    jb = lax.broadcasted_iota(jnp.int32, imp.shape, 1)
    cur = pos // SEL_BLOCK
    causal = (jb * SEL_BLOCK <= pos) & (jb < nb)
    forced = (jb == 0) | (jb == cur) | (jb == cur - 1)
    score_ref[0] = jnp.where(jb < nb, jnp.where(causal, jnp.where(forced, SEL_FORCE, imp), -1.0), -2.0)


def _cmp_sample(q3, cos, sin, gq, kc, vc, pos, nb):
    n_b = q3.shape[0]
    n_slots = kc.shape[2]
    nbp = -(-nb // LANES) * LANES
    ratio = SEL_BLOCK // CMP_STRIDE
    mm = np.arange(n_slots)[:, None]
    jj = np.arange(nbp)[None, :]
    overlap = ((mm >= ratio * jj) & (mm <= ratio * jj + ratio) & (jj < nb)).astype(np.float32)
    qspec = pl.BlockSpec((1, N_HEADS, HEAD_DIM), lambda b: (b, 0, 0))
    row = pl.BlockSpec((1, HEAD_DIM), lambda b: (0, 0))
    cspec = pl.BlockSpec((1, N_KV_HEADS, n_slots, HEAD_DIM), lambda b: (b, 0, 0, 0))
    return pl.pallas_call(
        functools.partial(_cmp_sample_kernel, pos=pos, nb=nb, n_slots=n_slots),
        grid=(n_b,),
        in_specs=[qspec, row, row, row, cspec, cspec, pl.BlockSpec((n_slots, nbp), lambda b: (0, 0))],
        out_specs=[pl.BlockSpec((1, SUBLANES, nbp), lambda b: (b, 0, 0)), qspec, qspec],
        out_shape=[
            jax.ShapeDtypeStruct((n_b, SUBLANES, nbp), F32),
            jax.ShapeDtypeStruct((n_b, N_HEADS, HEAD_DIM), F32),
            jax.ShapeDtypeStruct((n_b, N_HEADS, HEAD_DIM), F32),
        ],
        compiler_params=_params("parallel"),
        name="cmp_sample",
    )(q3, cos, sin, gq.reshape(1, HEAD_DIM), kc, vc, jnp.asarray(overlap))


def _topk_kernel(score_ref, idx_ref, *, n_sel):
    sc = score_ref[...]
    lane = lax.broadcasted_iota(jnp.int32, sc.shape, 1)
    out_lane = lax.broadcasted_iota(jnp.int32, idx_ref.shape, 1)
    out = jnp.zeros(idx_ref.shape, jnp.int32)
    for it in range(n_sel):
        mx = jnp.max(sc, axis=-1, keepdims=True)
        idx = jnp.min(jnp.where(sc == mx, lane, sc.shape[1]), axis=-1, keepdims=True)
        out = jnp.where(out_lane == it, idx, out)
        sc = jnp.where(lane == idx, -3.0, sc)
    idx_ref[...] = out


def _topk(score, n_sel):
    rows = score.shape[0]
    return pl.pallas_call(
        functools.partial(_topk_kernel, n_sel=n_sel),
        out_shape=jax.ShapeDtypeStruct((rows, LANES), jnp.int32),
        compiler_params=pltpu.CompilerParams(vmem_limit_bytes=VMEM_LIMIT),
        name="topk_sample",
    )(score)


def _sel_sample_kernel(idx_ref, pt_ref, *refs, n_sel, n_past_blocks):
    del pt_ref
    k_refs = refs[:n_sel]
    v_refs = refs[n_sel : 2 * n_sel]
    qr_ref, new_ref, o_ref = refs[2 * n_sel :]
    b = pl.program_id(0)
    h = pl.program_id(1)
    scale = HEAD_DIM ** -0.5
    qr = qr_ref[0, 0].astype(BF16)
    k = jnp.concatenate([r[0] for r in k_refs], axis=0).astype(BF16)
    v = jnp.concatenate([r[0] for r in v_refs], axis=0).astype(BF16)
    s = lax.dot_general(qr, k, NT_DIMS, preferred_element_type=F32) * scale
    lane_blk = lax.shift_right_logical(lax.broadcasted_iota(jnp.int32, s.shape, 1), SEL_SHIFT)
    bias = jnp.zeros(s.shape, F32)
    n_new = jnp.int32(0)
    for j in range(n_sel):
        is_new = idx_ref[(b * N_KV_HEADS + h) * n_sel + j] >= n_past_blocks
        bias = jnp.where(lane_blk == j, jnp.where(is_new, NEG_INF, 0.0), bias)
        n_new = n_new + jnp.where(is_new, 1, 0)
    s = s + bias
    k_new = new_ref[0, pl.ds(h, 1), :].astype(BF16).astype(F32)
    v_new = new_ref[0, pl.ds(N_KV_HEADS + h, 1), :].astype(BF16).astype(F32)
    s_new = jnp.sum(qr.astype(F32) * k_new, axis=-1, keepdims=True) * scale
    s_new = jnp.where(n_new > 0, s_new, NEG_INF)
    m = jnp.maximum(jnp.max(s, axis=-1, keepdims=True), s_new)
    e = jnp.exp(s - m)
    e_new = jnp.exp(s_new - m)
    l = jnp.sum(e, axis=-1, keepdims=True) + e_new
    o = jnp.dot(e.astype(BF16), v, preferred_element_type=F32) + e_new.astype(BF16).astype(F32) * v_new
    o_ref[0, 0] = o / l


def _sel_sample(idx_flat, pt_flat, pages, qr4, new_rows, n_sel, n_past_blocks, n_pp):
    n_b = qr4.shape[0]
    per_page = PAGE_SIZE // SEL_BLOCK

    def blk(b, h, idx, j):
        return jnp.minimum(idx[(b * N_KV_HEADS + h) * n_sel + j], n_past_blocks - 1)

    def kspec(j, lane_off):
        def imap(b, h, idx, pt):
            blk_id = blk(b, h, idx, j)
            return (pt[b * n_pp + blk_id // per_page], blk_id % per_page, lane_off + h)
        return pl.BlockSpec((1, SEL_BLOCK, HEAD_DIM), imap)

    grid_spec = pltpu.PrefetchScalarGridSpec(
        num_scalar_prefetch=2,
        grid=(n_b, N_KV_HEADS),
        in_specs=[kspec(j, 0) for j in range(n_sel)]
        + [kspec(j, N_KV_HEADS) for j in range(n_sel)]
        + [
            pl.BlockSpec((1, 1, GROUP, HEAD_DIM), lambda b, h, idx, pt: (b, h, 0, 0)),
            pl.BlockSpec((1, 2 * N_KV_HEADS, HEAD_DIM), lambda b, h, idx, pt: (b, 0, 0)),
        ],
        out_specs=pl.BlockSpec((1, 1, GROUP, HEAD_DIM), lambda b, h, idx, pt: (b, h, 0, 0)),
    )
    return pl.pallas_call(
        functools.partial(_sel_sample_kernel, n_sel=n_sel, n_past_blocks=n_past_blocks),
        grid_spec=grid_spec,
        out_shape=jax.ShapeDtypeStruct((n_b, N_KV_HEADS, GROUP, HEAD_DIM), F32),
        compiler_params=_params("parallel", "arbitrary"),
        name="sel_sample",
    )(idx_flat, pt_flat, *([pages] * (2 * n_sel)), qr4, new_rows)


def _win_sample_kernel(win_ref, new_ref, qr_ref, ocmp_ref, osel_ref, g_ref, nw_ref, o_ref, *, wb):
    scale = HEAD_DIM ** -0.5
    x = win_ref[0]
    row = lax.broadcasted_iota(jnp.int32, x.shape, 0)
    kv = jnp.where(row == wb - 1, new_ref[0], pltpu.roll(x, wb - 1, 0))
    nw_ref[0] = kv
    qr = qr_ref[0].astype(BF16)
    outs = []
    for h in range(N_KV_HEADS):
        k = kv[:, h * HEAD_DIM : (h + 1) * HEAD_DIM].astype(BF16)
        v = kv[:, KV_W + h * HEAD_DIM : KV_W + (h + 1) * HEAD_DIM].astype(BF16)
        s = lax.dot_general(qr[h * GROUP : (h + 1) * GROUP], k, NT_DIMS, preferred_element_type=F32) * scale
        e, l = _softmax_rows(s)
        outs.append(jnp.dot(e.astype(BF16), v, preferred_element_type=F32) / l)
    o_win = jnp.concatenate(outs, axis=0)
    g = g_ref[0]
    o = g[:, 0:1] * ocmp_ref[0] + g[:, 1:2] * osel_ref[0] + g[:, 2:3] * o_win
    o_ref[0] = o.astype(o_ref.dtype)


def _win_sample(win, new_rows, qr3, o_cmp, o_sel, g3):
    n_b, wb, _ = win.shape
    hspec = pl.BlockSpec((1, N_HEADS, HEAD_DIM), lambda b: (b, 0, 0))
    wspec = pl.BlockSpec((1, wb, ROW_W), lambda b: (b, 0, 0))
    return pl.pallas_call(
        functools.partial(_win_sample_kernel, wb=wb),
        grid=(n_b,),
        in_specs=[
            wspec,
            pl.BlockSpec((1, 1, ROW_W), lambda b: (b, 0, 0)),
            hspec,
            hspec,
            hspec,
            pl.BlockSpec((1, N_HEADS, 3), lambda b: (b, 0, 0)),
        ],
        out_specs=[wspec, hspec],
        out_shape=[jax.ShapeDtypeStruct((n_b, wb, ROW_W), F32), jax.ShapeDtypeStruct((n_b, N_HEADS, HEAD_DIM), BF16)],
        compiler_params=_params("parallel"),
        name="win_sample",
    )(win, new_rows, qr3, o_cmp, o_sel, g3)


def _rope_tables(pos):
    half = HEAD_DIM // 2
    inv = ROPE_THETA ** (-jnp.arange(half, dtype=F32) / half)
    ang = pos.astype(F32)[:, None] * inv
    cos = jnp.cos(ang)
    sin = jnp.sin(ang)
    return jnp.concatenate([cos, cos], axis=-1), jnp.concatenate([-sin, sin], axis=-1)


def _prep_weights(lw):
    w_in = lw["w_in"]
    c = 0
    parts = {}
    conv_ch = lw["w_conv_dw"].shape[1]
    d_model = w_in.shape[0]
    for name, width in (("q", Q_W), ("cmp", ROW_W), ("sel", ROW_W), ("win", ROW_W), ("gnsa", 3 * N_HEADS),
                        ("glu", 2 * conv_ch), ("gm", 2 * d_model)):
        parts[name] = w_in[:, c : c + width]
        c += width
    wg = parts["gnsa"].reshape(d_model, 3, N_KV_HEADS, GROUP).transpose(0, 2, 1, 3).reshape(d_model, N_KV_HEADS, 3 * GROUP)
    wg = jnp.pad(wg, ((0, 0), (0, 0), (0, LANES - 3 * GROUP))).reshape(d_model, N_KV_HEADS * LANES)
    w1 = lw["w_cmp1"]
    hid = w1.shape[-1]
    r = CMP_BLOCK // CMP_STRIDE
    w1cat = w1.reshape(2, r, CMP_STRIDE, HEAD_DIM, hid).transpose(0, 2, 3, 1, 4).reshape(2, CMP_STRIDE * HEAD_DIM, r * hid)
    dff = lw["w_up"].shape[1] // 2
    return {
        "w_main": jnp.concatenate([parts["q"], parts["glu"], parts["gm"]], axis=1).astype(BF16),
        "w_cmp_gate": jnp.concatenate([parts["cmp"], wg], axis=1).astype(BF16),
        "w_sel": parts["sel"].astype(BF16),
        "w_win": parts["win"].astype(BF16),
        "w1cat": w1cat.astype(BF16),
        "w2": lw["w_cmp2"].astype(BF16),
        "w_attn_out": lw["w_attn_out"].astype(BF16),
        "w_conv_out": lw["w_conv_out"].astype(BF16),
        "w_out": lw["w_out"].astype(BF16),
        "w_up_a": lw["w_up"][:, :dff].astype(BF16),
        "w_up_b": lw["w_up"][:, dff:].astype(BF16),
        "w_down": lw["w_down"].astype(BF16),
        "w_ple_gate": lw["w_ple_gate"].astype(BF16),
        "w_ple_proj": lw["w_ple_proj"].astype(BF16),
    }


def _dense_tail(x, o, zmain, y_conv, pe, lw, pw, *, tm_mix, tm, t_len=None, ffn_state=None):
    mixed = _attn_mix(o, pw["w_attn_out"], zmain, y_conv, tm_mix)
    h1 = _out_proj(mixed, pw["w_out"], x, tm)
    h2, a_tail = _conv_ffn(h1, lw["g_ffn"], pw["w_up_a"], pw["w_up_b"], lw["w_ffn_dw"], lw["b_ffn_dw"], pw["w_down"],
                           tm=tm, t_len=t_len, state=ffn_state)
    h3 = _ple(h2, lw["g_ple"], pw["w_ple_gate"], pe.astype(BF16), pw["w_ple_proj"], tm)
    return h3, a_tail


def _layer_prompt(x3, pe3, lw, pw, bias_cmp):
    n_b, t_len, d = x3.shape
    m = n_b * t_len
    x = x3.reshape(m, d)
    cos, sin = _rope_tables(jnp.arange(t_len, dtype=jnp.int32))
    xn = _rmsnorm_bf16(x, lw["g_mix"], 512)
    zmain = _matmul(xn, pw["w_main"], 1024, 1024)
    rows_cmp, gates = _cmp_gate_proj(xn, pw["w_cmp_gate"], 512)
    rows_sel, sel_bf = _kv_proj(xn, pw["w_sel"], lw["g_k"][1], cos, sin, 512)
    rows_win, win_bf = _kv_proj(xn, pw["w_win"], lw["g_k"][2], cos, sin, 512)
    n_pp = t_len // PAGE_SIZE
    ident = jnp.arange(n_b * n_pp, dtype=jnp.int32).reshape(n_b, n_pp)
    kc, vc = _compress(rows_cmp.reshape(n_b * n_pp, PAGE_SIZE * 2 * N_KV_HEADS, HEAD_DIM), ident, pw["w1cat"], bias_cmp, pw["w2"],
                       lw["b_cmp2"], lw["g_k"][0])
    o = _nsa_prompt(zmain, cos, sin, lw["g_q"], gates, kc, vc, sel_bf, win_bf, n_b, t_len)
    y_conv, u_tail = _conv_prompt(zmain, lw["w_conv_dw"], lw["b_conv_dw"], lw["g_conv_ln"], lw["b_conv_ln"],
                                  pw["w_conv_out"], n_b, t_len)
    h3, a_tail = _dense_tail(x, o, zmain, y_conv, pe3.reshape(m, -1), lw, pw, tm_mix=256, tm=512, t_len=t_len)
    kvshape = (n_b, t_len, 2, N_KV_HEADS, HEAD_DIM)
    w_keep = min(WINDOW, t_len)
    conv_w = lw["w_conv_dw"].shape[0]
    state = (
        rows_cmp.reshape(kvshape),
        rows_sel.reshape(kvshape),
        rows_win.reshape(kvshape)[:, t_len - w_keep :],
        u_tail[:, CONV_HALO - (conv_w - 1) :],
        a_tail.reshape(n_b, -1, SUBLANES, a_tail.shape[-1])[:, -1, SUBLANES - (lw["w_ffn_dw"].shape[0] - 1) :],
    )
    return h3.reshape(n_b, t_len, d), state


def _layer_sample(x3, pe3, lw, pw, bias_cmp, cache_cmp, cache_sel, cache_win, conv_state, ffn_state, page_table):
    n_b, t_new, d = x3.shape
    assert t_new == 1
    n_pp = page_table.shape[1]
    past_len = n_pp * PAGE_SIZE
    pos = past_len
    wb = cache_win.shape[1]
    assert wb == WINDOW and past_len % SEL_BLOCK == 0
    x = x3.reshape(n_b, d)
    cos, sin = _rope_tables(jnp.full((n_b,), pos, dtype=jnp.int32))
    xn = _rmsnorm_bf16(x, lw["g_mix"], n_b)
    zmain = _matmul(xn, pw["w_main"], n_b, 1024)
    rows_cmp, gates = _cmp_gate_proj(xn, pw["w_cmp_gate"], n_b)
    rows_sel, _ = _kv_proj(xn, pw["w_sel"], lw["g_k"][1], cos, sin, n_b)
    rows_win, _ = _kv_proj(xn, pw["w_win"], lw["g_k"][2], cos, sin, n_b)
    n_pool = cache_cmp.shape[0]
    kc, vc = _compress(cache_cmp.reshape(n_pool, PAGE_SIZE * 2 * N_KV_HEADS, HEAD_DIM), page_table, pw["w1cat"], bias_cmp, pw["w2"],
                       lw["b_cmp2"], lw["g_k"][0])
    n_past_blocks = past_len // SEL_BLOCK
    nb = n_past_blocks + 1
    n_sel = min(N_SEL, nb)
    q3 = zmain[:, :Q_W].reshape(n_b, N_HEADS, HEAD_DIM)
    score, o_cmp, qr3 = _cmp_sample(q3, cos[0:1], sin[0:1], lw["g_q"], kc, vc, pos, nb)
    idx = _topk(score.reshape(n_b * SUBLANES, -1), n_sel)
    idx = idx.reshape(n_b, SUBLANES, LANES)[:, :N_KV_HEADS, :n_sel]
    o_sel = _sel_sample(idx.reshape(-1), page_table.reshape(-1), cache_sel.reshape(n_pool, PAGE_SIZE, ROW_W),
                        qr3.reshape(n_b, N_KV_HEADS, GROUP, HEAD_DIM), rows_sel.reshape(n_b, 2 * N_KV_HEADS, HEAD_DIM),
                        n_sel, n_past_blocks, n_pp)
    g3 = gates.reshape(n_b, N_KV_HEADS, LANES)[:, :, : 3 * GROUP].reshape(n_b, N_KV_HEADS, 3, GROUP)
    g3 = g3.transpose(0, 1, 3, 2).reshape(n_b, N_HEADS, 3)
    new_win, o = _win_sample(cache_win.reshape(n_b, wb, ROW_W), rows_win.reshape(n_b, 1, ROW_W), qr3, o_cmp,
                             o_sel.reshape(n_b, N_HEADS, HEAD_DIM), g3)
    y_conv, u = _conv_sample(zmain, conv_state, lw["w_conv_dw"], lw["b_conv_dw"], lw["g_conv_ln"], lw["b_conv_ln"],
                             pw["w_conv_out"])
    h3, a_new = _dense_tail(x, o.reshape(n_b, Q_W), zmain, y_conv, pe3.reshape(n_b, -1), lw, pw,
                            tm_mix=n_b, tm=n_b, ffn_state=ffn_state)
    kvshape = (n_b, 1, 2, N_KV_HEADS, HEAD_DIM)
    state = (
        rows_cmp.reshape(kvshape),
        rows_sel.reshape(kvshape),
        new_win.reshape(n_b, wb, 2, N_KV_HEADS, HEAD_DIM),
        jnp.concatenate([conv_state[:, 1:], u[:, None, :]], axis=1),
        jnp.concatenate([ffn_state[:, 1:], a_new[:, None, :]], axis=1),
    )
    return h3.reshape(n_b, 1, d), state


_LAYER_WEIGHTS = ("g_mix", "w_in", "g_q", "g_k", "w_cmp1", "b_cmp1", "w_cmp2", "b_cmp2", "pe_cmp", "w_attn_out",
                  "w_conv_dw", "b_conv_dw", "g_conv_ln", "b_conv_ln", "w_conv_out", "w_out", "g_ffn", "w_up",
                  "w_ffn_dw", "b_ffn_dw", "w_down", "g_ple", "w_ple_gate", "w_ple_proj")


def kernel(x_prompt, x_sample, p_prompt, p_sample, cache_cmp_kv, cache_sel_kv, cache_win_kv, state_conv, state_ffn_conv, page_table, g_mix, w_in, g_q, g_k, w_cmp1, b_cmp1, w_cmp2, b_cmp2, pe_cmp, w_attn_out, w_conv_dw, b_conv_dw, g_conv_ln, b_conv_ln, w_conv_out, w_out, g_ffn, w_up, w_ffn_dw, b_ffn_dw, w_down, g_ple, w_ple_gate, w_ple_proj):
    stacked = dict(zip(_LAYER_WEIGHTS, (g_mix, w_in, g_q, g_k, w_cmp1, b_cmp1, w_cmp2, b_cmp2, pe_cmp, w_attn_out,
                                        w_conv_dw, b_conv_dw, g_conv_ln, b_conv_ln, w_conv_out, w_out, g_ffn, w_up,
                                        w_ffn_dw, b_ffn_dw, w_down, g_ple, w_ple_gate, w_ple_proj)))
    depth = w_in.shape[0]
    hp, hs = x_prompt, x_sample
    states_p, states_s = [], []
    for i in range(depth):
        lw = {k: v[i] for k, v in stacked.items()}
        pw = _prep_weights(lw)
        bias_cmp = _pe_bias(lw["pe_cmp"], lw["w_cmp1"], lw["b_cmp1"])
        hp, st_p = _layer_prompt(hp, p_prompt[i], lw, pw, bias_cmp)
        hs, st_s = _layer_sample(hs, p_sample[i], lw, pw, bias_cmp, cache_cmp_kv[i], cache_sel_kv[i], cache_win_kv[i],
                                 state_conv[i], state_ffn_conv[i], page_table)
        states_p.append(st_p)
        states_s.append(st_s)
    outs = [hp, hs]
    for k in range(5):
        outs.append(jnp.stack([s[k] for s in states_p]))
        outs.append(jnp.stack([s[k] for s in states_s]))
    return tuple(outs)
```

```python
import functools

import numpy as np
import jax
import jax.numpy as jnp
from jax import lax
from jax.experimental import pallas as pl
from jax.experimental.pallas import tpu as pltpu

N_HEADS = 16
HEAD_DIM = 128
N_KV_HEADS = 4
GROUP = N_HEADS // N_KV_HEADS
Q_W = N_HEADS * HEAD_DIM
KV_W = N_KV_HEADS * HEAD_DIM
ROW_W = 2 * KV_W
HEADS_PER_ROW = 2 * N_KV_HEADS
CMP_BLOCK = 32
CMP_STRIDE = 16
SEL_BLOCK = 64
SEL_SHIFT = 6
N_SEL = 16
WINDOW = 512
PAGE_SIZE = 128
ROPE_THETA = 10000.0
EPS = 1e-6
NEG_INF = -1e30
SEL_FORCE = 1e6
CONV_HALO = 32
FFN_HALO = 16

LANES = 128
SUBLANES = 8
VMEM_LIMIT = 56 * 1024 * 1024

F32 = jnp.float32
BF16 = jnp.bfloat16
NT_DIMS = (((1,), (1,)), ((), ()))


def _params(*sem):
    return pltpu.CompilerParams(dimension_semantics=sem, vmem_limit_bytes=VMEM_LIMIT)


def _rms(x):
    return x * lax.rsqrt(jnp.mean(x * x, axis=-1, keepdims=True) + EPS)


def _rope(x, cos, sin_signed):
    return x * cos + pltpu.roll(x, HEAD_DIM // 2, 1) * sin_signed


def _sigmoid(x):
    return jax.nn.sigmoid(x)


def _softmax_rows(s):
    m = jnp.max(s, axis=-1, keepdims=True)
    e = jnp.exp(s - m)
    return e, jnp.sum(e, axis=-1, keepdims=True)


def _softmax_rows_scaled(s, scale):
    m = jnp.max(s, axis=-1, keepdims=True)
    e = jnp.exp2((s - m) * np.float32(scale * np.log2(np.e)))
    return e, jnp.sum(e, axis=-1, keepdims=True)


def _rmsnorm_kernel(x_ref, g_ref, o_ref):
    o_ref[...] = (_rms(x_ref[...]) * g_ref[...]).astype(o_ref.dtype)


def _rmsnorm_bf16(x, g, tm):
    m, d = x.shape
    return pl.pallas_call(
        _rmsnorm_kernel,
        grid=(m // tm,),
        in_specs=[pl.BlockSpec((tm, d), lambda i: (i, 0)), pl.BlockSpec((1, d), lambda i: (0, 0))],
        out_specs=pl.BlockSpec((tm, d), lambda i: (i, 0)),
        out_shape=jax.ShapeDtypeStruct((m, d), BF16),
        compiler_params=_params("parallel"),
        name="rmsnorm_in",
    )(x, g.reshape(1, d))


def _matmul_kernel(x_ref, w_ref, o_ref):
    o_ref[...] = jnp.dot(x_ref[...], w_ref[...], preferred_element_type=F32)


def _matmul(x, w, tm, tn):
    m, k = x.shape
    n = w.shape[1]
    return pl.pallas_call(
        _matmul_kernel,
        grid=(m // tm, n // tn),
        in_specs=[pl.BlockSpec((tm, k), lambda i, j: (i, 0)), pl.BlockSpec((k, tn), lambda i, j: (0, j))],
        out_specs=pl.BlockSpec((tm, tn), lambda i, j: (i, j)),
        out_shape=jax.ShapeDtypeStruct((m, n), F32),
        compiler_params=_params("parallel", "arbitrary"),
        name="in_proj_main",
    )(x, w)


def _store_rows(rows_ref, acc):
    tm = acc.shape[0]
    for j in range(HEADS_PER_ROW):
        rows_ref[pl.ds(j, tm, stride=HEADS_PER_ROW), :] = acc[:, j * HEAD_DIM : (j + 1) * HEAD_DIM]


def _cmp_gate_kernel(x_ref, w_ref, rows_ref, gate_ref):
    acc = jnp.dot(x_ref[...], w_ref[...], preferred_element_type=F32)
    _store_rows(rows_ref, acc[:, :ROW_W])
    gate_ref[...] = _sigmoid(acc[:, ROW_W:])


def _cmp_gate_proj(xn, w, tm):
    m, k = xn.shape
    n = w.shape[1]
    ng = n - ROW_W
    return pl.pallas_call(
        _cmp_gate_kernel,
        grid=(m // tm,),
        in_specs=[pl.BlockSpec((tm, k), lambda i: (i, 0)), pl.BlockSpec((k, n), lambda i: (0, 0))],
        out_specs=[pl.BlockSpec((tm * HEADS_PER_ROW, HEAD_DIM), lambda i: (i, 0)),
                   pl.BlockSpec((tm, ng), lambda i: (i, 0))],
        out_shape=[jax.ShapeDtypeStruct((m * HEADS_PER_ROW, HEAD_DIM), F32), jax.ShapeDtypeStruct((m, ng), F32)],
        compiler_params=_params("parallel"),
        name="in_proj_cmp_gates",
    )(xn, w)


def _kv_kernel(x_ref, w_ref, gk_ref, cos_ref, sin_ref, rows_ref, rows_bf_ref):
    acc = jnp.dot(x_ref[...], w_ref[...], preferred_element_type=F32)
    cos = cos_ref[...]
    sin = sin_ref[...]
    gk = gk_ref[...]
    tm = acc.shape[0]
    for h in range(N_KV_HEADS):
        sl = slice(h * HEAD_DIM, (h + 1) * HEAD_DIM)
        k = _rope(_rms(acc[:, sl]) * gk, cos, sin)
        rows_ref[pl.ds(h, tm, stride=HEADS_PER_ROW), :] = k
        rows_bf_ref[:, sl] = k.astype(BF16)
        sv = slice(KV_W + h * HEAD_DIM, KV_W + (h + 1) * HEAD_DIM)
        rows_ref[pl.ds(N_KV_HEADS + h, tm, stride=HEADS_PER_ROW), :] = acc[:, sv]
    rows_bf_ref[:, KV_W:] = acc[:, KV_W:].astype(BF16)


def _kv_proj(xn, w, gk, cos, sin, tm):
    m, k = xn.shape
    nt = cos.shape[0] // tm
    return pl.pallas_call(
        _kv_kernel,
        grid=(m // tm,),
        in_specs=[
            pl.BlockSpec((tm, k), lambda i: (i, 0)),
            pl.BlockSpec((k, ROW_W), lambda i: (0, 0)),
            pl.BlockSpec((1, HEAD_DIM), lambda i: (0, 0)),
            pl.BlockSpec((tm, HEAD_DIM), lambda i: (i % nt, 0)),
            pl.BlockSpec((tm, HEAD_DIM), lambda i: (i % nt, 0)),
        ],
        out_specs=[pl.BlockSpec((tm * HEADS_PER_ROW, HEAD_DIM), lambda i: (i, 0)),
                   pl.BlockSpec((tm, ROW_W), lambda i: (i, 0))],
        out_shape=[jax.ShapeDtypeStruct((m * HEADS_PER_ROW, HEAD_DIM), F32), jax.ShapeDtypeStruct((m, ROW_W), BF16)],
        compiler_params=_params("parallel"),
        name="in_proj_kv",
    )(xn, w, gk.reshape(1, HEAD_DIM), cos, sin)


def _pe_bias_kernel(pe_ref, w1_ref, b1_ref, o_ref):
    for kv in range(2):
        acc = jnp.zeros((SUBLANES, HEAD_DIM), F32)
        for r in range(CMP_BLOCK):
            row = jnp.broadcast_to(pe_ref[kv, r : r + 1, :], (SUBLANES, HEAD_DIM)).astype(BF16)
            acc = acc + jnp.dot(
                row, w1_ref[kv, r * HEAD_DIM : (r + 1) * HEAD_DIM, :].astype(BF16), preferred_element_type=F32
            )
        o_ref[kv] = acc[0:1, :] + b1_ref[kv]


def _pe_bias(pe, w1, b1):
    hid = w1.shape[-1]
    return pl.pallas_call(
        _pe_bias_kernel,
        out_shape=jax.ShapeDtypeStruct((2, 1, hid), F32),
        compiler_params=pltpu.CompilerParams(vmem_limit_bytes=VMEM_LIMIT),
        name="cmp_pe_bias",
    )(pe, w1, b1.reshape(2, 1, hid))


def _compress_kernel(pt_ref, *refs, n_pg):
    del pt_ref
    page_refs = refs[:n_pg]
    w1_ref, bias_ref, w2_ref, b2_ref, gk_ref, kc_ref, vc_ref, carry_ref, out_scr = refs[n_pg:]
    sub_per_page = PAGE_SIZE // CMP_STRIDE
    nsub = n_pg * sub_per_page
    rows = nsub * HEADS_PER_ROW
    page_rows = sub_per_page * HEADS_PER_ROW

    @pl.when(pl.program_id(1) == 0)
    def _():
        carry_ref[...] = jnp.zeros_like(carry_ref)

    acc = None
    for pp in range(0, CMP_STRIDE, 2):
        pieces = []
        for pg in range(n_pg):
            a = page_refs[pg][0, pl.ds(pp, sub_per_page, stride=CMP_STRIDE), :, :].reshape(page_rows, HEAD_DIM)
            b = page_refs[pg][0, pl.ds(pp + 1, sub_per_page, stride=CMP_STRIDE), :, :].reshape(page_rows, HEAD_DIM)
            pieces.append(jnp.concatenate([a, b], axis=1))
        x = jnp.concatenate(pieces, axis=0).astype(BF16)
        d = jnp.dot(x, w1_ref[pp * HEAD_DIM : (pp + 2) * HEAD_DIM, :], preferred_element_type=F32)
        acc = d if acc is None else acc + d
    hid2 = acc.shape[1] // 2
    hid_w = hid2 // 2
    is_k = (lax.broadcasted_iota(jnp.int32, (rows, 1), 0) & (HEADS_PER_ROW - 1)) < N_KV_HEADS
    p = jnp.where(is_k, acc[:, :hid2], acc[:, hid2:])
    p_first = p[:, :hid_w]
    p_second = p[:, hid_w:]
    shifted = jnp.concatenate([carry_ref[...], p_first[: rows - HEADS_PER_ROW, :]], axis=0)
    carry_ref[...] = p_first[rows - HEADS_PER_ROW :, :]
    hid = shifted + p_second + jnp.where(is_k, bias_ref[0], bias_ref[1])
    hid = hid * _sigmoid(hid)
    o2 = jnp.dot(hid.astype(BF16), w2_ref[...], preferred_element_type=F32)
    out_k = o2[:, :HEAD_DIM] + b2_ref[0]
    out_v = o2[:, HEAD_DIM:] + b2_ref[1]
    out_scr[...] = jnp.where(is_k, _rms(out_k) * gk_ref[...], out_v)
    for j in range(HEADS_PER_ROW):
        dst = kc_ref if j < N_KV_HEADS else vc_ref
        dst[0, j % N_KV_HEADS, :, :] = out_scr[pl.ds(j, nsub, stride=HEADS_PER_ROW), :].astype(BF16)


def _compress(pages, page_table, w1all, bias, w2all, b2, gk):
    n_seq, n_pp = page_table.shape
    n_pg = max(d for d in range(1, 17) if n_pp % d == 0)
    steps = n_pp // n_pg
    nsub = n_pg * (PAGE_SIZE // CMP_STRIDE)
    n_slots = steps * nsub
    hid = bias.shape[-1]

    def page_spec(k):
        return pl.BlockSpec((1, PAGE_SIZE, HEADS_PER_ROW, HEAD_DIM),
                            lambda b, s, pt: (pt[b * n_pp + s * n_pg + k], 0, 0, 0))

    const2 = lambda b, s, pt: (0, 0)
    const3 = lambda b, s, pt: (0, 0, 0)
    out_spec = pl.BlockSpec((1, N_KV_HEADS, nsub, HEAD_DIM), lambda b, s, pt: (b, 0, s, 0))
    grid_spec = pltpu.PrefetchScalarGridSpec(
        num_scalar_prefetch=1,
        grid=(n_seq, steps),
        in_specs=[page_spec(k) for k in range(n_pg)]
        + [
            pl.BlockSpec(w1all.shape, const2),
            pl.BlockSpec((2, 1, hid), const3),
            pl.BlockSpec(w2all.shape, const2),
            pl.BlockSpec((2, 1, HEAD_DIM), const3),
            pl.BlockSpec((1, HEAD_DIM), const2),
        ],
        out_specs=[out_spec, out_spec],
        scratch_shapes=[pltpu.VMEM((HEADS_PER_ROW, hid), F32), pltpu.VMEM((nsub * HEADS_PER_ROW, HEAD_DIM), F32)],
    )
    shape = jax.ShapeDtypeStruct((n_seq, N_KV_HEADS, n_slots, HEAD_DIM), BF16)
    return pl.pallas_call(
        functools.partial(_compress_kernel, n_pg=n_pg),
        grid_spec=grid_spec,
        out_shape=[shape, shape],
        compiler_params=_params("parallel", "arbitrary"),
        name="compress",
    )(page_table.reshape(-1), *([pages] * n_pg), w1all, bias, w2all, b2.reshape(2, 1, HEAD_DIM), gk.reshape(1, HEAD_DIM))


def _nsa_prompt_kernel(
    zq_ref, cos_ref, sin_ref, gq_ref, gate_ref, kc_ref, vc_ref, ks_ref, vs_ref, kw_ref, vw_ref, expand_ref,
    o_ref, bias_scr, osel_scr, *, tq, t_len, nb, n_sel, n_slots, kt):
    i = pl.program_id(2)
    q0 = i * tq
    rows = GROUP * tq
    scale = HEAD_DIM ** -0.5
    cos = cos_ref[...]
    sin = sin_ref[...]
    gq = gq_ref[...]
    qs, qrs = [], []
    for g in range(GROUP):
        qn = _rms(zq_ref[:, g * HEAD_DIM : (g + 1) * HEAD_DIM]) * gq
        qs.append(qn.astype(BF16))
        qrs.append(_rope(qn, cos, sin).astype(BF16))
    q = jnp.concatenate(qs, axis=0)
    qr = jnp.concatenate(qrs, axis=0)

    s_t = lax.dot_general(kc_ref[0, 0], q, NT_DIMS, preferred_element_type=F32) * scale
    slot = lax.broadcasted_iota(jnp.int32, (n_slots, rows), 0)
    tok = q0 + (lax.broadcasted_iota(jnp.int32, (n_slots, rows), 1) & (tq - 1))
    valid = (slot >= 1) & (slot * CMP_STRIDE + CMP_STRIDE <= tok + 1)
    s_m = jnp.where(valid, s_t, NEG_INF)
    e = jnp.exp(s_m - jnp.max(s_m, axis=0, keepdims=True))
    p_t = jnp.where(valid, e / jnp.sum(e, axis=0, keepdims=True), 0.0)
    o_cmp = jnp.dot(p_t.T.astype(BF16), vc_ref[0, 0], preferred_element_type=F32)

    p_sum = p_t[:, 0:tq]
    for g in range(1, GROUP):
        p_sum = p_sum + p_t[:, g * tq : (g + 1) * tq]
    ratio = SEL_BLOCK // CMP_STRIDE
    jj = lax.broadcasted_iota(jnp.int32, (nb, n_slots), 0)
    mm = lax.broadcasted_iota(jnp.int32, (nb, n_slots), 1)
    overlap = jnp.where((mm >= ratio * jj) & (mm <= ratio * jj + ratio), 1.0, 0.0)
    imp = jnp.dot(overlap, p_sum, precision=lax.Precision.HIGHEST, preferred_element_type=F32)
    jb = lax.broadcasted_iota(jnp.int32, (nb, tq), 0)
    pos = q0 + lax.broadcasted_iota(jnp.int32, (nb, tq), 1)
    cur = lax.shift_right_logical(pos, SEL_SHIFT)
    causal = jb * SEL_BLOCK <= pos
    forced = (jb == 0) | (jb == cur) | (jb == cur - 1)
    score = jnp.where(causal, jnp.where(forced, SEL_FORCE, imp), -1.0)
    cnt = jnp.zeros((nb, tq), F32)
    for i2 in range(nb):
        r = score[i2 : i2 + 1, :]
        cnt = cnt + jnp.where(r > score, 1.0, jnp.where((r == score) & (jb > i2), 1.0, 0.0))
    sel = jnp.where((cnt < n_sel) & causal, 1.0, 0.0)
    if nb < LANES:
        sel = jnp.concatenate([sel, jnp.zeros((LANES - nb, tq), F32)], axis=0)
    key_sel = jnp.dot(sel.T.astype(BF16), expand_ref[...], preferred_element_type=F32)
    kpos = lax.broadcasted_iota(jnp.int32, (tq, t_len), 1)
    qpos = q0 + lax.broadcasted_iota(jnp.int32, (tq, t_len), 0)
    bias_scr[...] = jnp.where((key_sel > 0.5) & (kpos <= qpos), 0.0, NEG_INF)

    n_tiles = (q0 + tq + kt - 1) // kt
    for nt in range(1, t_len // kt + 1):

        @pl.when(n_tiles == nt)
        def _(nt=nt):
            ln = nt * kt
            s = lax.dot_general(qr, ks_ref[0, 0:ln, :], NT_DIMS, preferred_element_type=F32)
            s = (s.reshape(GROUP, tq, ln) + bias_scr[:, 0:ln][None]).reshape(rows, ln)
            e2, l2 = _softmax_rows_scaled(s, scale)
            osel_scr[...] = jnp.dot(e2.astype(BF16), vs_ref[0, 0:ln, :], preferred_element_type=F32) / l2

    lw = WINDOW + tq
    kw0 = pl.multiple_of(jnp.maximum(q0 - WINDOW, 0), tq)
    s = lax.dot_general(qr, kw_ref[0, pl.ds(kw0, lw), :], NT_DIMS, preferred_element_type=F32)
    diff = (q0 + lax.broadcasted_iota(jnp.int32, (tq, lw), 0)) - (kw0 + lax.broadcasted_iota(jnp.int32, (tq, lw), 1))
    wbias = jnp.where((diff >= 0) & (diff < WINDOW), 0.0, NEG_INF)
    s = (s.reshape(GROUP, tq, lw) + wbias[None]).reshape(rows, lw)
    e3, l3 = _softmax_rows_scaled(s, scale)
    o_win = jnp.dot(e3.astype(BF16), vw_ref[0, pl.ds(kw0, lw), :], preferred_element_type=F32) / l3

    o_sel = osel_scr[...]
    gt = gate_ref[...]
    outs = []
    for g in range(GROUP):
        sl = slice(g * tq, (g + 1) * tq)
        outs.append(
            gt[:, g : g + 1] * o_cmp[sl]
            + gt[:, GROUP + g : GROUP + g + 1] * o_sel[sl]
            + gt[:, 2 * GROUP + g : 2 * GROUP + g + 1] * o_win[sl]
        )
    o_ref[...] = jnp.concatenate(outs, axis=1).astype(o_ref.dtype)


def _nsa_prompt(zmain, cos, sin, gq, gates, kc, vc, sel_bf, win_bf, n_b, t_len):
    tq = 128
    kt = min(512, t_len)
    nb = t_len // SEL_BLOCK
    n_slots = kc.shape[2]
    n_sel = min(N_SEL, nb)
    assert t_len % kt == 0 and t_len >= WINDOW + tq and nb <= LANES and n_slots == t_len // CMP_STRIDE
    ntq = t_len // tq
    m = n_b * t_len
    expand = (np.arange(t_len)[None, :] // SEL_BLOCK == np.arange(LANES)[:, None]).astype(np.float32)
    sel3 = sel_bf.reshape(n_b, t_len, ROW_W)
    win3 = win_bf.reshape(n_b, t_len, ROW_W)
    kspec = pl.BlockSpec((1, t_len, HEAD_DIM), lambda b, h, i: (b, 0, h))
    vspec = pl.BlockSpec((1, t_len, HEAD_DIM), lambda b, h, i: (b, 0, N_KV_HEADS + h))
    cspec = pl.BlockSpec((1, 1, n_slots, HEAD_DIM), lambda b, h, i: (b, h, 0, 0))
    tspec = pl.BlockSpec((tq, HEAD_DIM), lambda b, h, i: (i, 0))
    kern = functools.partial(
        _nsa_prompt_kernel, tq=tq, t_len=t_len, nb=nb, n_sel=n_sel, n_slots=n_slots, kt=kt)
    return pl.pallas_call(
        kern,
        grid=(n_b, N_KV_HEADS, ntq),
        in_specs=[
            pl.BlockSpec((tq, GROUP * HEAD_DIM), lambda b, h, i: (b * ntq + i, h)),
            tspec,
            tspec,
            pl.BlockSpec((1, HEAD_DIM), lambda b, h, i: (0, 0)),
            pl.BlockSpec((tq, LANES), lambda b, h, i: (b * ntq + i, h)),
            cspec,
            cspec,
            kspec,
            vspec,
            kspec,
            vspec,
            pl.BlockSpec((LANES, t_len), lambda b, h, i: (0, 0)),
        ],
        out_specs=pl.BlockSpec((tq, GROUP * HEAD_DIM), lambda b, h, i: (b * ntq + i, h)),
        out_shape=jax.ShapeDtypeStruct((m, Q_W), BF16),
        scratch_shapes=[pltpu.VMEM((tq, t_len), F32), pltpu.VMEM((GROUP * tq, HEAD_DIM), F32)],
        compiler_params=_params("parallel", "parallel", "arbitrary"),
        name="nsa_prompt",
    )(zmain, cos, sin, gq.reshape(1, HEAD_DIM), gates, kc, vc, sel3, sel3, win3, win3, jnp.asarray(expand, BF16))


def _layer_norm(y, g, b):
    yc = y - jnp.mean(y, axis=-1, keepdims=True)
    var = jnp.mean(yc * yc, axis=-1, keepdims=True)
    return yc * lax.rsqrt(var + EPS) * g + b


def _glu(z):
    c = z.shape[1] // 2
    return z[:, :c] * _sigmoid(z[:, c:])


def _conv_prompt_kernel(z_ref, halo_ref, wdw_ref, bdw_ref, gln_ref, bln_ref, wout_ref, y_ref, tail_ref,
                        ufull, yconv, *, tc, width):
    i = pl.program_id(1)
    u = _glu(z_ref[...])
    uh = jnp.where(i == 0, 0.0, _glu(halo_ref[...]))
    ufull[0:CONV_HALO, :] = uh
    ufull[CONV_HALO:, :] = u
    tail_ref[0] = u[tc - CONV_HALO :, :]
    ch = u.shape[1]
    rc, cc = 64, 512
    off = CONV_HALO - (width - 1)
    for r0 in range(0, tc, rc):
        for c0 in range(0, ch, cc):
            acc = jnp.zeros((rc, cc), F32)
            for k in range(width):
                acc = acc + wdw_ref[k : k + 1, c0 : c0 + cc] * ufull[r0 + off + k : r0 + off + k + rc, c0 : c0 + cc]
            yconv[r0 : r0 + rc, c0 : c0 + cc] = acc + bdw_ref[:, c0 : c0 + cc]
    y = _layer_norm(yconv[...], gln_ref[...], bln_ref[...])
    y = y * _sigmoid(y)
    y_ref[...] = jnp.dot(y.astype(BF16), wout_ref[...], preferred_element_type=F32)


def _conv_prompt(zmain, w_dw, b_dw, g_ln, b_ln, w_out, n_b, t_len):
    tc = 256
    width, ch = w_dw.shape
    d_out = w_out.shape[1]
    ntc = t_len // tc
    hb = tc // CONV_HALO
    m = n_b * t_len
    glu_blk = Q_W // (2 * ch)
    row1 = lambda b, i: (0, 0)
    kern = functools.partial(_conv_prompt_kernel, tc=tc, width=width)
    return pl.pallas_call(
        kern,
        grid=(n_b, ntc),
        in_specs=[
            pl.BlockSpec((tc, 2 * ch), lambda b, i: (b * ntc + i, glu_blk)),
            pl.BlockSpec((CONV_HALO, 2 * ch), lambda b, i: (jnp.maximum((b * ntc + i) * hb - 1, 0), glu_blk)),
            pl.BlockSpec((width, ch), row1),
            pl.BlockSpec((1, ch), row1),
            pl.BlockSpec((1, ch), row1),
            pl.BlockSpec((1, ch), row1),
            pl.BlockSpec((ch, d_out), row1),
        ],
        out_specs=[
            pl.BlockSpec((tc, d_out), lambda b, i: (b * ntc + i, 0)),
            pl.BlockSpec((1, CONV_HALO, ch), lambda b, i: (b, 0, 0)),
        ],
        out_shape=[jax.ShapeDtypeStruct((m, d_out), F32), jax.ShapeDtypeStruct((n_b, CONV_HALO, ch), F32)],
        scratch_shapes=[pltpu.VMEM((tc + CONV_HALO, ch), F32), pltpu.VMEM((tc, ch), F32)],
        compiler_params=_params("parallel", "arbitrary"),
        name="conv_prompt",
    )(zmain, zmain, w_dw, b_dw.reshape(1, ch), g_ln.reshape(1, ch), b_ln.reshape(1, ch), w_out)


def _conv_sample_kernel(z_ref, st_ref, wdw_ref, bdw_ref, gln_ref, bln_ref, wout_ref, y_ref, u_ref, *, width):
    u = _glu(z_ref[...])
    u_ref[...] = u
    y = jnp.sum(st_ref[...] * wdw_ref[0 : width - 1, :][None], axis=1) + wdw_ref[width - 1 : width, :] * u
    y = _layer_norm(y + bdw_ref[...], gln_ref[...], bln_ref[...])
    y = y * _sigmoid(y)
    y_ref[...] = jnp.dot(y.astype(BF16), wout_ref[...], preferred_element_type=F32)


def _conv_sample(zmain, state, w_dw, b_dw, g_ln, b_ln, w_out):
    width, ch = w_dw.shape
    n_b = state.shape[0]
    d_out = w_out.shape[1]
    glu_blk = Q_W // (2 * ch)
    c2 = lambda i: (0, 0)
    return pl.pallas_call(
        functools.partial(_conv_sample_kernel, width=width),
        grid=(1,),
        in_specs=[
            pl.BlockSpec((n_b, 2 * ch), lambda i: (0, glu_blk)),
            pl.BlockSpec((n_b, width - 1, ch), lambda i: (0, 0, 0)),
            pl.BlockSpec((width, ch), c2),
            pl.BlockSpec((1, ch), c2),
            pl.BlockSpec((1, ch), c2),
            pl.BlockSpec((1, ch), c2),
            pl.BlockSpec((ch, d_out), c2),
        ],
        out_specs=[pl.BlockSpec((n_b, d_out), c2), pl.BlockSpec((n_b, ch), c2)],
        out_shape=[jax.ShapeDtypeStruct((n_b, d_out), F32), jax.ShapeDtypeStruct((n_b, ch), F32)],
        compiler_params=_params("arbitrary"),
        name="conv_sample",
    )(zmain, state, w_dw, b_dw.reshape(1, ch), g_ln.reshape(1, ch), b_ln.reshape(1, ch), w_out)


def _attn_mix_kernel(o_ref, w_ref, gm_a_ref, gm_c_ref, yc_ref, mixed_ref):
    ya = jnp.dot(o_ref[...], w_ref[...], preferred_element_type=F32)
    mixed_ref[...] = (_sigmoid(gm_a_ref[...]) * ya + _sigmoid(gm_c_ref[...]) * yc_ref[...]).astype(mixed_ref.dtype)


def _attn_mix(o, w_attn, zmain, y_conv, tm):
    m, k = o.shape
    d = w_attn.shape[1]
    gm_blk = (zmain.shape[1] - 2 * d) // d
    return pl.pallas_call(
        _attn_mix_kernel,
        grid=(m // tm,),
        in_specs=[
            pl.BlockSpec((tm, k), lambda i: (i, 0)),
            pl.BlockSpec((k, d), lambda i: (0, 0)),
            pl.BlockSpec((tm, d), lambda i: (i, gm_blk)),
            pl.BlockSpec((tm, d), lambda i: (i, gm_blk + 1)),
            pl.BlockSpec((tm, d), lambda i: (i, 0)),
        ],
        out_specs=pl.BlockSpec((tm, d), lambda i: (i, 0)),
        out_shape=jax.ShapeDtypeStruct((m, d), BF16),
        compiler_params=_params("parallel"),
        name="attn_out_mix",
    )(o, w_attn, zmain, zmain, y_conv)


def _out_proj_kernel(mixed_ref, w_ref, x_ref, h_ref):
    h_ref[...] = x_ref[...] + jnp.dot(mixed_ref[...], w_ref[...], preferred_element_type=F32)


def _out_proj(mixed, w_out, x, tm):
    m, k = mixed.shape
    d = w_out.shape[1]
    return pl.pallas_call(
        _out_proj_kernel,
        grid=(m // tm,),
        in_specs=[
            pl.BlockSpec((tm, k), lambda i: (i, 0)),
            pl.BlockSpec((k, d), lambda i: (0, 0)),
            pl.BlockSpec((tm, d), lambda i: (i, 0)),
        ],
        out_specs=pl.BlockSpec((tm, d), lambda i: (i, 0)),
        out_shape=jax.ShapeDtypeStruct((m, d), F32),
        compiler_params=_params("parallel"),
        name="out_proj",
    )(mixed, w_out, x)


def _gelu(x):
    return 0.5 * x * (1.0 + lax.erf(x * np.float32(np.sqrt(0.5))))


def _ffn_kernel(*refs, tm, tiles_per_seq, sample):
    if sample:
        h_ref, g_ref, wa_ref, wb_ref, wdw_ref, bdw_ref, wd_ref, s0_ref, s1_ref, out_ref, a_ref, hn, acc = refs
    else:
        h_ref, halo_ref, g_ref, wa_ref, wb_ref, wdw_ref, bdw_ref, wd_ref, out_ref, a_ref, hn, acc = refs
    i = pl.program_id(0)
    j = pl.program_id(1)
    lead = 0 if sample else FFN_HALO

    @pl.when(j == 0)
    def _():
        hn[lead:, :] = (_rms(h_ref[...]) * g_ref[...]).astype(BF16)
        if not sample:
            hn[0:lead, :] = (_rms(halo_ref[...]) * g_ref[...]).astype(BF16)
        acc[...] = jnp.zeros_like(acc)

    a_ext = jnp.dot(hn[...], wa_ref[...], preferred_element_type=F32)
    gate = jnp.dot(hn[lead:, :], wb_ref[...], preferred_element_type=F32)
    a0 = a_ext[lead:, :]
    if sample:
        a_m2 = s0_ref[...]
        a_m1 = s1_ref[...]
        a_ref[...] = a0
    else:
        first = (i % tiles_per_seq) == 0
        r = lax.broadcasted_iota(jnp.int32, a0.shape, 0)
        a_m2 = jnp.where(first & (r < 2), 0.0, a_ext[lead - 2 : lead - 2 + tm, :])
        a_m1 = jnp.where(first & (r < 1), 0.0, a_ext[lead - 1 : lead - 1 + tm, :])
        a_ref[0] = a0[tm - SUBLANES :, :]
    conv = wdw_ref[0:1, :] * a_m2 + wdw_ref[1:2, :] * a_m1 + wdw_ref[2:3, :] * a0 + bdw_ref[...]
    y = (_gelu(conv) * gate).astype(BF16)
    acc[...] += jnp.dot(y, wd_ref[...], preferred_element_type=F32)

    @pl.when(j == pl.num_programs(1) - 1)
    def _():
        out_ref[...] = h_ref[...] + acc[...]


def _conv_ffn(h, g, w_up_a, w_up_b, w_dw, b_dw, w_down, *, tm, t_len=None, state=None):
    m, d = h.shape
    dff = w_up_a.shape[1]
    tf = 512
    nf = dff // tf
    sample = state is not None
    assert w_dw.shape[0] == 3
    col = lambda i, j: (0, j)
    common = [
        pl.BlockSpec((1, d), lambda i, j: (0, 0)),
        pl.BlockSpec((d, tf), col),
        pl.BlockSpec((d, tf), col),
        pl.BlockSpec((3, tf), col),
        pl.BlockSpec((1, tf), col),
        pl.BlockSpec((tf, d), lambda i, j: (j, 0)),
    ]
    args = [g.reshape(1, d), w_up_a, w_up_b, w_dw, b_dw.reshape(1, dff), w_down]
    hspec = pl.BlockSpec((tm, d), lambda i, j: (i, 0))
    if sample:
        assert m == tm
        st = state.reshape(m, 2 * dff)
        in_specs = [hspec] + common + [pl.BlockSpec((tm, tf), col), pl.BlockSpec((tm, tf), lambda i, j: (0, nf + j))]
        args = [h] + args + [st, st]
        a_spec = pl.BlockSpec((tm, tf), col)
        a_shape = jax.ShapeDtypeStruct((m, dff), F32)
        tiles_per_seq = 1
        lead = 0
    else:
        tiles_per_seq = t_len // tm
        hb = tm // FFN_HALO
        in_specs = [hspec, pl.BlockSpec((FFN_HALO, d), lambda i, j: (jnp.maximum(i * hb - 1, 0), 0))] + common
        args = [h, h] + args
        a_spec = pl.BlockSpec((1, SUBLANES, tf), lambda i, j: (i, 0, j))
        a_shape = jax.ShapeDtypeStruct((m // tm, SUBLANES, dff), F32)
        lead = FFN_HALO
    kern = functools.partial(_ffn_kernel, tm=tm, tiles_per_seq=tiles_per_seq, sample=sample)
    return pl.pallas_call(
        kern,
        grid=(m // tm, nf),
        in_specs=in_specs,
        out_specs=[hspec, a_spec],
        out_shape=[jax.ShapeDtypeStruct((m, d), F32), a_shape],
        scratch_shapes=[pltpu.VMEM((tm + lead, d), BF16), pltpu.VMEM((tm, d), F32)],
        compiler_params=_params("parallel", "arbitrary"),
        name="conv_ffn",
    )(*args)


def _ple_kernel(h_ref, g_ref, wg_ref, pe_ref, wp_ref, o_ref):
    h = h_ref[...]
    hn = (_rms(h) * g_ref[...]).astype(BF16)
    gate = _sigmoid(jnp.dot(hn, wg_ref[...], preferred_element_type=F32))
    proj = jnp.dot(pe_ref[...], wp_ref[...], preferred_element_type=F32)
    o_ref[...] = h + gate * proj


def _ple(h, g, w_gate, pe, w_proj, tm):
    m, d = h.shape
    pd = pe.shape[1]
    return pl.pallas_call(
        _ple_kernel,
        grid=(m // tm,),
        in_specs=[
            pl.BlockSpec((tm, d), lambda i: (i, 0)),
            pl.BlockSpec((1, d), lambda i: (0, 0)),
            pl.BlockSpec((d, d), lambda i: (0, 0)),
            pl.BlockSpec((tm, pd), lambda i: (i, 0)),
            pl.BlockSpec((pd, d), lambda i: (0, 0)),
        ],
        out_specs=pl.BlockSpec((tm, d), lambda i: (i, 0)),
        out_shape=jax.ShapeDtypeStruct((m, d), F32),
        compiler_params=_params("parallel"),
        name="ple",
    )(h, g.reshape(1, d), w_gate, pe, w_proj)


def _cmp_sample_kernel(q_ref, cos_ref, sin_ref, gq_ref, kc_ref, vc_ref, ov_ref, score_ref, ocmp_ref, qr_ref,
                       *, pos, nb, n_slots):
    scale = HEAD_DIM ** -0.5
    qn = _rms(q_ref[0]) * gq_ref[...]
    qr_ref[0] = _rope(qn, cos_ref[...], sin_ref[...])
    slot = lax.broadcasted_iota(jnp.int32, (GROUP, n_slots), 1)
    valid = (slot >= 1) & (slot * CMP_STRIDE + CMP_STRIDE <= pos + 1)
    sums = []
    for h in range(N_KV_HEADS):
        qh = qn[h * GROUP : (h + 1) * GROUP, :].astype(BF16)
        s = lax.dot_general(qh, kc_ref[0, h], NT_DIMS, preferred_element_type=F32) * scale
        e, l = _softmax_rows(jnp.where(valid, s, NEG_INF))
        p = jnp.where(valid, e / l, 0.0)
        ocmp_ref[0, h * GROUP : (h + 1) * GROUP, :] = jnp.dot(p.astype(BF16), vc_ref[0, h], preferred_element_type=F32)
        sums.append(jnp.sum(p, axis=0, keepdims=True))
    p_sum = jnp.concatenate(sums + [jnp.zeros((SUBLANES - N_KV_HEADS, n_slots), F32)], axis=0)
    imp = jnp.dot(p_sum, ov_ref[...], precision=lax.Precision.HIGHEST, preferred_element_type=F32)
    jb = lax.broadcasted_iota(jnp.int32, imp.shape, 1)
    cur = pos // SEL_BLOCK
    causal = (jb * SEL_BLOCK <= pos) & (jb < nb)
    forced = (jb == 0) | (jb == cur) | (jb == cur - 1)
    score_ref[0] = jnp.where(jb < nb, jnp.where(causal, jnp.where(forced, SEL_FORCE, imp), -1.0), -2.0)


def _cmp_sample(q3, cos, sin, gq, kc, vc, pos, nb):
    n_b = q3.shape[0]
    n_slots = kc.shape[2]
    nbp = -(-nb // LANES) * LANES
    ratio = SEL_BLOCK // CMP_STRIDE
    mm = np.arange(n_slots)[:, None]
    jj = np.arange(nbp)[None, :]
    overlap = ((mm >= ratio * jj) & (mm <= ratio * jj + ratio) & (jj < nb)).astype(np.float32)
    qspec = pl.BlockSpec((1, N_HEADS, HEAD_DIM), lambda b: (b, 0, 0))
    row = pl.BlockSpec((1, HEAD_DIM), lambda b: (0, 0))
    cspec = pl.BlockSpec((1, N_KV_HEADS, n_slots, HEAD_DIM), lambda b: (b, 0, 0, 0))
    return pl.pallas_call(
        functools.partial(_cmp_sample_kernel, pos=pos, nb=nb, n_slots=n_slots),
        grid=(n_b,),
        in_specs=[qspec, row, row, row, cspec, cspec, pl.BlockSpec((n_slots, nbp), lambda b: (0, 0))],
        out_specs=[pl.BlockSpec((1, SUBLANES, nbp), lambda b: (b, 0, 0)), qspec, qspec],
        out_shape=[
            jax.ShapeDtypeStruct((n_b, SUBLANES, nbp), F32),
            jax.ShapeDtypeStruct((n_b, N_HEADS, HEAD_DIM), F32),
            jax.ShapeDtypeStruct((n_b, N_HEADS, HEAD_DIM), F32),
        ],
        compiler_params=_params("parallel"),
        name="cmp_sample",
    )(q3, cos, sin, gq.reshape(1, HEAD_DIM), kc, vc, jnp.asarray(overlap))


def _topk_kernel(score_ref, idx_ref, *, n_sel):
    sc = score_ref[...]
    lane = lax.broadcasted_iota(jnp.int32, sc.shape, 1)
    out_lane = lax.broadcasted_iota(jnp.int32, idx_ref.shape, 1)
    out = jnp.zeros(idx_ref.shape, jnp.int32)
    for it in range(n_sel):
        mx = jnp.max(sc, axis=-1, keepdims=True)
        idx = jnp.min(jnp.where(sc == mx, lane, sc.shape[1]), axis=-1, keepdims=True)
        out = jnp.where(out_lane == it, idx, out)
        sc = jnp.where(lane == idx, -3.0, sc)
    idx_ref[...] = out


def _topk(score, n_sel):
    rows = score.shape[0]
    return pl.pallas_call(
        functools.partial(_topk_kernel, n_sel=n_sel),
        out_shape=jax.ShapeDtypeStruct((rows, LANES), jnp.int32),
        compiler_params=pltpu.CompilerParams(vmem_limit_bytes=VMEM_LIMIT),
        name="topk_sample",
    )(score)


def _sel_sample_kernel(idx_ref, pt_ref, *refs, n_sel, n_past_blocks):
    del pt_ref
    blk_refs = refs[:n_sel]
    qr_ref, new_ref, o_ref = refs[n_sel:]
    b = pl.program_id(0)
    h = pl.program_id(1)
    scale = HEAD_DIM ** -0.5
    blk_rows = SEL_BLOCK * HEADS_PER_ROW
    qr = qr_ref[0, 0].astype(BF16)
    rows = jnp.concatenate([r[...] for r in blk_refs], axis=0).astype(BF16)
    s = lax.dot_general(qr, rows, NT_DIMS, preferred_element_type=F32) * scale
    lane = lax.broadcasted_iota(jnp.int32, s.shape, 1)
    slot = lane // blk_rows
    bias = jnp.where((lane & (HEADS_PER_ROW - 1)) == h, 0.0, NEG_INF)
    n_new = jnp.int32(0)
    for j in range(n_sel):
        is_new = idx_ref[(b * N_KV_HEADS + h) * n_sel + j] >= n_past_blocks
        bias = jnp.where(slot == j, jnp.where(is_new, NEG_INF, bias), bias)
        n_new = n_new + jnp.where(is_new, 1, 0)
    s = s + bias
    k_new = new_ref[0, pl.ds(h, 1), :].astype(BF16).astype(F32)
    v_new = new_ref[0, pl.ds(N_KV_HEADS + h, 1), :].astype(BF16).astype(F32)
    s_new = jnp.sum(qr.astype(F32) * k_new, axis=-1, keepdims=True) * scale
    s_new = jnp.where(n_new > 0, s_new, NEG_INF)
    m = jnp.maximum(jnp.max(s, axis=-1, keepdims=True), s_new)
    e = jnp.exp(s - m)
    e_new = jnp.exp(s_new - m)
    l = jnp.sum(e, axis=-1, keepdims=True) + e_new
    e_v = pltpu.roll(e, N_KV_HEADS, 1)
    o = jnp.dot(e_v.astype(BF16), rows, preferred_element_type=F32) + e_new.astype(BF16).astype(F32) * v_new
    o_ref[0, 0] = o / l


def _sel_sample(idx_flat, pt_flat, cache_rows, qr4, new_rows, n_sel, n_past_blocks, n_pp):
    n_b = qr4.shape[0]
    per_page = PAGE_SIZE // SEL_BLOCK
    blk_rows = SEL_BLOCK * HEADS_PER_ROW

    def bspec(j):
        def imap(b, h, idx, pt):
            blk_id = jnp.minimum(idx[(b * N_KV_HEADS + h) * n_sel + j], n_past_blocks - 1)
            return (pt[b * n_pp + blk_id // per_page] * per_page + blk_id % per_page, 0)
        return pl.BlockSpec((blk_rows, HEAD_DIM), imap)

    grid_spec = pltpu.PrefetchScalarGridSpec(
        num_scalar_prefetch=2,
        grid=(n_b, N_KV_HEADS),
        in_specs=[bspec(j) for j in range(n_sel)]
        + [
            pl.BlockSpec((1, 1, GROUP, HEAD_DIM), lambda b, h, idx, pt: (b, h, 0, 0)),
            pl.BlockSpec((1, HEADS_PER_ROW, HEAD_DIM), lambda b, h, idx, pt: (b, 0, 0)),
        ],
        out_specs=pl.BlockSpec((1, 1, GROUP, HEAD_DIM), lambda b, h, idx, pt: (b, h, 0, 0)),
    )
    return pl.pallas_call(
        functools.partial(_sel_sample_kernel, n_sel=n_sel, n_past_blocks=n_past_blocks),
        grid_spec=grid_spec,
        out_shape=jax.ShapeDtypeStruct((n_b, N_KV_HEADS, GROUP, HEAD_DIM), F32),
        compiler_params=_params("parallel", "arbitrary"),
        name="sel_sample",
    )(idx_flat, pt_flat, *([cache_rows] * n_sel), qr4, new_rows)


def _head_lane_mask(shape, kv_head_of_row):
    lane = lax.broadcasted_iota(jnp.int32, shape, 1)
    return (lane & (HEADS_PER_ROW - 1)) == kv_head_of_row


def _win_sample_kernel(win_ref, new_ref, qr_ref, ocmp_ref, osel_ref, g_ref, nw_ref, o_ref, *, wb):
    scale = HEAD_DIM ** -0.5
    keep = (wb - 1) * HEADS_PER_ROW
    nw_ref[0, 0:keep, :] = win_ref[0, HEADS_PER_ROW:, :]
    nw_ref[0, keep:, :] = new_ref[0]
    rows = nw_ref[0].astype(BF16)
    qr = qr_ref[0].astype(BF16)
    s = lax.dot_general(qr, rows, NT_DIMS, preferred_element_type=F32) * scale
    kv_head = lax.broadcasted_iota(jnp.int32, s.shape, 0) // GROUP
    s = jnp.where(_head_lane_mask(s.shape, kv_head), s, NEG_INF)
    e, l = _softmax_rows(s)
    e_v = pltpu.roll(e, N_KV_HEADS, 1)
    o_win = jnp.dot(e_v.astype(BF16), rows, preferred_element_type=F32) / l
    g = g_ref[0]
    o = g[:, 0:1] * ocmp_ref[0] + g[:, 1:2] * osel_ref[0] + g[:, 2:3] * o_win
    o_ref[0] = o.astype(o_ref.dtype)


def _win_sample(win_rows, new_rows, qr3, o_cmp, o_sel, g3):
    n_b, wr, _ = win_rows.shape
    wb = wr // HEADS_PER_ROW
    hspec = pl.BlockSpec((1, N_HEADS, HEAD_DIM), lambda b: (b, 0, 0))
    wspec = pl.BlockSpec((1, wr, HEAD_DIM), lambda b: (b, 0, 0))
    return pl.pallas_call(
        functools.partial(_win_sample_kernel, wb=wb),
        grid=(n_b,),
        in_specs=[
            wspec,
            pl.BlockSpec((1, HEADS_PER_ROW, HEAD_DIM), lambda b: (b, 0, 0)),
            hspec,
            hspec,
            hspec,
            pl.BlockSpec((1, N_HEADS, 3), lambda b: (b, 0, 0)),
        ],
        out_specs=[wspec, hspec],
        out_shape=[jax.ShapeDtypeStruct((n_b, wr, HEAD_DIM), F32), jax.ShapeDtypeStruct((n_b, N_HEADS, HEAD_DIM), BF16)],
        compiler_params=_params("parallel"),
        name="win_sample",
    )(win_rows, new_rows, qr3, o_cmp, o_sel, g3)


def _rope_tables(pos):
    half = HEAD_DIM // 2
    inv = ROPE_THETA ** (-jnp.arange(half, dtype=F32) / half)
    ang = pos.astype(F32)[:, None] * inv
    cos = jnp.cos(ang)
    sin = jnp.sin(ang)
    return jnp.concatenate([cos, cos], axis=-1), jnp.concatenate([-sin, sin], axis=-1)


def _prep_weights(lw):
    w_in = lw["w_in"]
    c = 0
    parts = {}
    conv_ch = lw["w_conv_dw"].shape[1]
    d_model = w_in.shape[0]
    for name, width in (("q", Q_W), ("cmp", ROW_W), ("sel", ROW_W), ("win", ROW_W), ("gnsa", 3 * N_HEADS),
                        ("glu", 2 * conv_ch), ("gm", 2 * d_model)):
        parts[name] = w_in[:, c : c + width]
        c += width
    wg = parts["gnsa"].reshape(d_model, 3, N_KV_HEADS, GROUP).transpose(0, 2, 1, 3).reshape(d_model, N_KV_HEADS, 3 * GROUP)
    wg = jnp.pad(wg, ((0, 0), (0, 0), (0, LANES - 3 * GROUP))).reshape(d_model, N_KV_HEADS * LANES)
    w1 = lw["w_cmp1"]
    hid = w1.shape[-1]
    r = CMP_BLOCK // CMP_STRIDE
    w1all = w1.reshape(2, r, CMP_STRIDE, HEAD_DIM, hid).transpose(2, 3, 0, 1, 4).reshape(CMP_STRIDE * HEAD_DIM, 2 * r * hid)
    dff = lw["w_up"].shape[1] // 2
    return {
        "w_main": jnp.concatenate([parts["q"], parts["glu"], parts["gm"]], axis=1).astype(BF16),
        "w_cmp_gate": jnp.concatenate([parts["cmp"], wg], axis=1).astype(BF16),
        "w_sel": parts["sel"].astype(BF16),
        "w_win": parts["win"].astype(BF16),
        "w1all": w1all.astype(BF16),
        "w2all": jnp.concatenate([lw["w_cmp2"][0], lw["w_cmp2"][1]], axis=1).astype(BF16),
        "w_attn_out": lw["w_attn_out"].astype(BF16),
        "w_conv_out": lw["w_conv_out"].astype(BF16),
        "w_out": lw["w_out"].astype(BF16),
        "w_up_a": lw["w_up"][:, :dff].astype(BF16),
        "w_up_b": lw["w_up"][:, dff:].astype(BF16),
        "w_down": lw["w_down"].astype(BF16),
        "w_ple_gate": lw["w_ple_gate"].astype(BF16),
        "w_ple_proj": lw["w_ple_proj"].astype(BF16),
    }


def _dense_tail(x, o, zmain, y_conv, pe, lw, pw, *, tm_mix, tm, t_len=None, ffn_state=None):
    mixed = _attn_mix(o, pw["w_attn_out"], zmain, y_conv, tm_mix)
    h1 = _out_proj(mixed, pw["w_out"], x, tm)
    h2, a_tail = _conv_ffn(h1, lw["g_ffn"], pw["w_up_a"], pw["w_up_b"], lw["w_ffn_dw"], lw["b_ffn_dw"], pw["w_down"],
                           tm=tm, t_len=t_len, state=ffn_state)
    h3 = _ple(h2, lw["g_ple"], pw["w_ple_gate"], pe.astype(BF16), pw["w_ple_proj"], tm)
    return h3, a_tail


def _layer_prompt(x3, pe3, lw, pw, bias_cmp):
    n_b, t_len, d = x3.shape
    m = n_b * t_len
    x = x3.reshape(m, d)
    cos, sin = _rope_tables(jnp.arange(t_len, dtype=jnp.int32))
    xn = _rmsnorm_bf16(x, lw["g_mix"], 512)
    zmain = _matmul(xn, pw["w_main"], 1024, 1024)
    rows_cmp, gates = _cmp_gate_proj(xn, pw["w_cmp_gate"], 512)
    rows_sel, sel_bf = _kv_proj(xn, pw["w_sel"], lw["g_k"][1], cos, sin, 512)
    rows_win, win_bf = _kv_proj(xn, pw["w_win"], lw["g_k"][2], cos, sin, 512)
    n_pp = t_len // PAGE_SIZE
    ident = jnp.arange(n_b * n_pp, dtype=jnp.int32).reshape(n_b, n_pp)
    kc, vc = _compress(rows_cmp.reshape(n_b * n_pp, PAGE_SIZE, HEADS_PER_ROW, HEAD_DIM), ident, pw["w1all"], bias_cmp,
                       pw["w2all"], lw["b_cmp2"], lw["g_k"][0])
    o = _nsa_prompt(zmain, cos, sin, lw["g_q"], gates, kc, vc, sel_bf, win_bf, n_b, t_len)
    y_conv, u_tail = _conv_prompt(zmain, lw["w_conv_dw"], lw["b_conv_dw"], lw["g_conv_ln"], lw["b_conv_ln"],
                                  pw["w_conv_out"], n_b, t_len)
    h3, a_tail = _dense_tail(x, o, zmain, y_conv, pe3.reshape(m, -1), lw, pw, tm_mix=256, tm=512, t_len=t_len)
    kvshape = (n_b, t_len, 2, N_KV_HEADS, HEAD_DIM)
    w_keep = min(WINDOW, t_len)
    conv_w = lw["w_conv_dw"].shape[0]
    state = (
        rows_cmp.reshape(kvshape),
        rows_sel.reshape(kvshape),
        rows_win.reshape(n_b, t_len * HEADS_PER_ROW, HEAD_DIM)[:, (t_len - w_keep) * HEADS_PER_ROW :].reshape(
            n_b, w_keep, 2, N_KV_HEADS, HEAD_DIM),
        u_tail[:, CONV_HALO - (conv_w - 1) :],
        a_tail.reshape(n_b, -1, SUBLANES, a_tail.shape[-1])[:, -1, SUBLANES - (lw["w_ffn_dw"].shape[0] - 1) :],
    )
    return h3.reshape(n_b, t_len, d), state


def _layer_sample(x3, pe3, lw, pw, bias_cmp, cache_cmp, cache_sel, cache_win, conv_state, ffn_state, page_table):
    n_b, t_new, d = x3.shape
    assert t_new == 1
    n_pp = page_table.shape[1]
    past_len = n_pp * PAGE_SIZE
    pos = past_len
    wb = cache_win.shape[1]
    assert wb == WINDOW and past_len % SEL_BLOCK == 0
    x = x3.reshape(n_b, d)
    cos, sin = _rope_tables(jnp.full((n_b,), pos, dtype=jnp.int32))
    xn = _rmsnorm_bf16(x, lw["g_mix"], n_b)
    zmain = _matmul(xn, pw["w_main"], n_b, 1024)
    rows_cmp, gates = _cmp_gate_proj(xn, pw["w_cmp_gate"], n_b)
    rows_sel, _ = _kv_proj(xn, pw["w_sel"], lw["g_k"][1], cos, sin, n_b)
    rows_win, _ = _kv_proj(xn, pw["w_win"], lw["g_k"][2], cos, sin, n_b)
    n_pool = cache_cmp.shape[0]
    kc, vc = _compress(cache_cmp.reshape(n_pool, PAGE_SIZE, HEADS_PER_ROW, HEAD_DIM), page_table, pw["w1all"], bias_cmp,
                       pw["w2all"], lw["b_cmp2"], lw["g_k"][0])
    n_past_blocks = past_len // SEL_BLOCK
    nb = n_past_blocks + 1
    n_sel = min(N_SEL, nb)
    q3 = zmain[:, :Q_W].reshape(n_b, N_HEADS, HEAD_DIM)
    score, o_cmp, qr3 = _cmp_sample(q3, cos[0:1], sin[0:1], lw["g_q"], kc, vc, pos, nb)
    idx = _topk(score.reshape(n_b * SUBLANES, -1), n_sel)
    idx = idx.reshape(n_b, SUBLANES, LANES)[:, :N_KV_HEADS, :n_sel]
    o_sel = _sel_sample(idx.reshape(-1), page_table.reshape(-1),
                        cache_sel.reshape(n_pool * PAGE_SIZE * HEADS_PER_ROW, HEAD_DIM),
                        qr3.reshape(n_b, N_KV_HEADS, GROUP, HEAD_DIM), rows_sel.reshape(n_b, HEADS_PER_ROW, HEAD_DIM),
                        n_sel, n_past_blocks, n_pp)
    g3 = gates.reshape(n_b, N_KV_HEADS, LANES)[:, :, : 3 * GROUP].reshape(n_b, N_KV_HEADS, 3, GROUP)
    g3 = g3.transpose(0, 1, 3, 2).reshape(n_b, N_HEADS, 3)
    new_win, o = _win_sample(cache_win.reshape(n_b, wb * HEADS_PER_ROW, HEAD_DIM),
                             rows_win.reshape(n_b, HEADS_PER_ROW, HEAD_DIM), qr3, o_cmp,
                             o_sel.reshape(n_b, N_HEADS, HEAD_DIM), g3)
    y_conv, u = _conv_sample(zmain, conv_state, lw["w_conv_dw"], lw["b_conv_dw"], lw["g_conv_ln"], lw["b_conv_ln"],
                             pw["w_conv_out"])
    h3, a_new = _dense_tail(x, o.reshape(n_b, Q_W), zmain, y_conv, pe3.reshape(n_b, -1), lw, pw,
                            tm_mix=n_b, tm=n_b, ffn_state=ffn_state)
    kvshape = (n_b, 1, 2, N_KV_HEADS, HEAD_DIM)
    state = (
        rows_cmp.reshape(kvshape),
        rows_sel.reshape(kvshape),
        new_win.reshape(n_b, wb, 2, N_KV_HEADS, HEAD_DIM),
        jnp.concatenate([conv_state[:, 1:], u[:, None, :]], axis=1),
        jnp.concatenate([ffn_state[:, 1:], a_new[:, None, :]], axis=1),
    )
    return h3.reshape(n_b, 1, d), state


_LAYER_WEIGHTS = ("g_mix", "w_in", "g_q", "g_k", "w_cmp1", "b_cmp1", "w_cmp2", "b_cmp2", "pe_cmp", "w_attn_out",
                  "w_conv_dw", "b_conv_dw", "g_conv_ln", "b_conv_ln", "w_conv_out", "w_out", "g_ffn", "w_up",
                  "w_ffn_dw", "b_ffn_dw", "w_down", "g_ple", "w_ple_gate", "w_ple_proj")


def kernel(x_prompt, x_sample, p_prompt, p_sample, cache_cmp_kv, cache_sel_kv, cache_win_kv, state_conv, state_ffn_conv, page_table, g_mix, w_in, g_q, g_k, w_cmp1, b_cmp1, w_cmp2, b_cmp2, pe_cmp, w_attn_out, w_conv_dw, b_conv_dw, g_conv_ln, b_conv_ln, w_conv_out, w_out, g_ffn, w_up, w_ffn_dw, b_ffn_dw, w_down, g_ple, w_ple_gate, w_ple_proj):
    stacked = dict(zip(_LAYER_WEIGHTS, (g_mix, w_in, g_q, g_k, w_cmp1, b_cmp1, w_cmp2, b_cmp2, pe_cmp, w_attn_out,
                                        w_conv_dw, b_conv_dw, g_conv_ln, b_conv_ln, w_conv_out, w_out, g_ffn, w_up,
                                        w_ffn_dw, b_ffn_dw, w_down, g_ple, w_ple_gate, w_ple_proj)))
    depth = w_in.shape[0]
    hp, hs = x_prompt, x_sample
    states_p, states_s = [], []
    for i in range(depth):
        lw = {k: v[i] for k, v in stacked.items()}
        pw = _prep_weights(lw)
        bias_cmp = _pe_bias(lw["pe_cmp"], lw["w_cmp1"], lw["b_cmp1"])
        hp, st_p = _layer_prompt(hp, p_prompt[i], lw, pw, bias_cmp)
        hs, st_s = _layer_sample(hs, p_sample[i], lw, pw, bias_cmp, cache_cmp_kv[i], cache_sel_kv[i], cache_win_kv[i],
                                 state_conv[i], state_ffn_conv[i], page_table)
        states_p.append(st_p)
        states_s.append(st_s)
    outs = [hp, hs]
    for k in range(5):
        outs.append(jnp.stack([s[k] for s in states_p]))
        outs.append(jnp.stack([s[k] for s in states_s]))
    return tuple(outs)
```

```python
import functools

import numpy as np
import jax
import jax.numpy as jnp
from jax import lax
from jax.experimental import pallas as pl
from jax.experimental.pallas import tpu as pltpu

N_HEADS = 16
HEAD_DIM = 128
N_KV_HEADS = 4
GROUP = N_HEADS // N_KV_HEADS
Q_W = N_HEADS * HEAD_DIM
KV_W = N_KV_HEADS * HEAD_DIM
ROW_W = 2 * KV_W
HEADS_PER_ROW = 2 * N_KV_HEADS
CMP_BLOCK = 32
CMP_STRIDE = 16
SEL_BLOCK = 64
SEL_SHIFT = 6
N_SEL = 16
WINDOW = 512
PAGE_SIZE = 128
ROPE_THETA = 10000.0
EPS = 1e-6
NEG_INF = -1e30
SEL_FORCE = 1e6
CONV_HALO = 32
FFN_HALO = 16
SCORE_CHUNK = 512
ROW_TILE = 128

LANES = 128
SUBLANES = 8
VMEM_LIMIT = 56 * 1024 * 1024

F32 = jnp.float32
BF16 = jnp.bfloat16
NT_DIMS = (((1,), (1,)), ((), ()))


def _params(*sem):
    return pltpu.CompilerParams(dimension_semantics=sem, vmem_limit_bytes=VMEM_LIMIT)


def _rms(x):
    return x * lax.rsqrt(jnp.mean(x * x, axis=-1, keepdims=True) + EPS)


def _rope(x, cos, sin_signed):
    return x * cos + pltpu.roll(x, HEAD_DIM // 2, 1) * sin_signed


def _sigmoid(x):
    return jax.nn.sigmoid(x)


def _softmax_rows(s):
    m = jnp.max(s, axis=-1, keepdims=True)
    e = jnp.exp(s - m)
    return e, jnp.sum(e, axis=-1, keepdims=True)


def _softmax_rows_scaled(s, scale):
    m = jnp.max(s, axis=-1, keepdims=True)
    e = jnp.exp2((s - m) * np.float32(scale * np.log2(np.e)))
    return e, jnp.sum(e, axis=-1, keepdims=True)


def _rmsnorm_kernel(x_ref, g_ref, o_ref):
    o_ref[...] = (_rms(x_ref[...]) * g_ref[...]).astype(o_ref.dtype)


def _rmsnorm_bf16(x, g, tm):
    m, d = x.shape
    return pl.pallas_call(
        _rmsnorm_kernel,
        grid=(m // tm,),
        in_specs=[pl.BlockSpec((tm, d), lambda i: (i, 0)), pl.BlockSpec((1, d), lambda i: (0, 0))],
        out_specs=pl.BlockSpec((tm, d), lambda i: (i, 0)),
        out_shape=jax.ShapeDtypeStruct((m, d), BF16),
        compiler_params=_params("parallel"),
        name="rmsnorm_in",
    )(x, g.reshape(1, d))


def _matmul_kernel(x_ref, w_ref, o_ref):
    o_ref[...] = jnp.dot(x_ref[...], w_ref[...], preferred_element_type=F32)


def _matmul(x, w, tm, tn):
    m, k = x.shape
    n = w.shape[1]
    return pl.pallas_call(
        _matmul_kernel,
        grid=(m // tm, n // tn),
        in_specs=[pl.BlockSpec((tm, k), lambda i, j: (i, 0)), pl.BlockSpec((k, tn), lambda i, j: (0, j))],
        out_specs=pl.BlockSpec((tm, tn), lambda i, j: (i, j)),
        out_shape=jax.ShapeDtypeStruct((m, n), F32),
        compiler_params=_params("parallel", "arbitrary"),
        name="in_proj_main",
    )(x, w)


def _q_kernel(x_ref, w_ref, gq_ref, cos_ref, sin_ref, q_ref, qr_ref):
    cos = cos_ref[...]
    sin = sin_ref[...]
    gq = gq_ref[...]
    x = x_ref[...]
    cw = 2 * LANES
    for c0 in range(0, w_ref.shape[1], cw):
        acc = jnp.dot(x, w_ref[:, c0 : c0 + cw], preferred_element_type=F32)
        for h0 in range(0, cw, HEAD_DIM):
            sl = slice(c0 + h0, c0 + h0 + HEAD_DIM)
            qn = _rms(acc[:, h0 : h0 + HEAD_DIM]) * gq
            q_ref[:, sl] = qn.astype(q_ref.dtype)
            qr_ref[:, sl] = _rope(qn, cos, sin).astype(qr_ref.dtype)


def _q_proj(xn, w, gq, cos, sin, tm, out_dtype):
    m, k = xn.shape
    n = w.shape[1]
    nt = cos.shape[0] // tm
    ospec = pl.BlockSpec((tm, n), lambda i: (i, 0))
    tspec = pl.BlockSpec((tm, HEAD_DIM), lambda i: (i % nt, 0))
    return pl.pallas_call(
        _q_kernel,
        grid=(m // tm,),
        in_specs=[pl.BlockSpec((tm, k), lambda i: (i, 0)), pl.BlockSpec((k, n), lambda i: (0, 0)),
                  pl.BlockSpec((1, HEAD_DIM), lambda i: (0, 0)), tspec, tspec],
        out_specs=[ospec, ospec],
        out_shape=[jax.ShapeDtypeStruct((m, n), out_dtype), jax.ShapeDtypeStruct((m, n), out_dtype)],
        compiler_params=_params("parallel"),
        name="in_proj_q",
    )(xn, w, gq.reshape(1, HEAD_DIM), cos, sin)


def _store_rows(rows_ref, acc):
    tm = acc.shape[0]
    for j in range(HEADS_PER_ROW):
        rows_ref[pl.ds(j, tm, stride=HEADS_PER_ROW), :] = acc[:, j * HEAD_DIM : (j + 1) * HEAD_DIM]


def _cmp_gate_kernel(x_ref, w_ref, rows_ref, gate_ref):
    acc = jnp.dot(x_ref[...], w_ref[...], preferred_element_type=F32)
    _store_rows(rows_ref, acc[:, :ROW_W])
    gate_ref[...] = _sigmoid(acc[:, ROW_W:])


def _cmp_gate_proj(xn, w, tm):
    m, k = xn.shape
    n = w.shape[1]
    ng = n - ROW_W
    return pl.pallas_call(
        _cmp_gate_kernel,
        grid=(m // tm,),
        in_specs=[pl.BlockSpec((tm, k), lambda i: (i, 0)), pl.BlockSpec((k, n), lambda i: (0, 0))],
        out_specs=[pl.BlockSpec((tm * HEADS_PER_ROW, HEAD_DIM), lambda i: (i, 0)),
                   pl.BlockSpec((tm, ng), lambda i: (i, 0))],
        out_shape=[jax.ShapeDtypeStruct((m * HEADS_PER_ROW, HEAD_DIM), F32), jax.ShapeDtypeStruct((m, ng), F32)],
        compiler_params=_params("parallel"),
        name="in_proj_cmp_gates",
    )(xn, w)


def _kv_kernel(x_ref, w_ref, gk_ref, cos_ref, sin_ref, rows_ref, rows_bf_ref):
    acc = jnp.dot(x_ref[...], w_ref[...], preferred_element_type=F32)
    cos = cos_ref[...]
    sin = sin_ref[...]
    gk = gk_ref[...]
    tm = acc.shape[0]
    for h in range(N_KV_HEADS):
        sl = slice(h * HEAD_DIM, (h + 1) * HEAD_DIM)
        k = _rope(_rms(acc[:, sl]) * gk, cos, sin)
        rows_ref[pl.ds(h, tm, stride=HEADS_PER_ROW), :] = k
        rows_bf_ref[:, sl] = k.astype(BF16)
        sv = slice(KV_W + h * HEAD_DIM, KV_W + (h + 1) * HEAD_DIM)
        rows_ref[pl.ds(N_KV_HEADS + h, tm, stride=HEADS_PER_ROW), :] = acc[:, sv]
    rows_bf_ref[:, KV_W:] = acc[:, KV_W:].astype(BF16)


def _kv_proj(xn, w, gk, cos, sin, tm):
    m, k = xn.shape
    nt = cos.shape[0] // tm
    return pl.pallas_call(
        _kv_kernel,
        grid=(m // tm,),
        in_specs=[
            pl.BlockSpec((tm, k), lambda i: (i, 0)),
            pl.BlockSpec((k, ROW_W), lambda i: (0, 0)),
            pl.BlockSpec((1, HEAD_DIM), lambda i: (0, 0)),
            pl.BlockSpec((tm, HEAD_DIM), lambda i: (i % nt, 0)),
            pl.BlockSpec((tm, HEAD_DIM), lambda i: (i % nt, 0)),
        ],
        out_specs=[pl.BlockSpec((tm * HEADS_PER_ROW, HEAD_DIM), lambda i: (i, 0)),
                   pl.BlockSpec((tm, ROW_W), lambda i: (i, 0))],
        out_shape=[jax.ShapeDtypeStruct((m * HEADS_PER_ROW, HEAD_DIM), F32), jax.ShapeDtypeStruct((m, ROW_W), BF16)],
        compiler_params=_params("parallel"),
        name="in_proj_kv",
    )(xn, w, gk.reshape(1, HEAD_DIM), cos, sin)


def _pe_bias_kernel(pe_ref, w1_ref, b1_ref, o_ref):
    for kv in range(2):
        acc = jnp.zeros((SUBLANES, HEAD_DIM), F32)
        for r in range(CMP_BLOCK):
            row = jnp.broadcast_to(pe_ref[kv, r : r + 1, :], (SUBLANES, HEAD_DIM)).astype(BF16)
            acc = acc + jnp.dot(
                row, w1_ref[kv, r * HEAD_DIM : (r + 1) * HEAD_DIM, :].astype(BF16), preferred_element_type=F32
            )
        o_ref[kv] = acc[0:1, :] + b1_ref[kv]


def _pe_bias(pe, w1, b1):
    hid = w1.shape[-1]
    return pl.pallas_call(
        _pe_bias_kernel,
        out_shape=jax.ShapeDtypeStruct((2, 1, hid), F32),
        compiler_params=pltpu.CompilerParams(vmem_limit_bytes=VMEM_LIMIT),
        name="cmp_pe_bias",
    )(pe, w1, b1.reshape(2, 1, hid))


def _compress_kernel(pt_ref, *refs, n_pg):
    del pt_ref
    page_refs = refs[:n_pg]
    w1_ref, bias_ref, w2_ref, b2_ref, gk_ref, kc_ref, vc_ref, carry_ref, out_scr = refs[n_pg:]
    sub_per_page = PAGE_SIZE // CMP_STRIDE
    nsub = n_pg * sub_per_page
    rows = nsub * HEADS_PER_ROW
    page_rows = sub_per_page * HEADS_PER_ROW

    @pl.when(pl.program_id(1) == 0)
    def _():
        carry_ref[...] = jnp.zeros_like(carry_ref)

    half = rows // 2
    hid2 = w1_ref.shape[1] // 2
    hid_w = hid2 // 2
    top = lax.broadcasted_iota(jnp.int32, (nsub // 2, HEADS_PER_ROW, 2 * HEAD_DIM), 1) < N_KV_HEADS
    acc = [None, None]
    for pp in range(0, CMP_STRIDE, 2):
        pieces = []
        for pg in range(n_pg):
            a = page_refs[pg][0, pl.ds(pp, sub_per_page, stride=CMP_STRIDE), :, :].reshape(page_rows, HEAD_DIM)
            b = page_refs[pg][0, pl.ds(pp + 1, sub_per_page, stride=CMP_STRIDE), :, :].reshape(page_rows, HEAD_DIM)
            pieces.append(jnp.concatenate([a, b], axis=1))
        x3 = jnp.concatenate(pieces, axis=0).reshape(nsub // 2, 2 * HEADS_PER_ROW, 2 * HEAD_DIM)
        even = x3[:, :HEADS_PER_ROW]
        odd = x3[:, HEADS_PER_ROW:]
        xs = (jnp.where(top, even, pltpu.roll(odd, N_KV_HEADS, 1)),
              jnp.where(top, pltpu.roll(even, N_KV_HEADS, 1), odd))
        w1 = w1_ref[pp * HEAD_DIM : (pp + 2) * HEAD_DIM, :]
        for kv in range(2):
            d = jnp.dot(xs[kv].reshape(half, 2 * HEAD_DIM).astype(BF16), w1[:, kv * hid2 : (kv + 1) * hid2],
                        preferred_element_type=F32)
            acc[kv] = d if acc[kv] is None else acc[kv] + d
    first_rows = lax.broadcasted_iota(jnp.int32, (half, hid_w), 0) < N_KV_HEADS
    pad = jnp.zeros((half - SUBLANES, hid_w), F32)
    for kv in range(2):
        p_first = acc[kv][:, :hid_w]
        p_second = acc[kv][:, hid_w:]
        rolled = pltpu.roll(p_first, N_KV_HEADS, 0)
        shifted = jnp.where(first_rows, jnp.concatenate([carry_ref[kv], pad], axis=0), rolled)
        carry_ref[kv] = rolled[0:SUBLANES, :]
        hid = shifted + p_second + bias_ref[kv]
        hid = hid * _sigmoid(hid)
        out = jnp.dot(hid.astype(BF16), w2_ref[:, kv * HEAD_DIM : (kv + 1) * HEAD_DIM],
                      preferred_element_type=F32) + b2_ref[kv]
        if kv == 0:
            out = _rms(out) * gk_ref[...]
        out_scr[kv] = out
        dst = kc_ref if kv == 0 else vc_ref
        for h in range(N_KV_HEADS):
            dst[0, h, :, :] = out_scr[kv, pl.ds(h, nsub, stride=N_KV_HEADS), :].astype(BF16)


def _compress(pages, page_table, w1all, bias, w2all, b2, gk):
    n_seq, n_pp = page_table.shape
    n_pg = max(d for d in range(1, 17) if n_pp % d == 0)
    steps = n_pp // n_pg
    nsub = n_pg * (PAGE_SIZE // CMP_STRIDE)
    n_slots = steps * nsub
    hid = bias.shape[-1]

    def page_spec(k):
        return pl.BlockSpec((1, PAGE_SIZE, HEADS_PER_ROW, HEAD_DIM),
                            lambda b, s, pt: (pt[b * n_pp + s * n_pg + k], 0, 0, 0))

    const2 = lambda b, s, pt: (0, 0)
    const3 = lambda b, s, pt: (0, 0, 0)
    out_spec = pl.BlockSpec((1, N_KV_HEADS, nsub, HEAD_DIM), lambda b, s, pt: (b, 0, s, 0))
    grid_spec = pltpu.PrefetchScalarGridSpec(
        num_scalar_prefetch=1,
        grid=(n_seq, steps),
        in_specs=[page_spec(k) for k in range(n_pg)]
        + [
            pl.BlockSpec(w1all.shape, const2),
            pl.BlockSpec((2, 1, hid), const3),
            pl.BlockSpec(w2all.shape, const2),
            pl.BlockSpec((2, 1, HEAD_DIM), const3),
            pl.BlockSpec((1, HEAD_DIM), const2),
        ],
        out_specs=[out_spec, out_spec],
        scratch_shapes=[pltpu.VMEM((2, SUBLANES, hid), F32), pltpu.VMEM((2, nsub * N_KV_HEADS, HEAD_DIM), F32)],
    )
    shape = jax.ShapeDtypeStruct((n_seq, N_KV_HEADS, n_slots, HEAD_DIM), BF16)
    return pl.pallas_call(
        functools.partial(_compress_kernel, n_pg=n_pg),
        grid_spec=grid_spec,
        out_shape=[shape, shape],
        compiler_params=_params("parallel", "arbitrary"),
        name="compress",
    )(page_table.reshape(-1), *([pages] * n_pg), w1all, bias, w2all, b2.reshape(2, 1, HEAD_DIM), gk.reshape(1, HEAD_DIM))


def _nsa_prompt_kernel(
    q_ref, qr_ref, gate_ref, kc_ref, vc_ref, ks_ref, vs_ref, kw_ref, vw_ref, expand_ref,
    o_ref, bias_scr, osel_scr, s_scr, p_scr, owin_scr, sw_scr, pw_scr, *, tq, t_len, nb, n_sel, n_slots, kt):
    i = pl.program_id(2)
    q0 = i * tq
    rows = GROUP * tq
    scale = HEAD_DIM ** -0.5
    heads = [slice(g * HEAD_DIM, (g + 1) * HEAD_DIM) for g in range(GROUP)]
    q = jnp.concatenate([q_ref[:, sl] for sl in heads], axis=0)
    qr = jnp.concatenate([qr_ref[:, sl] for sl in heads], axis=0)

    def attend(k_ref, v_ref, k0, n_keys, bias_chunk, s_buf, p_buf):
        for c0 in range(0, n_keys, SCORE_CHUNK):
            w = min(SCORE_CHUNK, n_keys - c0)
            s = lax.dot_general(qr, k_ref[0, pl.ds(k0 + c0, w), :], NT_DIMS, preferred_element_type=F32)
            s_buf[:, c0 : c0 + w] = (s.reshape(GROUP, tq, w) + bias_chunk(c0, w)[None]).reshape(rows, w)
        coef = np.float32(scale * np.log2(np.e))
        outs = []
        for g in range(GROUP):
            sums = []
            for r0 in range(g * tq, (g + 1) * tq, ROW_TILE):
                r = slice(r0, r0 + ROW_TILE)
                m_acc = s_buf[r, 0:LANES]
                for c0 in range(LANES, n_keys, LANES):
                    m_acc = jnp.maximum(m_acc, s_buf[r, c0 : c0 + LANES])
                m_b = jnp.broadcast_to(jnp.max(m_acc, axis=-1, keepdims=True), (ROW_TILE, LANES))
                l_acc = jnp.zeros((ROW_TILE, LANES), F32)
                for c0 in range(0, n_keys, LANES):
                    p = jnp.exp2((s_buf[r, c0 : c0 + LANES] - m_b) * coef)
                    l_acc = l_acc + p
                    p_buf[r, c0 : c0 + LANES] = p.astype(BF16)
                sums.append(jnp.sum(l_acc, axis=-1, keepdims=True))
            rg = slice(g * tq, (g + 1) * tq)
            pv = jnp.dot(p_buf[rg, 0:n_keys], v_ref[0, pl.ds(k0, n_keys), :], preferred_element_type=F32)
            outs.append(pv / jnp.concatenate(sums, axis=0))
        return jnp.concatenate(outs, axis=0)

    lw = WINDOW + tq
    kw0 = pl.multiple_of(jnp.maximum(q0 - WINDOW, 0), tq)

    def window_bias(c0, w):
        diff = (q0 + lax.broadcasted_iota(jnp.int32, (tq, w), 0)) - (
            kw0 + c0 + lax.broadcasted_iota(jnp.int32, (tq, w), 1))
        return jnp.where((diff >= 0) & (diff < WINDOW), 0.0, NEG_INF)

    owin_scr[...] = attend(kw_ref, vw_ref, kw0, lw, window_bias, sw_scr, pw_scr)

    s_t = lax.dot_general(kc_ref[0, 0], q, NT_DIMS, preferred_element_type=F32) * scale
    slot = lax.broadcasted_iota(jnp.int32, (n_slots, rows), 0)
    tok = q0 + (lax.broadcasted_iota(jnp.int32, (n_slots, rows), 1) & (tq - 1))
    valid = (slot >= 1) & (slot * CMP_STRIDE + CMP_STRIDE <= tok + 1)
    s_m = jnp.where(valid, s_t, NEG_INF)
    e = jnp.exp(s_m - jnp.max(s_m, axis=0, keepdims=True))
    p_t = jnp.where(valid, e / jnp.sum(e, axis=0, keepdims=True), 0.0)
    o_cmp = jnp.dot(p_t.T.astype(BF16), vc_ref[0, 0], preferred_element_type=F32)

    p_sum = p_t[:, 0:tq]
    for g in range(1, GROUP):
        p_sum = p_sum + p_t[:, g * tq : (g + 1) * tq]
    ratio = SEL_BLOCK // CMP_STRIDE
    jj = lax.broadcasted_iota(jnp.int32, (nb, n_slots), 0)
    mm = lax.broadcasted_iota(jnp.int32, (nb, n_slots), 1)
    overlap = jnp.where((mm >= ratio * jj) & (mm <= ratio * jj + ratio), 1.0, 0.0)
    imp = jnp.dot(overlap, p_sum, precision=lax.Precision.HIGHEST, preferred_element_type=F32)
    jb = lax.broadcasted_iota(jnp.int32, (nb, tq), 0)
    pos = q0 + lax.broadcasted_iota(jnp.int32, (nb, tq), 1)
    cur = lax.shift_right_logical(pos, SEL_SHIFT)
    causal = jb * SEL_BLOCK <= pos
    forced = (jb == 0) | (jb == cur) | (jb == cur - 1)
    score = jnp.where(causal, jnp.where(forced, SEL_FORCE, imp), -1.0)
    cnt = jnp.zeros((nb, tq), F32)
    for i2 in range(nb):
        r = score[i2 : i2 + 1, :]
        cnt = cnt + jnp.where(r > score, 1.0, jnp.where((r == score) & (jb > i2), 1.0, 0.0))
    sel = jnp.where((cnt < n_sel) & causal, 1.0, 0.0)
    if nb < LANES:
        sel = jnp.concatenate([sel, jnp.zeros((LANES - nb, tq), F32)], axis=0)
    key_sel = jnp.dot(sel.T.astype(BF16), expand_ref[...], preferred_element_type=F32)
    kpos = lax.broadcasted_iota(jnp.int32, (tq, t_len), 1)
    qpos = q0 + lax.broadcasted_iota(jnp.int32, (tq, t_len), 0)
    bias_scr[...] = jnp.where((key_sel > 0.5) & (kpos <= qpos), 0.0, NEG_INF)

    n_tiles = (q0 + tq + kt - 1) // kt
    for nt in range(1, t_len // kt + 1):

        @pl.when(n_tiles == nt)
        def _(nt=nt):
            osel_scr[...] = attend(ks_ref, vs_ref, 0, nt * kt, lambda c0, w: bias_scr[:, c0 : c0 + w], s_scr, p_scr)

    o_sel = osel_scr[...]
    o_win = owin_scr[...]
    gt = gate_ref[...]
    outs = []
    for g in range(GROUP):
        sl = slice(g * tq, (g + 1) * tq)
        outs.append(
            gt[:, g : g + 1] * o_cmp[sl]
            + gt[:, GROUP + g : GROUP + g + 1] * o_sel[sl]
            + gt[:, 2 * GROUP + g : 2 * GROUP + g + 1] * o_win[sl]
        )
    o_ref[...] = jnp.concatenate(outs, axis=1).astype(o_ref.dtype)


def _nsa_prompt(q_bf, qr_bf, gates, kc, vc, sel_bf, win_bf, n_b, t_len):
    tq = 128
    kt = min(512, t_len)
    nb = t_len // SEL_BLOCK
    n_slots = kc.shape[2]
    n_sel = min(N_SEL, nb)
    assert t_len % kt == 0 and t_len >= WINDOW + tq and nb <= LANES and n_slots == t_len // CMP_STRIDE
    ntq = t_len // tq
    m = n_b * t_len
    expand = (np.arange(t_len)[None, :] // SEL_BLOCK == np.arange(LANES)[:, None]).astype(np.float32)
    sel3 = sel_bf.reshape(n_b, t_len, ROW_W)
    win3 = win_bf.reshape(n_b, t_len, ROW_W)
    kspec = pl.BlockSpec((1, t_len, HEAD_DIM), lambda b, h, i: (b, 0, h))
    vspec = pl.BlockSpec((1, t_len, HEAD_DIM), lambda b, h, i: (b, 0, N_KV_HEADS + h))
    cspec = pl.BlockSpec((1, 1, n_slots, HEAD_DIM), lambda b, h, i: (b, h, 0, 0))
    qspec = pl.BlockSpec((tq, GROUP * HEAD_DIM), lambda b, h, i: (b * ntq + i, h))
    kern = functools.partial(
        _nsa_prompt_kernel, tq=tq, t_len=t_len, nb=nb, n_sel=n_sel, n_slots=n_slots, kt=kt)
    return pl.pallas_call(
        kern,
        grid=(n_b, N_KV_HEADS, ntq),
        in_specs=[
            qspec,
            qspec,
            pl.BlockSpec((tq, LANES), lambda b, h, i: (b * ntq + i, h)),
            cspec,
            cspec,
            kspec,
            vspec,
            kspec,
            vspec,
            pl.BlockSpec((LANES, t_len), lambda b, h, i: (0, 0)),
        ],
        out_specs=qspec,
        out_shape=jax.ShapeDtypeStruct((m, Q_W), BF16),
        scratch_shapes=[pltpu.VMEM((tq, t_len), F32), pltpu.VMEM((GROUP * tq, HEAD_DIM), F32),
                        pltpu.VMEM((GROUP * tq, t_len), F32), pltpu.VMEM((GROUP * tq, t_len), BF16),
                        pltpu.VMEM((GROUP * tq, HEAD_DIM), F32),
                        pltpu.VMEM((GROUP * tq, WINDOW + tq), F32), pltpu.VMEM((GROUP * tq, WINDOW + tq), BF16)],
        compiler_params=_params("parallel", "parallel", "arbitrary"),
        name="nsa_prompt",
    )(q_bf, qr_bf, gates, kc, vc, sel3, sel3, win3, win3, jnp.asarray(expand, BF16))


def _layer_norm(y, g, b):
    yc = y - jnp.mean(y, axis=-1, keepdims=True)
    var = jnp.mean(yc * yc, axis=-1, keepdims=True)
    return yc * lax.rsqrt(var + EPS) * g + b


def _glu(z):
    c = z.shape[1] // 2
    return z[:, :c] * _sigmoid(z[:, c:])


def _conv_prompt_kernel(z_ref, halo_ref, wdw_ref, bdw_ref, gln_ref, bln_ref, wout_ref, y_ref, tail_ref,
                        ufull, yconv, *, tc, width):
    i = pl.program_id(1)
    u = _glu(z_ref[...])
    uh = jnp.where(i == 0, 0.0, _glu(halo_ref[...]))
    ufull[0:CONV_HALO, :] = uh
    ufull[CONV_HALO:, :] = u
    tail_ref[0] = u[tc - CONV_HALO :, :]
    ch = u.shape[1]
    rc, cc = 64, 256
    off = CONV_HALO - (width - 1)
    for r0 in range(0, tc, rc):
        for c0 in range(0, ch, cc):
            acc = jnp.zeros((rc, cc), F32)
            span_all = rc + CONV_HALO
            u_rows = ufull[r0 : r0 + span_all, c0 : c0 + cc]
            for b in range(min(SUBLANES, width)):
                n_a = (width - 1 - b) // SUBLANES + 1
                ub = pltpu.roll(u_rows, (span_all - off - b) % span_all, 0)
                for a in range(n_a):
                    k = SUBLANES * a + b
                    acc = acc + wdw_ref[k : k + 1, c0 : c0 + cc] * ub[SUBLANES * a : SUBLANES * a + rc, :]
            yconv[r0 : r0 + rc, c0 : c0 + cc] = acc + bdw_ref[:, c0 : c0 + cc]
    y = _layer_norm(yconv[...], gln_ref[...], bln_ref[...])
    y = y * _sigmoid(y)
    y_ref[...] = jnp.dot(y.astype(BF16), wout_ref[...], preferred_element_type=F32)


def _conv_prompt(zmain, w_dw, b_dw, g_ln, b_ln, w_out, n_b, t_len):
    tc = 256
    width, ch = w_dw.shape
    d_out = w_out.shape[1]
    ntc = t_len // tc
    hb = tc // CONV_HALO
    m = n_b * t_len
    glu_blk = 0
    row1 = lambda b, i: (0, 0)
    kern = functools.partial(_conv_prompt_kernel, tc=tc, width=width)
    return pl.pallas_call(
        kern,
        grid=(n_b, ntc),
        in_specs=[
            pl.BlockSpec((tc, 2 * ch), lambda b, i: (b * ntc + i, glu_blk)),
            pl.BlockSpec((CONV_HALO, 2 * ch), lambda b, i: (jnp.maximum((b * ntc + i) * hb - 1, 0), glu_blk)),
            pl.BlockSpec((width, ch), row1),
            pl.BlockSpec((1, ch), row1),
            pl.BlockSpec((1, ch), row1),
            pl.BlockSpec((1, ch), row1),
            pl.BlockSpec((ch, d_out), row1),
        ],
        out_specs=[
            pl.BlockSpec((tc, d_out), lambda b, i: (b * ntc + i, 0)),
            pl.BlockSpec((1, CONV_HALO, ch), lambda b, i: (b, 0, 0)),
        ],
        out_shape=[jax.ShapeDtypeStruct((m, d_out), F32), jax.ShapeDtypeStruct((n_b, CONV_HALO, ch), F32)],
        scratch_shapes=[pltpu.VMEM((tc + CONV_HALO, ch), F32), pltpu.VMEM((tc, ch), F32)],
        compiler_params=_params("parallel", "arbitrary"),
        name="conv_prompt",
    )(zmain, zmain, w_dw, b_dw.reshape(1, ch), g_ln.reshape(1, ch), b_ln.reshape(1, ch), w_out)


def _conv_sample_kernel(z_ref, st_ref, wdw_ref, bdw_ref, gln_ref, bln_ref, wout_ref, y_ref, u_ref, *, width):
    u = _glu(z_ref[...])
    u_ref[...] = u
    y = jnp.sum(st_ref[...] * wdw_ref[0 : width - 1, :][None], axis=1) + wdw_ref[width - 1 : width, :] * u
    y = _layer_norm(y + bdw_ref[...], gln_ref[...], bln_ref[...])
    y = y * _sigmoid(y)
    y_ref[...] = jnp.dot(y.astype(BF16), wout_ref[...], preferred_element_type=F32)


def _conv_sample(zmain, state, w_dw, b_dw, g_ln, b_ln, w_out):
    width, ch = w_dw.shape
    n_b = state.shape[0]
    d_out = w_out.shape[1]
    glu_blk = 0
    c2 = lambda i: (0, 0)
    return pl.pallas_call(
        functools.partial(_conv_sample_kernel, width=width),
        grid=(1,),
        in_specs=[
            pl.BlockSpec((n_b, 2 * ch), lambda i: (0, glu_blk)),
            pl.BlockSpec((n_b, width - 1, ch), lambda i: (0, 0, 0)),
            pl.BlockSpec((width, ch), c2),
            pl.BlockSpec((1, ch), c2),
            pl.BlockSpec((1, ch), c2),
            pl.BlockSpec((1, ch), c2),
            pl.BlockSpec((ch, d_out), c2),
        ],
        out_specs=[pl.BlockSpec((n_b, d_out), c2), pl.BlockSpec((n_b, ch), c2)],
        out_shape=[jax.ShapeDtypeStruct((n_b, d_out), F32), jax.ShapeDtypeStruct((n_b, ch), F32)],
        compiler_params=_params("arbitrary"),
        name="conv_sample",
    )(zmain, state, w_dw, b_dw.reshape(1, ch), g_ln.reshape(1, ch), b_ln.reshape(1, ch), w_out)


def _attn_mix_kernel(o_ref, w_ref, gm_a_ref, gm_c_ref, yc_ref, mixed_ref):
    ya = jnp.dot(o_ref[...], w_ref[...], preferred_element_type=F32)
    mixed_ref[...] = (_sigmoid(gm_a_ref[...]) * ya + _sigmoid(gm_c_ref[...]) * yc_ref[...]).astype(mixed_ref.dtype)


def _attn_mix(o, w_attn, zmain, y_conv, tm):
    m, k = o.shape
    d = w_attn.shape[1]
    gm_blk = (zmain.shape[1] - 2 * d) // d
    return pl.pallas_call(
        _attn_mix_kernel,
        grid=(m // tm,),
        in_specs=[
            pl.BlockSpec((tm, k), lambda i: (i, 0)),
            pl.BlockSpec((k, d), lambda i: (0, 0)),
            pl.BlockSpec((tm, d), lambda i: (i, gm_blk)),
            pl.BlockSpec((tm, d), lambda i: (i, gm_blk + 1)),
            pl.BlockSpec((tm, d), lambda i: (i, 0)),
        ],
        out_specs=pl.BlockSpec((tm, d), lambda i: (i, 0)),
        out_shape=jax.ShapeDtypeStruct((m, d), BF16),
        compiler_params=_params("parallel"),
        name="attn_out_mix",
    )(o, w_attn, zmain, zmain, y_conv)


def _out_proj_kernel(mixed_ref, w_ref, x_ref, h_ref):
    h_ref[...] = x_ref[...] + jnp.dot(mixed_ref[...], w_ref[...], preferred_element_type=F32)


def _out_proj(mixed, w_out, x, tm):
    m, k = mixed.shape
    d = w_out.shape[1]
    return pl.pallas_call(
        _out_proj_kernel,
        grid=(m // tm,),
        in_specs=[
            pl.BlockSpec((tm, k), lambda i: (i, 0)),
            pl.BlockSpec((k, d), lambda i: (0, 0)),
            pl.BlockSpec((tm, d), lambda i: (i, 0)),
        ],
        out_specs=pl.BlockSpec((tm, d), lambda i: (i, 0)),
        out_shape=jax.ShapeDtypeStruct((m, d), F32),
        compiler_params=_params("parallel"),
        name="out_proj",
    )(mixed, w_out, x)


def _gelu(x):
    return 0.5 * x * (1.0 + lax.erf(x * np.float32(np.sqrt(0.5))))


def _ffn_kernel(*refs, tm, tiles_per_seq, sample):
    if sample:
        h_ref, g_ref, wa_ref, wb_ref, wdw_ref, bdw_ref, wd_ref, s0_ref, s1_ref, out_ref, a_ref, hn, acc = refs
    else:
        h_ref, halo_ref, g_ref, wa_ref, wb_ref, wdw_ref, bdw_ref, wd_ref, out_ref, a_ref, hn, acc = refs
    i = pl.program_id(0)
    j = pl.program_id(1)
    lead = 0 if sample else FFN_HALO

    @pl.when(j == 0)
    def _():
        hn[lead:, :] = (_rms(h_ref[...]) * g_ref[...]).astype(BF16)
        if not sample:
            hn[0:lead, :] = (_rms(halo_ref[...]) * g_ref[...]).astype(BF16)
        acc[...] = jnp.zeros_like(acc)

    a_ext = jnp.dot(hn[...], wa_ref[...], preferred_element_type=F32)
    gate = jnp.dot(hn[lead:, :], wb_ref[...], preferred_element_type=F32)
    a0 = a_ext[lead:, :]
    if sample:
        a_m2 = s0_ref[...]
        a_m1 = s1_ref[...]
        a_ref[...] = a0
    else:
        first = (i % tiles_per_seq) == 0
        r = lax.broadcasted_iota(jnp.int32, a0.shape, 0)
        a_m2 = jnp.where(first & (r < 2), 0.0, a_ext[lead - 2 : lead - 2 + tm, :])
        a_m1 = jnp.where(first & (r < 1), 0.0, a_ext[lead - 1 : lead - 1 + tm, :])
        a_ref[0] = a0[tm - SUBLANES :, :]
    conv = wdw_ref[0:1, :] * a_m2 + wdw_ref[1:2, :] * a_m1 + wdw_ref[2:3, :] * a0 + bdw_ref[...]
    y = (_gelu(conv) * gate).astype(BF16)
    acc[...] += jnp.dot(y, wd_ref[...], preferred_element_type=F32)

    @pl.when(j == pl.num_programs(1) - 1)
    def _():
        out_ref[...] = h_ref[...] + acc[...]


def _conv_ffn(h, g, w_up_a, w_up_b, w_dw, b_dw, w_down, *, tm, t_len=None, state=None):
    m, d = h.shape
    dff = w_up_a.shape[1]
    tf = 512
    nf = dff // tf
    sample = state is not None
    assert w_dw.shape[0] == 3
    col = lambda i, j: (0, j)
    common = [
        pl.BlockSpec((1, d), lambda i, j: (0, 0)),
        pl.BlockSpec((d, tf), col),
        pl.BlockSpec((d, tf), col),
        pl.BlockSpec((3, tf), col),
        pl.BlockSpec((1, tf), col),
        pl.BlockSpec((tf, d), lambda i, j: (j, 0)),
    ]
    args = [g.reshape(1, d), w_up_a, w_up_b, w_dw, b_dw.reshape(1, dff), w_down]
    hspec = pl.BlockSpec((tm, d), lambda i, j: (i, 0))
    if sample:
        assert m == tm
        st = state.reshape(m, 2 * dff)
        in_specs = [hspec] + common + [pl.BlockSpec((tm, tf), col), pl.BlockSpec((tm, tf), lambda i, j: (0, nf + j))]
        args = [h] + args + [st, st]
        a_spec = pl.BlockSpec((tm, tf), col)
        a_shape = jax.ShapeDtypeStruct((m, dff), F32)
        tiles_per_seq = 1
        lead = 0
    else:
        tiles_per_seq = t_len // tm
        hb = tm // FFN_HALO
        in_specs = [hspec, pl.BlockSpec((FFN_HALO, d), lambda i, j: (jnp.maximum(i * hb - 1, 0), 0))] + common
        args = [h, h] + args
        a_spec = pl.BlockSpec((1, SUBLANES, tf), lambda i, j: (i, 0, j))
        a_shape = jax.ShapeDtypeStruct((m // tm, SUBLANES, dff), F32)
        lead = FFN_HALO
    kern = functools.partial(_ffn_kernel, tm=tm, tiles_per_seq=tiles_per_seq, sample=sample)
    return pl.pallas_call(
        kern,
        grid=(m // tm, nf),
        in_specs=in_specs,
        out_specs=[hspec, a_spec],
        out_shape=[jax.ShapeDtypeStruct((m, d), F32), a_shape],
        scratch_shapes=[pltpu.VMEM((tm + lead, d), BF16), pltpu.VMEM((tm, d), F32)],
        compiler_params=_params("parallel", "arbitrary"),
        name="conv_ffn",
    )(*args)


def _ple_kernel(h_ref, g_ref, wg_ref, pe_ref, wp_ref, o_ref):
    h = h_ref[...]
    hn = (_rms(h) * g_ref[...]).astype(BF16)
    gate = _sigmoid(jnp.dot(hn, wg_ref[...], preferred_element_type=F32))
    proj = jnp.dot(pe_ref[...], wp_ref[...], preferred_element_type=F32)
    o_ref[...] = h + gate * proj


def _ple(h, g, w_gate, pe, w_proj, tm):
    m, d = h.shape
    pd = pe.shape[1]
    return pl.pallas_call(
        _ple_kernel,
        grid=(m // tm,),
        in_specs=[
            pl.BlockSpec((tm, d), lambda i: (i, 0)),
            pl.BlockSpec((1, d), lambda i: (0, 0)),
            pl.BlockSpec((d, d), lambda i: (0, 0)),
            pl.BlockSpec((tm, pd), lambda i: (i, 0)),
            pl.BlockSpec((pd, d), lambda i: (0, 0)),
        ],
        out_specs=pl.BlockSpec((tm, d), lambda i: (i, 0)),
        out_shape=jax.ShapeDtypeStruct((m, d), F32),
        compiler_params=_params("parallel"),
        name="ple",
    )(h, g.reshape(1, d), w_gate, pe, w_proj)


def _cmp_sample_kernel(q_ref, kc_ref, vc_ref, ov_ref, score_ref, ocmp_ref, *, pos, nb, n_slots):
    scale = HEAD_DIM ** -0.5
    qn = q_ref[0]
    slot = lax.broadcasted_iota(jnp.int32, (GROUP, n_slots), 1)
    valid = (slot >= 1) & (slot * CMP_STRIDE + CMP_STRIDE <= pos + 1)
    sums = []
    for h in range(N_KV_HEADS):
        qh = qn[h * GROUP : (h + 1) * GROUP, :].astype(BF16)
        s = lax.dot_general(qh, kc_ref[0, h], NT_DIMS, preferred_element_type=F32) * scale
        e, l = _softmax_rows(jnp.where(valid, s, NEG_INF))
        p = jnp.where(valid, e / l, 0.0)
        ocmp_ref[0, h * GROUP : (h + 1) * GROUP, :] = jnp.dot(p.astype(BF16), vc_ref[0, h], preferred_element_type=F32)
        sums.append(jnp.sum(p, axis=0, keepdims=True))
    p_sum = jnp.concatenate(sums + [jnp.zeros((SUBLANES - N_KV_HEADS, n_slots), F32)], axis=0)
    imp = jnp.dot(p_sum, ov_ref[...], precision=lax.Precision.HIGHEST, preferred_element_type=F32)
    jb = lax.broadcasted_iota(jnp.int32, imp.shape, 1)
    cur = pos // SEL_BLOCK
    causal = (jb * SEL_BLOCK <= pos) & (jb < nb)
    forced = (jb == 0) | (jb == cur) | (jb == cur - 1)
    score_ref[0] = jnp.where(jb < nb, jnp.where(causal, jnp.where(forced, SEL_FORCE, imp), -1.0), -2.0)


def _cmp_sample(q3, kc, vc, pos, nb):
    n_b = q3.shape[0]
    n_slots = kc.shape[2]
    nbp = -(-nb // LANES) * LANES
    ratio = SEL_BLOCK // CMP_STRIDE
    mm = np.arange(n_slots)[:, None]
    jj = np.arange(nbp)[None, :]
    overlap = ((mm >= ratio * jj) & (mm <= ratio * jj + ratio) & (jj < nb)).astype(np.float32)
    qspec = pl.BlockSpec((1, N_HEADS, HEAD_DIM), lambda b: (b, 0, 0))
    cspec = pl.BlockSpec((1, N_KV_HEADS, n_slots, HEAD_DIM), lambda b: (b, 0, 0, 0))
    return pl.pallas_call(
        functools.partial(_cmp_sample_kernel, pos=pos, nb=nb, n_slots=n_slots),
        grid=(n_b,),
        in_specs=[qspec, cspec, cspec, pl.BlockSpec((n_slots, nbp), lambda b: (0, 0))],
        out_specs=[pl.BlockSpec((1, SUBLANES, nbp), lambda b: (b, 0, 0)), qspec],
        out_shape=[
            jax.ShapeDtypeStruct((n_b, SUBLANES, nbp), F32),
            jax.ShapeDtypeStruct((n_b, N_HEADS, HEAD_DIM), F32),
        ],
        compiler_params=_params("parallel"),
        name="cmp_sample",
    )(q3, kc, vc, jnp.asarray(overlap))


def _topk_kernel(score_ref, idx_ref, *, n_sel):
    sc = score_ref[...]
    lane = lax.broadcasted_iota(jnp.int32, sc.shape, 1)
    out_lane = lax.broadcasted_iota(jnp.int32, idx_ref.shape, 1)
    out = jnp.zeros(idx_ref.shape, jnp.int32)
    for it in range(n_sel):
        mx = jnp.max(sc, axis=-1, keepdims=True)
        idx = jnp.min(jnp.where(sc == mx, lane, sc.shape[1]), axis=-1, keepdims=True)
        out = jnp.where(out_lane == it, idx, out)
        sc = jnp.where(lane == idx, -3.0, sc)
    idx_ref[...] = out


def _topk(score, n_sel):
    rows = score.shape[0]
    return pl.pallas_call(
        functools.partial(_topk_kernel, n_sel=n_sel),
        out_shape=jax.ShapeDtypeStruct((rows, LANES), jnp.int32),
        compiler_params=pltpu.CompilerParams(vmem_limit_bytes=VMEM_LIMIT),
        name="topk_sample",
    )(score)


def _sel_sample_kernel(idx_ref, pt_ref, *refs, n_sel, n_past_blocks):
    del pt_ref
    blk_refs = refs[:n_sel]
    qr_ref, new_ref, o_ref = refs[n_sel:]
    b = pl.program_id(0)
    h = pl.program_id(1)
    scale = HEAD_DIM ** -0.5
    blk_rows = SEL_BLOCK * HEADS_PER_ROW
    qr = qr_ref[0, 0].astype(BF16)
    rows = jnp.concatenate([r[...] for r in blk_refs], axis=0).astype(BF16)
    s = lax.dot_general(qr, rows, NT_DIMS, preferred_element_type=F32) * scale
    lane = lax.broadcasted_iota(jnp.int32, s.shape, 1)
    slot = lane // blk_rows
    bias = jnp.where((lane & (HEADS_PER_ROW - 1)) == h, 0.0, NEG_INF)
    n_new = jnp.int32(0)
    for j in range(n_sel):
        is_new = idx_ref[(b * N_KV_HEADS + h) * n_sel + j] >= n_past_blocks
        bias = jnp.where(slot == j, jnp.where(is_new, NEG_INF, bias), bias)
        n_new = n_new + jnp.where(is_new, 1, 0)
    s = s + bias
    k_new = new_ref[0, pl.ds(h, 1), :].astype(BF16).astype(F32)
    v_new = new_ref[0, pl.ds(N_KV_HEADS + h, 1), :].astype(BF16).astype(F32)
    s_new = jnp.sum(qr.astype(F32) * k_new, axis=-1, keepdims=True) * scale
    s_new = jnp.where(n_new > 0, s_new, NEG_INF)
    m = jnp.maximum(jnp.max(s, axis=-1, keepdims=True), s_new)
    e = jnp.exp(s - m)
    e_new = jnp.exp(s_new - m)
    l = jnp.sum(e, axis=-1, keepdims=True) + e_new
    e_v = pltpu.roll(e, N_KV_HEADS, 1)
    o = jnp.dot(e_v.astype(BF16), rows, preferred_element_type=F32) + e_new.astype(BF16).astype(F32) * v_new
    o_ref[0, 0] = o / l


def _sel_sample(idx_flat, pt_flat, cache_rows, qr4, new_rows, n_sel, n_past_blocks, n_pp):
    n_b = qr4.shape[0]
    per_page = PAGE_SIZE // SEL_BLOCK
    blk_rows = SEL_BLOCK * HEADS_PER_ROW

    def bspec(j):
        def imap(b, h, idx, pt):
            blk_id = jnp.minimum(idx[(b * N_KV_HEADS + h) * n_sel + j], n_past_blocks - 1)
            return (pt[b * n_pp + blk_id // per_page] * per_page + blk_id % per_page, 0)
        return pl.BlockSpec((blk_rows, HEAD_DIM), imap)

    grid_spec = pltpu.PrefetchScalarGridSpec(
        num_scalar_prefetch=2,
        grid=(n_b, N_KV_HEADS),
        in_specs=[bspec(j) for j in range(n_sel)]
        + [
            pl.BlockSpec((1, 1, GROUP, HEAD_DIM), lambda b, h, idx, pt: (b, h, 0, 0)),
            pl.BlockSpec((1, HEADS_PER_ROW, HEAD_DIM), lambda b, h, idx, pt: (b, 0, 0)),
        ],
        out_specs=pl.BlockSpec((1, 1, GROUP, HEAD_DIM), lambda b, h, idx, pt: (b, h, 0, 0)),
    )
    return pl.pallas_call(
        functools.partial(_sel_sample_kernel, n_sel=n_sel, n_past_blocks=n_past_blocks),
        grid_spec=grid_spec,
        out_shape=jax.ShapeDtypeStruct((n_b, N_KV_HEADS, GROUP, HEAD_DIM), F32),
        compiler_params=_params("parallel", "arbitrary"),
        name="sel_sample",
    )(idx_flat, pt_flat, *([cache_rows] * n_sel), qr4, new_rows)


def _head_lane_mask(shape, kv_head_of_row):
    lane = lax.broadcasted_iota(jnp.int32, shape, 1)
    return (lane & (HEADS_PER_ROW - 1)) == kv_head_of_row


def _win_sample_kernel(win_ref, new_ref, qr_ref, ocmp_ref, osel_ref, g_ref, nw_ref, o_ref, *, wb):
    scale = HEAD_DIM ** -0.5
    keep = (wb - 1) * HEADS_PER_ROW
    nw_ref[0, 0:keep, :] = win_ref[0, HEADS_PER_ROW:, :]
    nw_ref[0, keep:, :] = new_ref[0]
    rows = nw_ref[0].astype(BF16)
    qr = qr_ref[0].astype(BF16)
    s = lax.dot_general(qr, rows, NT_DIMS, preferred_element_type=F32) * scale
    kv_head = lax.broadcasted_iota(jnp.int32, s.shape, 0) // GROUP
    s = jnp.where(_head_lane_mask(s.shape, kv_head), s, NEG_INF)
    e, l = _softmax_rows(s)
    e_v = pltpu.roll(e, N_KV_HEADS, 1)
    o_win = jnp.dot(e_v.astype(BF16), rows, preferred_element_type=F32) / l
    g = g_ref[0]
    o = g[:, 0:1] * ocmp_ref[0] + g[:, 1:2] * osel_ref[0] + g[:, 2:3] * o_win
    o_ref[0] = o.astype(o_ref.dtype)


def _win_sample(win_rows, new_rows, qr3, o_cmp, o_sel, g3):
    n_b, wr, _ = win_rows.shape
    wb = wr // HEADS_PER_ROW
    hspec = pl.BlockSpec((1, N_HEADS, HEAD_DIM), lambda b: (b, 0, 0))
    wspec = pl.BlockSpec((1, wr, HEAD_DIM), lambda b: (b, 0, 0))
    return pl.pallas_call(
        functools.partial(_win_sample_kernel, wb=wb),
        grid=(n_b,),
        in_specs=[
            wspec,
            pl.BlockSpec((1, HEADS_PER_ROW, HEAD_DIM), lambda b: (b, 0, 0)),
            hspec,
            hspec,
            hspec,
            pl.BlockSpec((1, N_HEADS, 3), lambda b: (b, 0, 0)),
        ],
        out_specs=[wspec, hspec],
        out_shape=[jax.ShapeDtypeStruct((n_b, wr, HEAD_DIM), F32), jax.ShapeDtypeStruct((n_b, N_HEADS, HEAD_DIM), BF16)],
        compiler_params=_params("parallel"),
        name="win_sample",
    )(win_rows, new_rows, qr3, o_cmp, o_sel, g3)


def _rope_tables(pos):
    half = HEAD_DIM // 2
    inv = ROPE_THETA ** (-jnp.arange(half, dtype=F32) / half)
    ang = pos.astype(F32)[:, None] * inv
    cos = jnp.cos(ang)
    sin = jnp.sin(ang)
    return jnp.concatenate([cos, cos], axis=-1), jnp.concatenate([-sin, sin], axis=-1)


def _prep_weights(lw):
    w_in = lw["w_in"]
    c = 0
    parts = {}
    conv_ch = lw["w_conv_dw"].shape[1]
    d_model = w_in.shape[0]
    for name, width in (("q", Q_W), ("cmp", ROW_W), ("sel", ROW_W), ("win", ROW_W), ("gnsa", 3 * N_HEADS),
                        ("glu", 2 * conv_ch), ("gm", 2 * d_model)):
        parts[name] = w_in[:, c : c + width]
        c += width
    wg = parts["gnsa"].reshape(d_model, 3, N_KV_HEADS, GROUP).transpose(0, 2, 1, 3).reshape(d_model, N_KV_HEADS, 3 * GROUP)
    wg = jnp.pad(wg, ((0, 0), (0, 0), (0, LANES - 3 * GROUP))).reshape(d_model, N_KV_HEADS * LANES)
    w1 = lw["w_cmp1"]
    hid = w1.shape[-1]
    r = CMP_BLOCK // CMP_STRIDE
    w1all = w1.reshape(2, r, CMP_STRIDE, HEAD_DIM, hid).transpose(2, 3, 0, 1, 4).reshape(CMP_STRIDE * HEAD_DIM, 2 * r * hid)
    dff = lw["w_up"].shape[1] // 2
    return {
        "w_q": parts["q"].astype(BF16),
        "w_main": jnp.concatenate([parts["glu"], parts["gm"]], axis=1).astype(BF16),
        "w_cmp_gate": jnp.concatenate([parts["cmp"], wg], axis=1).astype(BF16),
        "w_sel": parts["sel"].astype(BF16),
        "w_win": parts["win"].astype(BF16),
        "w1all": w1all.astype(BF16),
        "w2all": jnp.concatenate([lw["w_cmp2"][0], lw["w_cmp2"][1]], axis=1).astype(BF16),
        "w_attn_out": lw["w_attn_out"].astype(BF16),
        "w_conv_out": lw["w_conv_out"].astype(BF16),
        "w_out": lw["w_out"].astype(BF16),
        "w_up_a": lw["w_up"][:, :dff].astype(BF16),
        "w_up_b": lw["w_up"][:, dff:].astype(BF16),
        "w_down": lw["w_down"].astype(BF16),
        "w_ple_gate": lw["w_ple_gate"].astype(BF16),
        "w_ple_proj": lw["w_ple_proj"].astype(BF16),
    }


def _dense_tail(x, o, zmain, y_conv, pe, lw, pw, *, tm_mix, tm, t_len=None, ffn_state=None):
    mixed = _attn_mix(o, pw["w_attn_out"], zmain, y_conv, tm_mix)
    h1 = _out_proj(mixed, pw["w_out"], x, tm)
    h2, a_tail = _conv_ffn(h1, lw["g_ffn"], pw["w_up_a"], pw["w_up_b"], lw["w_ffn_dw"], lw["b_ffn_dw"], pw["w_down"],
                           tm=tm, t_len=t_len, state=ffn_state)
    h3 = _ple(h2, lw["g_ple"], pw["w_ple_gate"], pe.astype(BF16), pw["w_ple_proj"], tm)
    return h3, a_tail


def _layer_prompt(x3, pe3, lw, pw, bias_cmp):
    n_b, t_len, d = x3.shape
    m = n_b * t_len
    x = x3.reshape(m, d)
    cos, sin = _rope_tables(jnp.arange(t_len, dtype=jnp.int32))
    xn = _rmsnorm_bf16(x, lw["g_mix"], 512)
    zmain = _matmul(xn, pw["w_main"], 1024, 1024)
    rows_cmp, gates = _cmp_gate_proj(xn, pw["w_cmp_gate"], 512)
    rows_sel, sel_bf = _kv_proj(xn, pw["w_sel"], lw["g_k"][1], cos, sin, 512)
    rows_win, win_bf = _kv_proj(xn, pw["w_win"], lw["g_k"][2], cos, sin, 512)
    n_pp = t_len // PAGE_SIZE
    ident = jnp.arange(n_b * n_pp, dtype=jnp.int32).reshape(n_b, n_pp)
    kc, vc = _compress(rows_cmp.reshape(n_b * n_pp, PAGE_SIZE, HEADS_PER_ROW, HEAD_DIM), ident, pw["w1all"], bias_cmp,
                       pw["w2all"], lw["b_cmp2"], lw["g_k"][0])
    q_bf, qr_bf = _q_proj(xn, pw["w_q"], lw["g_q"], cos, sin, 512, BF16)
    o = _nsa_prompt(q_bf, qr_bf, gates, kc, vc, sel_bf, win_bf, n_b, t_len)
    y_conv, u_tail = _conv_prompt(zmain, lw["w_conv_dw"], lw["b_conv_dw"], lw["g_conv_ln"], lw["b_conv_ln"],
                                  pw["w_conv_out"], n_b, t_len)
    h3, a_tail = _dense_tail(x, o, zmain, y_conv, pe3.reshape(m, -1), lw, pw, tm_mix=256, tm=512, t_len=t_len)
    kvshape = (n_b, t_len, 2, N_KV_HEADS, HEAD_DIM)
    w_keep = min(WINDOW, t_len)
    conv_w = lw["w_conv_dw"].shape[0]
    state = (
        rows_cmp.reshape(kvshape),
        rows_sel.reshape(kvshape),
        rows_win.reshape(n_b, t_len * HEADS_PER_ROW, HEAD_DIM)[:, (t_len - w_keep) * HEADS_PER_ROW :].reshape(
            n_b, w_keep, 2, N_KV_HEADS, HEAD_DIM),
        u_tail[:, CONV_HALO - (conv_w - 1) :],
        a_tail.reshape(n_b, -1, SUBLANES, a_tail.shape[-1])[:, -1, SUBLANES - (lw["w_ffn_dw"].shape[0] - 1) :],
    )
    return h3.reshape(n_b, t_len, d), state


def _layer_sample(x3, pe3, lw, pw, bias_cmp, cache_cmp, cache_sel, cache_win, conv_state, ffn_state, page_table):
    n_b, t_new, d = x3.shape
    assert t_new == 1
    n_pp = page_table.shape[1]
    past_len = n_pp * PAGE_SIZE
    pos = past_len
    wb = cache_win.shape[1]
    assert wb == WINDOW and past_len % SEL_BLOCK == 0
    x = x3.reshape(n_b, d)
    cos, sin = _rope_tables(jnp.full((n_b,), pos, dtype=jnp.int32))
    xn = _rmsnorm_bf16(x, lw["g_mix"], n_b)
    zmain = _matmul(xn, pw["w_main"], n_b, 1024)
    rows_cmp, gates = _cmp_gate_proj(xn, pw["w_cmp_gate"], n_b)
    rows_sel, _ = _kv_proj(xn, pw["w_sel"], lw["g_k"][1], cos, sin, n_b)
    rows_win, _ = _kv_proj(xn, pw["w_win"], lw["g_k"][2], cos, sin, n_b)
    n_pool = cache_cmp.shape[0]
    kc, vc = _compress(cache_cmp.reshape(n_pool, PAGE_SIZE, HEADS_PER_ROW, HEAD_DIM), page_table, pw["w1all"], bias_cmp,
                       pw["w2all"], lw["b_cmp2"], lw["g_k"][0])
    n_past_blocks = past_len // SEL_BLOCK
    nb = n_past_blocks + 1
    n_sel = min(N_SEL, nb)
    q_s, qr_s = _q_proj(xn, pw["w_q"], lw["g_q"], cos, sin, n_b, F32)
    q3 = q_s.reshape(n_b, N_HEADS, HEAD_DIM)
    qr3 = qr_s.reshape(n_b, N_HEADS, HEAD_DIM)
    score, o_cmp = _cmp_sample(q3, kc, vc, pos, nb)
    idx = _topk(score.reshape(n_b * SUBLANES, -1), n_sel)
    idx = idx.reshape(n_b, SUBLANES, LANES)[:, :N_KV_HEADS, :n_sel]
    o_sel = _sel_sample(idx.reshape(-1), page_table.reshape(-1),
                        cache_sel.reshape(n_pool * PAGE_SIZE * HEADS_PER_ROW, HEAD_DIM),
                        qr3.reshape(n_b, N_KV_HEADS, GROUP, HEAD_DIM), rows_sel.reshape(n_b, HEADS_PER_ROW, HEAD_DIM),
                        n_sel, n_past_blocks, n_pp)
    g3 = gates.reshape(n_b, N_KV_HEADS, LANES)[:, :, : 3 * GROUP].reshape(n_b, N_KV_HEADS, 3, GROUP)
    g3 = g3.transpose(0, 1, 3, 2).reshape(n_b, N_HEADS, 3)
    new_win, o = _win_sample(cache_win.reshape(n_b, wb * HEADS_PER_ROW, HEAD_DIM),
                             rows_win.reshape(n_b, HEADS_PER_ROW, HEAD_DIM), qr3, o_cmp,
                             o_sel.reshape(n_b, N_HEADS, HEAD_DIM), g3)
    y_conv, u = _conv_sample(zmain, conv_state, lw["w_conv_dw"], lw["b_conv_dw"], lw["g_conv_ln"], lw["b_conv_ln"],
                             pw["w_conv_out"])
    h3, a_new = _dense_tail(x, o.reshape(n_b, Q_W), zmain, y_conv, pe3.reshape(n_b, -1), lw, pw,
                            tm_mix=n_b, tm=n_b, ffn_state=ffn_state)
    kvshape = (n_b, 1, 2, N_KV_HEADS, HEAD_DIM)
    state = (
        rows_cmp.reshape(kvshape),
        rows_sel.reshape(kvshape),
        new_win.reshape(n_b, wb, 2, N_KV_HEADS, HEAD_DIM),
        jnp.concatenate([conv_state[:, 1:], u[:, None, :]], axis=1),
        jnp.concatenate([ffn_state[:, 1:], a_new[:, None, :]], axis=1),
    )
    return h3.reshape(n_b, 1, d), state


_LAYER_WEIGHTS = ("g_mix", "w_in", "g_q", "g_k", "w_cmp1", "b_cmp1", "w_cmp2", "b_cmp2", "pe_cmp", "w_attn_out",
                  "w_conv_dw", "b_conv_dw", "g_conv_ln", "b_conv_ln", "w_conv_out", "w_out", "g_ffn", "w_up",
                  "w_ffn_dw", "b_ffn_dw", "w_down", "g_ple", "w_ple_gate", "w_ple_proj")


def kernel(x_prompt, x_sample, p_prompt, p_sample, cache_cmp_kv, cache_sel_kv, cache_win_kv, state_conv, state_ffn_conv, page_table, g_mix, w_in, g_q, g_k, w_cmp1, b_cmp1, w_cmp2, b_cmp2, pe_cmp, w_attn_out, w_conv_dw, b_conv_dw, g_conv_ln, b_conv_ln, w_conv_out, w_out, g_ffn, w_up, w_ffn_dw, b_ffn_dw, w_down, g_ple, w_ple_gate, w_ple_proj):
    stacked = dict(zip(_LAYER_WEIGHTS, (g_mix, w_in, g_q, g_k, w_cmp1, b_cmp1, w_cmp2, b_cmp2, pe_cmp, w_attn_out,
                                        w_conv_dw, b_conv_dw, g_conv_ln, b_conv_ln, w_conv_out, w_out, g_ffn, w_up,
                                        w_ffn_dw, b_ffn_dw, w_down, g_ple, w_ple_gate, w_ple_proj)))
    depth = w_in.shape[0]
    hp, hs = x_prompt, x_sample
    states_p, states_s = [], []
    for i in range(depth):
        lw = {k: v[i] for k, v in stacked.items()}
        pw = _prep_weights(lw)
        bias_cmp = _pe_bias(lw["pe_cmp"], lw["w_cmp1"], lw["b_cmp1"])
        hp, st_p = _layer_prompt(hp, p_prompt[i], lw, pw, bias_cmp)
        hs, st_s = _layer_sample(hs, p_sample[i], lw, pw, bias_cmp, cache_cmp_kv[i], cache_sel_kv[i], cache_win_kv[i],
                                 state_conv[i], state_ffn_conv[i], page_table)
        states_p.append(st_p)
        states_s.append(st_s)
    outs = [hp, hs]
    for k in range(5):
        outs.append(jnp.stack([s[k] for s in states_p]))
        outs.append(jnp.stack([s[k] for s in states_s]))
    return tuple(outs)
```

```python
import functools

import numpy as np
import jax
import jax.numpy as jnp
from jax import lax
from jax.experimental import pallas as pl
from jax.experimental.pallas import tpu as pltpu

N_HEADS = 16
HEAD_DIM = 128
N_KV_HEADS = 4
GROUP = N_HEADS // N_KV_HEADS
Q_W = N_HEADS * HEAD_DIM
KV_W = N_KV_HEADS * HEAD_DIM
ROW_W = 2 * KV_W
HEADS_PER_ROW = 2 * N_KV_HEADS
CMP_BLOCK = 32
CMP_STRIDE = 16
SEL_BLOCK = 64
SEL_SHIFT = 6
N_SEL = 16
WINDOW = 512
PAGE_SIZE = 128
ROPE_THETA = 10000.0
EPS = 1e-6
NEG_INF = -1e30
SEL_FORCE = 1e6
MASK_BIG = 2.0 ** 100
CONV_HALO = 32
FFN_HALO = 16
SCORE_CHUNK = 512
ROW_TILE = 128
MAX_PAGES_PER_STEP = 32

LANES = 128
SUBLANES = 8
VMEM_LIMIT = 56 * 1024 * 1024

F32 = jnp.float32
BF16 = jnp.bfloat16
NT_DIMS = (((1,), (1,)), ((), ()))


def _params(*sem):
    return pltpu.CompilerParams(dimension_semantics=sem, vmem_limit_bytes=VMEM_LIMIT)


def _rms(x):
    return x * lax.rsqrt(jnp.mean(x * x, axis=-1, keepdims=True) + EPS)


def _rope(x, cos, sin_signed):
    return x * cos + pltpu.roll(x, HEAD_DIM // 2, 1) * sin_signed


def _sigmoid(x):
    return jax.nn.sigmoid(x)


def _softmax_rows(s):
    m = jnp.max(s, axis=-1, keepdims=True)
    e = jnp.exp(s - m)
    return e, jnp.sum(e, axis=-1, keepdims=True)


def _softmax_rows_scaled(s, scale):
    m = jnp.max(s, axis=-1, keepdims=True)
    e = jnp.exp2((s - m) * np.float32(scale * np.log2(np.e)))
    return e, jnp.sum(e, axis=-1, keepdims=True)


def _rmsnorm_kernel(x_ref, g_ref, o_ref):
    o_ref[...] = (_rms(x_ref[...]) * g_ref[...]).astype(o_ref.dtype)


def _rmsnorm_bf16(x, g, tm):
    m, d = x.shape
    return pl.pallas_call(
        _rmsnorm_kernel,
        grid=(m // tm,),
        in_specs=[pl.BlockSpec((tm, d), lambda i: (i, 0)), pl.BlockSpec((1, d), lambda i: (0, 0))],
        out_specs=pl.BlockSpec((tm, d), lambda i: (i, 0)),
        out_shape=jax.ShapeDtypeStruct((m, d), BF16),
        compiler_params=_params("parallel"),
        name="rmsnorm_in",
    )(x, g.reshape(1, d))


def _matmul_kernel(x_ref, w_ref, o_ref):
    o_ref[...] = jnp.dot(x_ref[...], w_ref[...], preferred_element_type=F32)


def _matmul(x, w, tm, tn):
    m, k = x.shape
    n = w.shape[1]
    return pl.pallas_call(
        _matmul_kernel,
        grid=(m // tm, n // tn),
        in_specs=[pl.BlockSpec((tm, k), lambda i, j: (i, 0)), pl.BlockSpec((k, tn), lambda i, j: (0, j))],
        out_specs=pl.BlockSpec((tm, tn), lambda i, j: (i, j)),
        out_shape=jax.ShapeDtypeStruct((m, n), F32),
        compiler_params=_params("parallel", "arbitrary"),
        name="in_proj_main",
    )(x, w)


def _q_kernel(x_ref, w_ref, gq_ref, cos_ref, sin_ref, q_ref, qr_ref):
    cos = cos_ref[...]
    sin = sin_ref[...]
    gq = gq_ref[...]
    x = x_ref[...]
    cw = 2 * LANES
    for c0 in range(0, w_ref.shape[1], cw):
        acc = jnp.dot(x, w_ref[:, c0 : c0 + cw], preferred_element_type=F32)
        for h0 in range(0, cw, HEAD_DIM):
            sl = slice(c0 + h0, c0 + h0 + HEAD_DIM)
            qn = _rms(acc[:, h0 : h0 + HEAD_DIM]) * gq
            q_ref[:, sl] = qn.astype(q_ref.dtype)
            qr_ref[:, sl] = _rope(qn, cos, sin).astype(qr_ref.dtype)


def _q_proj(xn, w, gq, cos, sin, tm, out_dtype):
    m, k = xn.shape
    n = w.shape[1]
    nt = cos.shape[0] // tm
    ospec = pl.BlockSpec((tm, n), lambda i: (i, 0))
    tspec = pl.BlockSpec((tm, HEAD_DIM), lambda i: (i % nt, 0))
    return pl.pallas_call(
        _q_kernel,
        grid=(m // tm,),
        in_specs=[pl.BlockSpec((tm, k), lambda i: (i, 0)), pl.BlockSpec((k, n), lambda i: (0, 0)),
                  pl.BlockSpec((1, HEAD_DIM), lambda i: (0, 0)), tspec, tspec],
        out_specs=[ospec, ospec],
        out_shape=[jax.ShapeDtypeStruct((m, n), out_dtype), jax.ShapeDtypeStruct((m, n), out_dtype)],
        compiler_params=_params("parallel"),
        name="in_proj_q",
    )(xn, w, gq.reshape(1, HEAD_DIM), cos, sin)


def _store_rows(rows_ref, acc):
    tm = acc.shape[0]
    for j in range(HEADS_PER_ROW):
        rows_ref[pl.ds(j, tm, stride=HEADS_PER_ROW), :] = acc[:, j * HEAD_DIM : (j + 1) * HEAD_DIM]


def _cmp_gate_kernel(x_ref, w_ref, rows_ref, gate_ref):
    acc = jnp.dot(x_ref[...], w_ref[...], preferred_element_type=F32)
    _store_rows(rows_ref, acc[:, :ROW_W])
    gate_ref[...] = _sigmoid(acc[:, ROW_W:])


def _cmp_gate_proj(xn, w, tm):
    m, k = xn.shape
    n = w.shape[1]
    ng = n - ROW_W
    return pl.pallas_call(
        _cmp_gate_kernel,
        grid=(m // tm,),
        in_specs=[pl.BlockSpec((tm, k), lambda i: (i, 0)), pl.BlockSpec((k, n), lambda i: (0, 0))],
        out_specs=[pl.BlockSpec((tm * HEADS_PER_ROW, HEAD_DIM), lambda i: (i, 0)),
                   pl.BlockSpec((tm, ng), lambda i: (i, 0))],
        out_shape=[jax.ShapeDtypeStruct((m * HEADS_PER_ROW, HEAD_DIM), F32), jax.ShapeDtypeStruct((m, ng), F32)],
        compiler_params=_params("parallel"),
        name="in_proj_cmp_gates",
    )(xn, w)


def _kv_kernel(x_ref, w_ref, gk_ref, cos_ref, sin_ref, rows_ref, rows_bf_ref):
    acc = jnp.dot(x_ref[...], w_ref[...], preferred_element_type=F32)
    cos = cos_ref[...]
    sin = sin_ref[...]
    gk = gk_ref[...]
    tm = acc.shape[0]
    for h in range(N_KV_HEADS):
        sl = slice(h * HEAD_DIM, (h + 1) * HEAD_DIM)
        k = _rope(_rms(acc[:, sl]) * gk, cos, sin)
        rows_ref[pl.ds(h, tm, stride=HEADS_PER_ROW), :] = k
        rows_bf_ref[:, sl] = k.astype(BF16)
        sv = slice(KV_W + h * HEAD_DIM, KV_W + (h + 1) * HEAD_DIM)
        rows_ref[pl.ds(N_KV_HEADS + h, tm, stride=HEADS_PER_ROW), :] = acc[:, sv]
    rows_bf_ref[:, KV_W:] = acc[:, KV_W:].astype(BF16)


def _kv_proj(xn, w, gk, cos, sin, tm):
    m, k = xn.shape
    nt = cos.shape[0] // tm
    return pl.pallas_call(
        _kv_kernel,
        grid=(m // tm,),
        in_specs=[
            pl.BlockSpec((tm, k), lambda i: (i, 0)),
            pl.BlockSpec((k, ROW_W), lambda i: (0, 0)),
            pl.BlockSpec((1, HEAD_DIM), lambda i: (0, 0)),
            pl.BlockSpec((tm, HEAD_DIM), lambda i: (i % nt, 0)),
            pl.BlockSpec((tm, HEAD_DIM), lambda i: (i % nt, 0)),
        ],
        out_specs=[pl.BlockSpec((tm * HEADS_PER_ROW, HEAD_DIM), lambda i: (i, 0)),
                   pl.BlockSpec((tm, ROW_W), lambda i: (i, 0))],
        out_shape=[jax.ShapeDtypeStruct((m * HEADS_PER_ROW, HEAD_DIM), F32), jax.ShapeDtypeStruct((m, ROW_W), BF16)],
        compiler_params=_params("parallel"),
        name="in_proj_kv",
    )(xn, w, gk.reshape(1, HEAD_DIM), cos, sin)


def _pe_bias_kernel(pe_ref, w1_ref, b1_ref, o_ref):
    for kv in range(2):
        acc = jnp.zeros((SUBLANES, HEAD_DIM), F32)
        for r in range(CMP_BLOCK):
            row = jnp.broadcast_to(pe_ref[kv, r : r + 1, :], (SUBLANES, HEAD_DIM)).astype(BF16)
            acc = acc + jnp.dot(
                row, w1_ref[kv, r * HEAD_DIM : (r + 1) * HEAD_DIM, :].astype(BF16), preferred_element_type=F32
            )
        o_ref[kv] = acc[0:1, :] + b1_ref[kv]


def _pe_bias(pe, w1, b1):
    hid = w1.shape[-1]
    return pl.pallas_call(
        _pe_bias_kernel,
        out_shape=jax.ShapeDtypeStruct((2, 1, hid), F32),
        compiler_params=pltpu.CompilerParams(vmem_limit_bytes=VMEM_LIMIT),
        name="cmp_pe_bias",
    )(pe, w1, b1.reshape(2, 1, hid))


def _compress_kernel(pt_ref, *refs, n_pg):
    del pt_ref
    page_refs = refs[:n_pg]
    w1_ref, bias_ref, w2_ref, b2_ref, gk_ref, kc_ref, vc_ref, carry_ref, out_scr = refs[n_pg:]
    sub_per_page = PAGE_SIZE // CMP_STRIDE
    nsub = n_pg * sub_per_page
    rows = nsub * HEADS_PER_ROW
    page_rows = sub_per_page * HEADS_PER_ROW

    @pl.when(pl.program_id(1) == 0)
    def _():
        carry_ref[...] = jnp.zeros_like(carry_ref)

    half = rows // 2
    hid2 = w1_ref.shape[1] // 2
    hid_w = hid2 // 2
    top = lax.broadcasted_iota(jnp.int32, (nsub // 2, HEADS_PER_ROW, 2 * HEAD_DIM), 1) < N_KV_HEADS
    acc = [None, None]
    for pp in range(0, CMP_STRIDE, 2):
        pieces = []
        for pg in range(n_pg):
            a = page_refs[pg][0, pl.ds(pp, sub_per_page, stride=CMP_STRIDE), :, :].reshape(page_rows, HEAD_DIM)
            b = page_refs[pg][0, pl.ds(pp + 1, sub_per_page, stride=CMP_STRIDE), :, :].reshape(page_rows, HEAD_DIM)
            pieces.append(jnp.concatenate([a, b], axis=1))
        x3 = jnp.concatenate(pieces, axis=0).reshape(nsub // 2, 2 * HEADS_PER_ROW, 2 * HEAD_DIM)
        even = x3[:, :HEADS_PER_ROW]
        odd = x3[:, HEADS_PER_ROW:]
        xs = (jnp.where(top, even, pltpu.roll(odd, N_KV_HEADS, 1)),
              jnp.where(top, pltpu.roll(even, N_KV_HEADS, 1), odd))
        w1 = w1_ref[pp * HEAD_DIM : (pp + 2) * HEAD_DIM, :]
        for kv in range(2):
            d = jnp.dot(xs[kv].reshape(half, 2 * HEAD_DIM).astype(BF16), w1[:, kv * hid2 : (kv + 1) * hid2],
                        preferred_element_type=F32)
            acc[kv] = d if acc[kv] is None else acc[kv] + d
    first_rows = lax.broadcasted_iota(jnp.int32, (half, hid_w), 0) < N_KV_HEADS
    pad = jnp.zeros((half - SUBLANES, hid_w), F32)
    for kv in range(2):
        p_first = acc[kv][:, :hid_w]
        p_second = acc[kv][:, hid_w:]
        rolled = pltpu.roll(p_first, N_KV_HEADS, 0)
        shifted = jnp.where(first_rows, jnp.concatenate([carry_ref[kv], pad], axis=0), rolled)
        carry_ref[kv] = rolled[0:SUBLANES, :]
        hid = shifted + p_second + bias_ref[kv]
        hid = hid * _sigmoid(hid)
        out = jnp.dot(hid.astype(BF16), w2_ref[:, kv * HEAD_DIM : (kv + 1) * HEAD_DIM],
                      preferred_element_type=F32) + b2_ref[kv]
        if kv == 0:
            out = _rms(out) * gk_ref[...]
        out_scr[kv] = out
        dst = kc_ref if kv == 0 else vc_ref
        for h in range(N_KV_HEADS):
            dst[0, h, :, :] = out_scr[kv, pl.ds(h, nsub, stride=N_KV_HEADS), :].astype(BF16)


def _compress(pages, page_table, w1all, bias, w2all, b2, gk):
    n_seq, n_pp = page_table.shape
    n_pg = max(d for d in range(1, MAX_PAGES_PER_STEP + 1) if n_pp % d == 0)
    steps = n_pp // n_pg
    nsub = n_pg * (PAGE_SIZE // CMP_STRIDE)
    n_slots = steps * nsub
    hid = bias.shape[-1]

    def page_spec(k):
        return pl.BlockSpec((1, PAGE_SIZE, HEADS_PER_ROW, HEAD_DIM),
                            lambda b, s, pt: (pt[b * n_pp + s * n_pg + k], 0, 0, 0))

    const2 = lambda b, s, pt: (0, 0)
    const3 = lambda b, s, pt: (0, 0, 0)
    out_spec = pl.BlockSpec((1, N_KV_HEADS, nsub, HEAD_DIM), lambda b, s, pt: (b, 0, s, 0))
    grid_spec = pltpu.PrefetchScalarGridSpec(
        num_scalar_prefetch=1,
        grid=(n_seq, steps),
        in_specs=[page_spec(k) for k in range(n_pg)]
        + [
            pl.BlockSpec(w1all.shape, const2),
            pl.BlockSpec((2, 1, hid), const3),
            pl.BlockSpec(w2all.shape, const2),
            pl.BlockSpec((2, 1, HEAD_DIM), const3),
            pl.BlockSpec((1, HEAD_DIM), const2),
        ],
        out_specs=[out_spec, out_spec],
        scratch_shapes=[pltpu.VMEM((2, SUBLANES, hid), F32), pltpu.VMEM((2, nsub * N_KV_HEADS, HEAD_DIM), F32)],
    )
    shape = jax.ShapeDtypeStruct((n_seq, N_KV_HEADS, n_slots, HEAD_DIM), BF16)
    return pl.pallas_call(
        functools.partial(_compress_kernel, n_pg=n_pg),
        grid_spec=grid_spec,
        out_shape=[shape, shape],
        compiler_params=_params("parallel", "arbitrary"),
        name="compress",
    )(page_table.reshape(-1), *([pages] * n_pg), w1all, bias, w2all, b2.reshape(2, 1, HEAD_DIM), gk.reshape(1, HEAD_DIM))


def _nsa_prompt_kernel(
    q_ref, qr_ref, gate_ref, kc_ref, vc_ref, ks_ref, vs_ref, kw_ref, vw_ref, expand_ref,
    o_ref, bias_scr, osel_scr, s_scr, p_scr, m_scr, owin_scr, sw_scr, pw_scr, mw_scr,
    *, tq, t_len, nb, n_sel, n_slots, kt):
    i = pl.program_id(2)
    q0 = i * tq
    rows = GROUP * tq
    scale = HEAD_DIM ** -0.5
    heads = [slice(g * HEAD_DIM, (g + 1) * HEAD_DIM) for g in range(GROUP)]
    q = jnp.concatenate([q_ref[:, sl] for sl in heads], axis=0)
    qr = jnp.concatenate([qr_ref[:, sl] for sl in heads], axis=0)

    def attend(k_ref, v_ref, k0, n_keys, bias_chunk, s_buf, p_buf, m_buf):
        n_chunks = -(-n_keys // SCORE_CHUNK)
        for c in range(n_chunks):
            c0 = c * SCORE_CHUNK
            w = min(SCORE_CHUNK, n_keys - c0)
            s = lax.dot_general(qr, k_ref[0, pl.ds(k0 + c0, w), :], NT_DIMS, preferred_element_type=F32)
            s = (s.reshape(GROUP, tq, w) + bias_chunk(c0, w)[None]).reshape(rows, w)
            s_buf[:, c0 : c0 + w] = s
            pm = s[:, 0:LANES]
            for j0 in range(LANES, w, LANES):
                pm = jnp.maximum(pm, s[:, j0 : j0 + LANES])
            m_buf[:, c * LANES : (c + 1) * LANES] = pm
        coef = np.float32(scale * np.log2(np.e))
        outs = []
        for g in range(GROUP):
            sums = []
            for r0 in range(g * tq, (g + 1) * tq, ROW_TILE):
                r = slice(r0, r0 + ROW_TILE)
                m_acc = m_buf[r, 0:LANES]
                for c in range(1, n_chunks):
                    m_acc = jnp.maximum(m_acc, m_buf[r, c * LANES : (c + 1) * LANES])
                m_b = jnp.broadcast_to(jnp.max(m_acc, axis=-1, keepdims=True), (ROW_TILE, LANES))
                l_acc = jnp.zeros((ROW_TILE, LANES), F32)
                for c0 in range(0, n_keys, LANES):
                    p = jnp.exp2((s_buf[r, c0 : c0 + LANES] - m_b) * coef)
                    l_acc = l_acc + p
                    p_buf[r, c0 : c0 + LANES] = p.astype(BF16)
                sums.append(jnp.sum(l_acc, axis=-1, keepdims=True))
            rg = slice(g * tq, (g + 1) * tq)
            pv = jnp.dot(p_buf[rg, 0:n_keys], v_ref[0, pl.ds(k0, n_keys), :], preferred_element_type=F32)
            outs.append(pv / jnp.concatenate(sums, axis=0))
        return jnp.concatenate(outs, axis=0)

    lw = WINDOW + tq
    kw0 = pl.multiple_of(jnp.maximum(q0 - WINDOW, 0), tq)

    def window_bias(c0, w):
        diff = (q0 + lax.broadcasted_iota(jnp.int32, (tq, w), 0)) - (
            kw0 + c0 + lax.broadcasted_iota(jnp.int32, (tq, w), 1))
        return jnp.where((diff >= 0) & (diff < WINDOW), 0.0, NEG_INF)

    owin_scr[...] = attend(kw_ref, vw_ref, kw0, lw, window_bias, sw_scr, pw_scr, mw_scr)

    s_t = lax.dot_general(kc_ref[0, 0], q, NT_DIMS, preferred_element_type=F32) * scale
    slot = lax.broadcasted_iota(jnp.int32, (n_slots, rows), 0)
    tok = q0 + (lax.broadcasted_iota(jnp.int32, (n_slots, rows), 1) & (tq - 1))
    valid = (slot >= 1) & (slot * CMP_STRIDE + CMP_STRIDE <= tok + 1)
    s_m = jnp.where(valid, s_t, NEG_INF)
    e = jnp.exp(s_m - jnp.max(s_m, axis=0, keepdims=True))
    p_t = jnp.where(valid, e / jnp.sum(e, axis=0, keepdims=True), 0.0)
    o_cmp = jnp.dot(p_t.T.astype(BF16), vc_ref[0, 0], preferred_element_type=F32)

    p_sum = p_t[:, 0:tq]
    for g in range(1, GROUP):
        p_sum = p_sum + p_t[:, g * tq : (g + 1) * tq]
    ratio = SEL_BLOCK // CMP_STRIDE
    jj = lax.broadcasted_iota(jnp.int32, (nb, n_slots), 0)
    mm = lax.broadcasted_iota(jnp.int32, (nb, n_slots), 1)
    overlap = jnp.where((mm >= ratio * jj) & (mm <= ratio * jj + ratio), 1.0, 0.0)
    imp = jnp.dot(overlap, p_sum, precision=lax.Precision.HIGHEST, preferred_element_type=F32)
    jb = lax.broadcasted_iota(jnp.int32, (nb, tq), 0)
    pos = q0 + lax.broadcasted_iota(jnp.int32, (nb, tq), 1)
    cur = lax.shift_right_logical(pos, SEL_SHIFT)
    causal = jb * SEL_BLOCK <= pos
    forced = (jb == 0) | (jb == cur) | (jb == cur - 1)
    score = jnp.where(causal, jnp.where(forced, SEL_FORCE, imp), -1.0)
    cnt = jnp.zeros((nb, tq), F32)
    for i2 in range(nb):
        r = score[i2 : i2 + 1, :]
        cnt = cnt + jnp.where(r > score, 1.0, jnp.where((r == score) & (jb > i2), 1.0, 0.0))
    sel = jnp.where((cnt < n_sel) & causal, 1.0, 0.0)
    if nb < LANES:
        sel = jnp.concatenate([sel, jnp.zeros((LANES - nb, tq), F32)], axis=0)
    bias_scr[...] = jnp.dot(sel.T.astype(BF16), expand_ref[...], preferred_element_type=F32) - MASK_BIG

    def sel_bias(c0, w, first_diag_key):
        b = bias_scr[:, c0 : c0 + w]
        if c0 + w <= first_diag_key:
            return b
        kpos = c0 + lax.broadcasted_iota(jnp.int32, (tq, w), 1)
        qpos = q0 + lax.broadcasted_iota(jnp.int32, (tq, w), 0)
        return jnp.where(kpos <= qpos, b, -MASK_BIG)

    n_tiles = (q0 + tq + kt - 1) // kt
    for nt in range(1, t_len // kt + 1):

        @pl.when(n_tiles == nt)
        def _(nt=nt):
            osel_scr[...] = attend(ks_ref, vs_ref, 0, nt * kt, functools.partial(sel_bias, first_diag_key=(nt - 1) * kt),
                                   s_scr, p_scr, m_scr)

    o_sel = osel_scr[...]
    o_win = owin_scr[...]
    gt = gate_ref[...]
    outs = []
    for g in range(GROUP):
        sl = slice(g * tq, (g + 1) * tq)
        outs.append(
            gt[:, g : g + 1] * o_cmp[sl]
            + gt[:, GROUP + g : GROUP + g + 1] * o_sel[sl]
            + gt[:, 2 * GROUP + g : 2 * GROUP + g + 1] * o_win[sl]
        )
    o_ref[...] = jnp.concatenate(outs, axis=1).astype(o_ref.dtype)


def _nsa_prompt(q_bf, qr_bf, gates, kc, vc, sel_bf, win_bf, n_b, t_len):
    tq = 128
    kt = min(512, t_len)
    nb = t_len // SEL_BLOCK
    n_slots = kc.shape[2]
    n_sel = min(N_SEL, nb)
    assert t_len % kt == 0 and t_len >= WINDOW + tq and nb <= LANES and n_slots == t_len // CMP_STRIDE
    ntq = t_len // tq
    m = n_b * t_len
    rows = GROUP * tq
    lw = WINDOW + tq
    expand =(np.arange(t_len)[None, :] // SEL_BLOCK == np.arange(LANES)[:, None]).astype(np.float32) * MASK_BIG
    sel3 = sel_bf.reshape(n_b, t_len, ROW_W)
    win3 = win_bf.reshape(n_b, t_len, ROW_W)
    kspec = pl.BlockSpec((1, t_len, HEAD_DIM), lambda b, h, i: (b, 0, h))
    vspec = pl.BlockSpec((1, t_len, HEAD_DIM), lambda b, h, i: (b, 0, N_KV_HEADS + h))
    cspec = pl.BlockSpec((1, 1, n_slots, HEAD_DIM), lambda b, h, i: (b, h, 0, 0))
    qspec = pl.BlockSpec((tq, GROUP * HEAD_DIM), lambda b, h, i: (b * ntq + i, h))
    kern = functools.partial(
        _nsa_prompt_kernel, tq=tq, t_len=t_len, nb=nb, n_sel=n_sel, n_slots=n_slots, kt=kt)
    return pl.pallas_call(
        kern,
        grid=(n_b, N_KV_HEADS, ntq),
        in_specs=[
            qspec,
            qspec,
            pl.BlockSpec((tq, LANES), lambda b, h, i: (b * ntq + i, h)),
            cspec,
            cspec,
            kspec,
            vspec,
            kspec,
            vspec,
            pl.BlockSpec((LANES, t_len), lambda b, h, i: (0, 0)),
        ],
        out_specs=qspec,
        out_shape=jax.ShapeDtypeStruct((m, Q_W), BF16),
        scratch_shapes=[pltpu.VMEM((tq, t_len), F32), pltpu.VMEM((rows, HEAD_DIM), F32),
                        pltpu.VMEM((rows, t_len), F32), pltpu.VMEM((rows, t_len), BF16),
                        pltpu.VMEM((rows, -(-t_len // SCORE_CHUNK) * LANES), F32),
                        pltpu.VMEM((rows, HEAD_DIM), F32),
                        pltpu.VMEM((rows, lw), F32), pltpu.VMEM((rows, lw), BF16),
                        pltpu.VMEM((rows, -(-lw // SCORE_CHUNK) * LANES), F32)],
        compiler_params=_params("parallel", "parallel", "arbitrary"),
        name="nsa_prompt",
    )(q_bf, qr_bf, gates, kc, vc, sel3, sel3, win3, win3, jnp.asarray(expand, BF16))


def _layer_norm(y, g, b):
    yc = y - jnp.mean(y, axis=-1, keepdims=True)
    var = jnp.mean(yc * yc, axis=-1, keepdims=True)
    return yc * lax.rsqrt(var + EPS) * g + b


def _glu(z):
    c = z.shape[1] // 2
    return z[:, :c] * _sigmoid(z[:, c:])


def _conv_prompt_kernel(z_ref, halo_ref, wdw_ref, bdw_ref, gln_ref, bln_ref, wout_ref, o_ref, wattn_ref,
                        gm_a_ref, gm_c_ref, mixed_ref, tail_ref, ufull, yconv, yattn, *, tc, width):
    i = pl.program_id(1)
    u = _glu(z_ref[...])
    uh = jnp.where(i == 0, 0.0, _glu(halo_ref[...]))
    ufull[0:CONV_HALO, :] = uh
    ufull[CONV_HALO:, :] = u
    tail_ref[0] = u[tc - CONV_HALO :, :]
    ch = u.shape[1]
    rc, cc = 64, 256
    off = CONV_HALO - (width - 1)
    d_out = wattn_ref.shape[1]
    n_conv_chunks = (tc // rc) * (ch // cc)
    aw = d_out // n_conv_chunks * 2
    step = 0
    for r0 in range(0, tc, rc):
        for c0 in range(0, ch, cc):
            if step % 2 == 0:
                a0 = step // 2 * aw
                yattn[:, a0 : a0 + aw] = jnp.dot(o_ref[...], wattn_ref[:, a0 : a0 + aw], preferred_element_type=F32)
            step += 1
            acc = jnp.zeros((rc, cc), F32)
            span_all = rc + CONV_HALO
            u_rows = ufull[r0 : r0 + span_all, c0 : c0 + cc]
            for b in range(min(SUBLANES, width)):
                n_a = (width - 1 - b) // SUBLANES + 1
                ub = pltpu.roll(u_rows, (span_all - off - b) % span_all, 0)
                for a in range(n_a):
                    k = SUBLANES * a + b
                    acc = acc + wdw_ref[k : k + 1, c0 : c0 + cc] * ub[SUBLANES * a : SUBLANES * a + rc, :]
            yconv[r0 : r0 + rc, c0 : c0 + cc] = acc + bdw_ref[:, c0 : c0 + cc]
    y = _layer_norm(yconv[...], gln_ref[...], bln_ref[...])
    y = y * _sigmoid(y)
    y_conv = jnp.dot(y.astype(BF16), wout_ref[...], preferred_element_type=F32)
    mixed_ref[...] = (_sigmoid(gm_a_ref[...]) * yattn[...] + _sigmoid(gm_c_ref[...]) * y_conv).astype(mixed_ref.dtype)


def _conv_mix_prompt(zmain, o, w_dw, b_dw, g_ln, b_ln, w_out, w_attn, n_b, t_len):
    tc = 256
    width, ch = w_dw.shape
    d_out = w_out.shape[1]
    ntc = t_len // tc
    hb = tc // CONV_HALO
    m = n_b * t_len
    glu_blk = 0
    gm_blk = (zmain.shape[1] - 2 * d_out) // d_out
    row1 = lambda b, i: (0, 0)
    tile = lambda blk: pl.BlockSpec((tc, d_out), lambda b, i: (b * ntc + i, blk))
    kern = functools.partial(_conv_prompt_kernel, tc=tc, width=width)
    return pl.pallas_call(
        kern,
        grid=(n_b, ntc),
        in_specs=[
            pl.BlockSpec((tc, 2 * ch), lambda b, i: (b * ntc + i, glu_blk)),
            pl.BlockSpec((CONV_HALO, 2 * ch), lambda b, i: (jnp.maximum((b * ntc + i) * hb - 1, 0), glu_blk)),
            pl.BlockSpec((width, ch), row1),
            pl.BlockSpec((1, ch), row1),
            pl.BlockSpec((1, ch), row1),
            pl.BlockSpec((1, ch), row1),
            pl.BlockSpec((ch, d_out), row1),
            pl.BlockSpec((tc, o.shape[1]), lambda b, i: (b * ntc + i, 0)),
            pl.BlockSpec(w_attn.shape, row1),
            tile(gm_blk),
            tile(gm_blk + 1),
        ],
        out_specs=[
            tile(0),
            pl.BlockSpec((1, CONV_HALO, ch), lambda b, i: (b, 0, 0)),
        ],
        out_shape=[jax.ShapeDtypeStruct((m, d_out), BF16), jax.ShapeDtypeStruct((n_b, CONV_HALO, ch), F32)],
        scratch_shapes=[pltpu.VMEM((tc + CONV_HALO, ch), F32), pltpu.VMEM((tc, ch), F32), pltpu.VMEM((tc, d_out), F32)],
        compiler_params=_params("parallel", "arbitrary"),
        name="conv_mix_prompt",
    )(zmain, zmain, w_dw, b_dw.reshape(1, ch), g_ln.reshape(1, ch), b_ln.reshape(1, ch), w_out, o, w_attn,
      zmain, zmain)


def _conv_sample_kernel(z_ref, st_ref, wdw_ref, bdw_ref, gln_ref, bln_ref, wout_ref, y_ref, u_ref, *, width):
    u = _glu(z_ref[...])
    u_ref[...] = u
    y = jnp.sum(st_ref[...] * wdw_ref[0 : width - 1, :][None], axis=1) + wdw_ref[width - 1 : width, :] * u
    y = _layer_norm(y + bdw_ref[...], gln_ref[...], bln_ref[...])
    y = y * _sigmoid(y)
    y_ref[...] = jnp.dot(y.astype(BF16), wout_ref[...], preferred_element_type=F32)


def _conv_sample(zmain, state, w_dw, b_dw, g_ln, b_ln, w_out):
    width, ch = w_dw.shape
    n_b = state.shape[0]
    d_out = w_out.shape[1]
    glu_blk = 0
    c2 = lambda i: (0, 0)
    return pl.pallas_call(
        functools.partial(_conv_sample_kernel, width=width),
        grid=(1,),
        in_specs=[
            pl.BlockSpec((n_b, 2 * ch), lambda i: (0, glu_blk)),
            pl.BlockSpec((n_b, width - 1, ch), lambda i: (0, 0, 0)),
            pl.BlockSpec((width, ch), c2),
            pl.BlockSpec((1, ch), c2),
            pl.BlockSpec((1, ch), c2),
            pl.BlockSpec((1, ch), c2),
            pl.BlockSpec((ch, d_out), c2),
        ],
        out_specs=[pl.BlockSpec((n_b, d_out), c2), pl.BlockSpec((n_b, ch), c2)],
        out_shape=[jax.ShapeDtypeStruct((n_b, d_out), F32), jax.ShapeDtypeStruct((n_b, ch), F32)],
        compiler_params=_params("arbitrary"),
        name="conv_sample",
    )(zmain, state, w_dw, b_dw.reshape(1, ch), g_ln.reshape(1, ch), b_ln.reshape(1, ch), w_out)


def _attn_mix_kernel(o_ref, w_ref, gm_a_ref, gm_c_ref, yc_ref, mixed_ref):
    ya = jnp.dot(o_ref[...], w_ref[...], preferred_element_type=F32)
    mixed_ref[...] = (_sigmoid(gm_a_ref[...]) * ya + _sigmoid(gm_c_ref[...]) * yc_ref[...]).astype(mixed_ref.dtype)


def _attn_mix(o, w_attn, zmain, y_conv, tm):
    m, k = o.shape
    d = w_attn.shape[1]
    gm_blk = (zmain.shape[1] - 2 * d) // d
    return pl.pallas_call(
        _attn_mix_kernel,
        grid=(m // tm,),
        in_specs=[
            pl.BlockSpec((tm, k), lambda i: (i, 0)),
            pl.BlockSpec((k, d), lambda i: (0, 0)),
            pl.BlockSpec((tm, d), lambda i: (i, gm_blk)),
            pl.BlockSpec((tm, d), lambda i: (i, gm_blk + 1)),
            pl.BlockSpec((tm, d), lambda i: (i, 0)),
        ],
        out_specs=pl.BlockSpec((tm, d), lambda i: (i, 0)),
        out_shape=jax.ShapeDtypeStruct((m, d), BF16),
        compiler_params=_params("parallel"),
        name="attn_out_mix",
    )(o, w_attn, zmain, zmain, y_conv)


def _out_proj_kernel(mixed_ref, w_ref, x_ref, h_ref):
    h_ref[...] = x_ref[...] + jnp.dot(mixed_ref[...], w_ref[...], preferred_element_type=F32)


def _out_proj(mixed, w_out, x, tm):
    m, k = mixed.shape
    d = w_out.shape[1]
    return pl.pallas_call(
        _out_proj_kernel,
        grid=(m // tm,),
        in_specs=[
            pl.BlockSpec((tm, k), lambda i: (i, 0)),
            pl.BlockSpec((k, d), lambda i: (0, 0)),
            pl.BlockSpec((tm, d), lambda i: (i, 0)),
        ],
        out_specs=pl.BlockSpec((tm, d), lambda i: (i, 0)),
        out_shape=jax.ShapeDtypeStruct((m, d), F32),
        compiler_params=_params("parallel"),
        name="out_proj",
    )(mixed, w_out, x)


def _gelu(x):
    return 0.5 * x * (1.0 + lax.erf(x * np.float32(np.sqrt(0.5))))


def _ffn_kernel(*refs, tm, tiles_per_seq, sample):
    if sample:
        h_ref, g_ref, wa_ref, wb_ref, wdw_ref, bdw_ref, wd_ref, s0_ref, s1_ref, out_ref, a_ref, hn, acc = refs
    else:
        h_ref, halo_ref, g_ref, wa_ref, wb_ref, wdw_ref, bdw_ref, wd_ref, out_ref, a_ref, hn, acc = refs
    i = pl.program_id(0)
    j = pl.program_id(1)
    lead = 0 if sample else FFN_HALO

    @pl.when(j == 0)
    def _():
        hn[lead:, :] = (_rms(h_ref[...]) * g_ref[...]).astype(BF16)
        if not sample:
            hn[0:lead, :] = (_rms(halo_ref[...]) * g_ref[...]).astype(BF16)
        acc[...] = jnp.zeros_like(acc)

    a_ext = jnp.dot(hn[...], wa_ref[...], preferred_element_type=F32)
    gate = jnp.dot(hn[lead:, :], wb_ref[...], preferred_element_type=F32)
    a0 = a_ext[lead:, :]
    if sample:
        a_m2 = s0_ref[...]
        a_m1 = s1_ref[...]
        a_ref[...] = a0
    else:
        first = (i % tiles_per_seq) == 0
        r = lax.broadcasted_iota(jnp.int32, a0.shape, 0)
        a_m2 = jnp.where(first & (r < 2), 0.0, a_ext[lead - 2 : lead - 2 + tm, :])
        a_m1 = jnp.where(first & (r < 1), 0.0, a_ext[lead - 1 : lead - 1 + tm, :])
        a_ref[0] = a0[tm - SUBLANES :, :]
    conv = wdw_ref[0:1, :] * a_m2 + wdw_ref[1:2, :] * a_m1 + wdw_ref[2:3, :] * a0 + bdw_ref[...]
    y = (_gelu(conv) * gate).astype(BF16)
    acc[...] += jnp.dot(y, wd_ref[...], preferred_element_type=F32)

    @pl.when(j == pl.num_programs(1) - 1)
    def _():
        out_ref[...] = h_ref[...] + acc[...]


def _conv_ffn(h, g, w_up_a, w_up_b, w_dw, b_dw, w_down, *, tm, t_len=None, state=None):
    m, d = h.shape
    dff = w_up_a.shape[1]
    tf = 512
    nf = dff // tf
    sample = state is not None
    assert w_dw.shape[0] == 3
    col = lambda i, j: (0, j)
    common = [
        pl.BlockSpec((1, d), lambda i, j: (0, 0)),
        pl.BlockSpec((d, tf), col),
        pl.BlockSpec((d, tf), col),
        pl.BlockSpec((3, tf), col),
        pl.BlockSpec((1, tf), col),
        pl.BlockSpec((tf, d), lambda i, j: (j, 0)),
    ]
    args = [g.reshape(1, d), w_up_a, w_up_b, w_dw, b_dw.reshape(1, dff), w_down]
    hspec = pl.BlockSpec((tm, d), lambda i, j: (i, 0))
    if sample:
        assert m == tm
        st = state.reshape(m, 2 * dff)
        in_specs = [hspec] + common + [pl.BlockSpec((tm, tf), col), pl.BlockSpec((tm, tf), lambda i, j: (0, nf + j))]
        args = [h] + args + [st, st]
        a_spec = pl.BlockSpec((tm, tf), col)
        a_shape = jax.ShapeDtypeStruct((m, dff), F32)
        tiles_per_seq = 1
        lead = 0
    else:
        tiles_per_seq = t_len // tm
        hb = tm // FFN_HALO
        in_specs = [hspec, pl.BlockSpec((FFN_HALO, d), lambda i, j: (jnp.maximum(i * hb - 1, 0), 0))] + common
        args = [h, h] + args
        a_spec = pl.BlockSpec((1, SUBLANES, tf), lambda i, j: (i, 0, j))
        a_shape = jax.ShapeDtypeStruct((m // tm, SUBLANES, dff), F32)
        lead = FFN_HALO
    kern = functools.partial(_ffn_kernel, tm=tm, tiles_per_seq=tiles_per_seq, sample=sample)
    return pl.pallas_call(
        kern,
        grid=(m // tm, nf),
        in_specs=in_specs,
        out_specs=[hspec, a_spec],
        out_shape=[jax.ShapeDtypeStruct((m, d), F32), a_shape],
        scratch_shapes=[pltpu.VMEM((tm + lead, d), BF16), pltpu.VMEM((tm, d), F32)],
        compiler_params=_params("parallel", "arbitrary"),
        name="conv_ffn",
    )(*args)


def _ple_kernel(h_ref, g_ref, wg_ref, pe_ref, wp_ref, o_ref):
    h = h_ref[...]
    hn = (_rms(h) * g_ref[...]).astype(BF16)
    gate = _sigmoid(jnp.dot(hn, wg_ref[...], preferred_element_type=F32))
    proj = jnp.dot(pe_ref[...], wp_ref[...], preferred_element_type=F32)
    o_ref[...] = h + gate * proj


def _ple(h, g, w_gate, pe, w_proj, tm):
    m, d = h.shape
    pd = pe.shape[1]
    return pl.pallas_call(
        _ple_kernel,
        grid=(m // tm,),
        in_specs=[
            pl.BlockSpec((tm, d), lambda i: (i, 0)),
            pl.BlockSpec((1, d), lambda i: (0, 0)),
            pl.BlockSpec((d, d), lambda i: (0, 0)),
            pl.BlockSpec((tm, pd), lambda i: (i, 0)),
            pl.BlockSpec((pd, d), lambda i: (0, 0)),
        ],
        out_specs=pl.BlockSpec((tm, d), lambda i: (i, 0)),
        out_shape=jax.ShapeDtypeStruct((m, d), F32),
        compiler_params=_params("parallel"),
        name="ple",
    )(h, g.reshape(1, d), w_gate, pe, w_proj)


def _cmp_sample_kernel(q_ref, kc_ref, vc_ref, ov_ref, score_ref, ocmp_ref, *, pos, nb, n_slots):
    scale = HEAD_DIM ** -0.5
    qn = q_ref[0]
    slot = lax.broadcasted_iota(jnp.int32, (GROUP, n_slots), 1)
    valid = (slot >= 1) & (slot * CMP_STRIDE + CMP_STRIDE <= pos + 1)
    sums = []
    for h in range(N_KV_HEADS):
        qh = qn[h * GROUP : (h + 1) * GROUP, :].astype(BF16)
        s = lax.dot_general(qh, kc_ref[0, h], NT_DIMS, preferred_element_type=F32) * scale
        e, l = _softmax_rows(jnp.where(valid, s, NEG_INF))
        p = jnp.where(valid, e / l, 0.0)
        ocmp_ref[0, h * GROUP : (h + 1) * GROUP, :] = jnp.dot(p.astype(BF16), vc_ref[0, h], preferred_element_type=F32)
        sums.append(jnp.sum(p, axis=0, keepdims=True))
    p_sum = jnp.concatenate(sums + [jnp.zeros((SUBLANES - N_KV_HEADS, n_slots), F32)], axis=0)
    imp = jnp.dot(p_sum, ov_ref[...], precision=lax.Precision.HIGHEST, preferred_element_type=F32)
    jb = lax.broadcasted_iota(jnp.int32, imp.shape, 1)
    cur = pos // SEL_BLOCK
    causal = (jb * SEL_BLOCK <= pos) & (jb < nb)
    forced = (jb == 0) | (jb == cur) | (jb == cur - 1)
    score_ref[0] = jnp.where(jb < nb, jnp.where(causal, jnp.where(forced, SEL_FORCE, imp), -1.0), -2.0)


def _cmp_sample(q3, kc, vc, pos, nb):
    n_b = q3.shape[0]
    n_slots = kc.shape[2]
    nbp = -(-nb // LANES) * LANES
    ratio = SEL_BLOCK // CMP_STRIDE
    mm = np.arange(n_slots)[:, None]
    jj = np.arange(nbp)[None, :]
    overlap = ((mm >= ratio * jj) & (mm <= ratio * jj + ratio) & (jj < nb)).astype(np.float32)
    qspec = pl.BlockSpec((1, N_HEADS, HEAD_DIM), lambda b: (b, 0, 0))
    cspec = pl.BlockSpec((1, N_KV_HEADS, n_slots, HEAD_DIM), lambda b: (b, 0, 0, 0))
    return pl.pallas_call(
        functools.partial(_cmp_sample_kernel, pos=pos, nb=nb, n_slots=n_slots),
        grid=(n_b,),
        in_specs=[qspec, cspec, cspec, pl.BlockSpec((n_slots, nbp), lambda b: (0, 0))],
        out_specs=[pl.BlockSpec((1, SUBLANES, nbp), lambda b: (b, 0, 0)), qspec],
        out_shape=[
            jax.ShapeDtypeStruct((n_b, SUBLANES, nbp), F32),
            jax.ShapeDtypeStruct((n_b, N_HEADS, HEAD_DIM), F32),
        ],
        compiler_params=_params("parallel"),
        name="cmp_sample",
    )(q3, kc, vc, jnp.asarray(overlap))


def _topk_kernel(score_ref, idx_ref, *, n_sel):
    sc = score_ref[...]
    lane = lax.broadcasted_iota(jnp.int32, sc.shape, 1)
    out_lane = lax.broadcasted_iota(jnp.int32, idx_ref.shape, 1)
    out = jnp.zeros(idx_ref.shape, jnp.int32)
    for it in range(n_sel):
        mx = jnp.max(sc, axis=-1, keepdims=True)
        idx = jnp.min(jnp.where(sc == mx, lane, sc.shape[1]), axis=-1, keepdims=True)
        out = jnp.where(out_lane == it, idx, out)
        sc = jnp.where(lane == idx, -3.0, sc)
    idx_ref[...] = out


def _topk(score, n_sel):
    rows = score.shape[0]
    return pl.pallas_call(
        functools.partial(_topk_kernel, n_sel=n_sel),
        out_shape=jax.ShapeDtypeStruct((rows, LANES), jnp.int32),
        compiler_params=pltpu.CompilerParams(vmem_limit_bytes=VMEM_LIMIT),
        name="topk_sample",
    )(score)


def _sel_sample_kernel(idx_ref, pt_ref, *refs, n_sel, n_past_blocks):
    del pt_ref
    blk_refs = refs[:n_sel]
    qr_ref, new_ref, o_ref = refs[n_sel:]
    b = pl.program_id(0)
    h = pl.program_id(1)
    scale = HEAD_DIM ** -0.5
    blk_rows = SEL_BLOCK * HEADS_PER_ROW
    qr = qr_ref[0, 0].astype(BF16)
    rows = jnp.concatenate([r[...] for r in blk_refs], axis=0).astype(BF16)
    s = lax.dot_general(qr, rows, NT_DIMS, preferred_element_type=F32) * scale
    lane = lax.broadcasted_iota(jnp.int32, s.shape, 1)
    slot = lane // blk_rows
    bias = jnp.where((lane & (HEADS_PER_ROW - 1)) == h, 0.0, NEG_INF)
    n_new = jnp.int32(0)
    for j in range(n_sel):
        is_new = idx_ref[(b * N_KV_HEADS + h) * n_sel + j] >= n_past_blocks
        bias = jnp.where(slot == j, jnp.where(is_new, NEG_INF, bias), bias)
        n_new = n_new + jnp.where(is_new, 1, 0)
    s = s + bias
    k_new = new_ref[0, pl.ds(h, 1), :].astype(BF16).astype(F32)
    v_new = new_ref[0, pl.ds(N_KV_HEADS + h, 1), :].astype(BF16).astype(F32)
    s_new = jnp.sum(qr.astype(F32) * k_new, axis=-1, keepdims=True) * scale
    s_new = jnp.where(n_new > 0, s_new, NEG_INF)
    m = jnp.maximum(jnp.max(s, axis=-1, keepdims=True), s_new)
    e = jnp.exp(s - m)
    e_new = jnp.exp(s_new - m)
    l = jnp.sum(e, axis=-1, keepdims=True) + e_new
    e_v = pltpu.roll(e, N_KV_HEADS, 1)
    o = jnp.dot(e_v.astype(BF16), rows, preferred_element_type=F32) + e_new.astype(BF16).astype(F32) * v_new
    o_ref[0, 0] = o / l


def _sel_sample(idx_flat, pt_flat, cache_rows, qr4, new_rows, n_sel, n_past_blocks, n_pp):
    n_b = qr4.shape[0]
    per_page = PAGE_SIZE // SEL_BLOCK
    blk_rows = SEL_BLOCK * HEADS_PER_ROW

    def bspec(j):
        def imap(b, h, idx, pt):
            blk_id = jnp.minimum(idx[(b * N_KV_HEADS + h) * n_sel + j], n_past_blocks - 1)
            return (pt[b * n_pp + blk_id // per_page] * per_page + blk_id % per_page, 0)
        return pl.BlockSpec((blk_rows, HEAD_DIM), imap)

    grid_spec = pltpu.PrefetchScalarGridSpec(
        num_scalar_prefetch=2,
        grid=(n_b, N_KV_HEADS),
        in_specs=[bspec(j) for j in range(n_sel)]
        + [
            pl.BlockSpec((1, 1, GROUP, HEAD_DIM), lambda b, h, idx, pt: (b, h, 0, 0)),
            pl.BlockSpec((1, HEADS_PER_ROW, HEAD_DIM), lambda b, h, idx, pt: (b, 0, 0)),
        ],
        out_specs=pl.BlockSpec((1, 1, GROUP, HEAD_DIM), lambda b, h, idx, pt: (b, h, 0, 0)),
    )
    return pl.pallas_call(
        functools.partial(_sel_sample_kernel, n_sel=n_sel, n_past_blocks=n_past_blocks),
        grid_spec=grid_spec,
        out_shape=jax.ShapeDtypeStruct((n_b, N_KV_HEADS, GROUP, HEAD_DIM), F32),
        compiler_params=_params("parallel", "arbitrary"),
        name="sel_sample",
    )(idx_flat, pt_flat, *([cache_rows] * n_sel), qr4, new_rows)


def _head_lane_mask(shape, kv_head_of_row):
    lane = lax.broadcasted_iota(jnp.int32, shape, 1)
    return (lane & (HEADS_PER_ROW - 1)) == kv_head_of_row


def _win_sample_kernel(win_ref, new_ref, qr_ref, ocmp_ref, osel_ref, g_ref, nw_ref, o_ref, *, wb):
    scale = HEAD_DIM ** -0.5
    keep = (wb - 1) * HEADS_PER_ROW
    nw_ref[0, 0:keep, :] = win_ref[0, HEADS_PER_ROW:, :]
    nw_ref[0, keep:, :] = new_ref[0]
    rows = nw_ref[0].astype(BF16)
    qr = qr_ref[0].astype(BF16)
    s = lax.dot_general(qr, rows, NT_DIMS, preferred_element_type=F32) * scale
    kv_head = lax.broadcasted_iota(jnp.int32, s.shape, 0) // GROUP
    s = jnp.where(_head_lane_mask(s.shape, kv_head), s, NEG_INF)
    e, l = _softmax_rows(s)
    e_v = pltpu.roll(e, N_KV_HEADS, 1)
    o_win = jnp.dot(e_v.astype(BF16), rows, preferred_element_type=F32) / l
    g = g_ref[0]
    o = g[:, 0:1] * ocmp_ref[0] + g[:, 1:2] * osel_ref[0] + g[:, 2:3] * o_win
    o_ref[0] = o.astype(o_ref.dtype)


def _win_sample(win_rows, new_rows, qr3, o_cmp, o_sel, g3):
    n_b, wr, _ = win_rows.shape
    wb = wr // HEADS_PER_ROW
    hspec = pl.BlockSpec((1, N_HEADS, HEAD_DIM), lambda b: (b, 0, 0))
    wspec = pl.BlockSpec((1, wr, HEAD_DIM), lambda b: (b, 0, 0))
    return pl.pallas_call(
        functools.partial(_win_sample_kernel, wb=wb),
        grid=(n_b,),
        in_specs=[
            wspec,
            pl.BlockSpec((1, HEADS_PER_ROW, HEAD_DIM), lambda b: (b, 0, 0)),
            hspec,
            hspec,
            hspec,
            pl.BlockSpec((1, N_HEADS, 3), lambda b: (b, 0, 0)),
        ],
        out_specs=[wspec, hspec],
        out_shape=[jax.ShapeDtypeStruct((n_b, wr, HEAD_DIM), F32), jax.ShapeDtypeStruct((n_b, N_HEADS, HEAD_DIM), BF16)],
        compiler_params=_params("parallel"),
        name="win_sample",
    )(win_rows, new_rows, qr3, o_cmp, o_sel, g3)


def _rope_tables(pos):
    half = HEAD_DIM // 2
    inv = ROPE_THETA ** (-jnp.arange(half, dtype=F32) / half)
    ang = pos.astype(F32)[:, None] * inv
    cos = jnp.cos(ang)
    sin = jnp.sin(ang)
    return jnp.concatenate([cos, cos], axis=-1), jnp.concatenate([-sin, sin], axis=-1)


def _prep_weights(lw):
    w_in = lw["w_in"]
    c = 0
    parts = {}
    conv_ch = lw["w_conv_dw"].shape[1]
    d_model = w_in.shape[0]
    for name, width in (("q", Q_W), ("cmp", ROW_W), ("sel", ROW_W), ("win", ROW_W), ("gnsa", 3 * N_HEADS),
                        ("glu", 2 * conv_ch), ("gm", 2 * d_model)):
        parts[name] = w_in[:, c : c + width]
        c += width
    wg = parts["gnsa"].reshape(d_model, 3, N_KV_HEADS, GROUP).transpose(0, 2, 1, 3).reshape(d_model, N_KV_HEADS, 3 * GROUP)
    wg = jnp.pad(wg, ((0, 0), (0, 0), (0, LANES - 3 * GROUP))).reshape(d_model, N_KV_HEADS * LANES)
    w1 = lw["w_cmp1"]
    hid = w1.shape[-1]
    r = CMP_BLOCK // CMP_STRIDE
    w1all = w1.reshape(2, r, CMP_STRIDE, HEAD_DIM, hid).transpose(2, 3, 0, 1, 4).reshape(CMP_STRIDE * HEAD_DIM, 2 * r * hid)
    dff = lw["w_up"].shape[1] // 2
    return {
        "w_q": parts["q"].astype(BF16),
        "w_main": jnp.concatenate([parts["glu"], parts["gm"]], axis=1).astype(BF16),
        "w_cmp_gate": jnp.concatenate([parts["cmp"], wg], axis=1).astype(BF16),
        "w_sel": parts["sel"].astype(BF16),
        "w_win": parts["win"].astype(BF16),
        "w1all": w1all.astype(BF16),
        "w2all": jnp.concatenate([lw["w_cmp2"][0], lw["w_cmp2"][1]], axis=1).astype(BF16),
        "w_attn_out": lw["w_attn_out"].astype(BF16),
        "w_conv_out": lw["w_conv_out"].astype(BF16),
        "w_out": lw["w_out"].astype(BF16),
        "w_up_a": lw["w_up"][:, :dff].astype(BF16),
        "w_up_b": lw["w_up"][:, dff:].astype(BF16),
        "w_down": lw["w_down"].astype(BF16),
        "w_ple_gate": lw["w_ple_gate"].astype(BF16),
        "w_ple_proj": lw["w_ple_proj"].astype(BF16),
    }


def _dense_tail(x, mixed, pe, lw, pw, *, tm, t_len=None, ffn_state=None):
    h1 = _out_proj(mixed, pw["w_out"], x, tm)
    h2, a_tail = _conv_ffn(h1, lw["g_ffn"], pw["w_up_a"], pw["w_up_b"], lw["w_ffn_dw"], lw["b_ffn_dw"], pw["w_down"],
                           tm=tm, t_len=t_len, state=ffn_state)
    h3 = _ple(h2, lw["g_ple"], pw["w_ple_gate"], pe.astype(BF16), pw["w_ple_proj"], tm)
    return h3, a_tail


def _layer_prompt(x3, pe3, lw, pw, bias_cmp):
    n_b, t_len, d = x3.shape
    m = n_b * t_len
    x = x3.reshape(m, d)
    cos, sin = _rope_tables(jnp.arange(t_len, dtype=jnp.int32))
    xn = _rmsnorm_bf16(x, lw["g_mix"], 512)
    zmain = _matmul(xn, pw["w_main"], 1024, 1024)
    rows_cmp, gates = _cmp_gate_proj(xn, pw["w_cmp_gate"], 512)
    rows_sel, sel_bf = _kv_proj(xn, pw["w_sel"], lw["g_k"][1], cos, sin, 512)
    rows_win, win_bf = _kv_proj(xn, pw["w_win"], lw["g_k"][2], cos, sin, 512)
    n_pp = t_len // PAGE_SIZE
    ident = jnp.arange(n_b * n_pp, dtype=jnp.int32).reshape(n_b, n_pp)
    kc, vc = _compress(rows_cmp.reshape(n_b * n_pp, PAGE_SIZE, HEADS_PER_ROW, HEAD_DIM), ident, pw["w1all"], bias_cmp,
                       pw["w2all"], lw["b_cmp2"], lw["g_k"][0])
    q_bf, qr_bf = _q_proj(xn, pw["w_q"], lw["g_q"], cos, sin, 512, BF16)
    o = _nsa_prompt(q_bf, qr_bf, gates, kc, vc, sel_bf, win_bf, n_b, t_len)
    mixed, u_tail = _conv_mix_prompt(zmain, o, lw["w_conv_dw"], lw["b_conv_dw"], lw["g_conv_ln"], lw["b_conv_ln"],
                                     pw["w_conv_out"], pw["w_attn_out"], n_b, t_len)
    h3, a_tail = _dense_tail(x, mixed, pe3.reshape(m, -1), lw, pw, tm=512, t_len=t_len)
    kvshape = (n_b, t_len, 2, N_KV_HEADS, HEAD_DIM)
    w_keep = min(WINDOW, t_len)
    conv_w = lw["w_conv_dw"].shape[0]
    state = (
        rows_cmp.reshape(kvshape),
        rows_sel.reshape(kvshape),
        rows_win.reshape(n_b, t_len * HEADS_PER_ROW, HEAD_DIM)[:, (t_len - w_keep) * HEADS_PER_ROW :].reshape(
            n_b, w_keep, 2, N_KV_HEADS, HEAD_DIM),
        u_tail[:, CONV_HALO - (conv_w - 1) :],
        a_tail.reshape(n_b, -1, SUBLANES, a_tail.shape[-1])[:, -1, SUBLANES - (lw["w_ffn_dw"].shape[0] - 1) :],
    )
    return h3.reshape(n_b, t_len, d), state


def _layer_sample(x3, pe3, lw, pw, bias_cmp, cache_cmp, cache_sel, cache_win, conv_state, ffn_state, page_table):
    n_b, t_new, d = x3.shape
    assert t_new == 1
    n_pp = page_table.shape[1]
    past_len = n_pp * PAGE_SIZE
    pos = past_len
    wb = cache_win.shape[1]
    assert wb == WINDOW and past_len % SEL_BLOCK == 0
    x = x3.reshape(n_b, d)
    cos, sin = _rope_tables(jnp.full((n_b,), pos, dtype=jnp.int32))
    xn = _rmsnorm_bf16(x, lw["g_mix"], n_b)
    zmain = _matmul(xn, pw["w_main"], n_b, 1024)
    rows_cmp, gates = _cmp_gate_proj(xn, pw["w_cmp_gate"], n_b)
    rows_sel, _ = _kv_proj(xn, pw["w_sel"], lw["g_k"][1], cos, sin, n_b)
    rows_win, _ = _kv_proj(xn, pw["w_win"], lw["g_k"][2], cos, sin, n_b)
    n_pool = cache_cmp.shape[0]
    kc, vc = _compress(cache_cmp.reshape(n_pool, PAGE_SIZE, HEADS_PER_ROW, HEAD_DIM), page_table, pw["w1all"], bias_cmp,
                       pw["w2all"], lw["b_cmp2"], lw["g_k"][0])
    n_past_blocks = past_len // SEL_BLOCK
    nb = n_past_blocks + 1
    n_sel = min(N_SEL, nb)
    q_s, qr_s = _q_proj(xn, pw["w_q"], lw["g_q"], cos, sin, n_b, F32)
    q3 = q_s.reshape(n_b, N_HEADS, HEAD_DIM)
    qr3 = qr_s.reshape(n_b, N_HEADS, HEAD_DIM)
    score, o_cmp = _cmp_sample(q3, kc, vc, pos, nb)
    idx = _topk(score.reshape(n_b * SUBLANES, -1), n_sel)
    idx = idx.reshape(n_b, SUBLANES, LANES)[:, :N_KV_HEADS, :n_sel]
    o_sel = _sel_sample(idx.reshape(-1), page_table.reshape(-1),
                        cache_sel.reshape(n_pool * PAGE_SIZE * HEADS_PER_ROW, HEAD_DIM),
                        qr3.reshape(n_b, N_KV_HEADS, GROUP, HEAD_DIM), rows_sel.reshape(n_b, HEADS_PER_ROW, HEAD_DIM),
                        n_sel, n_past_blocks, n_pp)
    g3 = gates.reshape(n_b, N_KV_HEADS, LANES)[:, :, : 3 * GROUP].reshape(n_b, N_KV_HEADS, 3, GROUP)
    g3 = g3.transpose(0, 1, 3, 2).reshape(n_b, N_HEADS, 3)
    new_win, o = _win_sample(cache_win.reshape(n_b, wb * HEADS_PER_ROW, HEAD_DIM),
                             rows_win.reshape(n_b, HEADS_PER_ROW, HEAD_DIM), qr3, o_cmp,
                             o_sel.reshape(n_b, N_HEADS, HEAD_DIM), g3)
    y_conv, u = _conv_sample(zmain, conv_state, lw["w_conv_dw"], lw["b_conv_dw"], lw["g_conv_ln"], lw["b_conv_ln"],
                             pw["w_conv_out"])
    mixed = _attn_mix(o.reshape(n_b, Q_W), pw["w_attn_out"], zmain, y_conv, n_b)
    h3, a_new = _dense_tail(x, mixed, pe3.reshape(n_b, -1), lw, pw, tm=n_b, ffn_state=ffn_state)
    kvshape = (n_b, 1, 2, N_KV_HEADS, HEAD_DIM)
    state = (
        rows_cmp.reshape(kvshape),
        rows_sel.reshape(kvshape),
        new_win.reshape(n_b, wb, 2, N_KV_HEADS, HEAD_DIM),
        jnp.concatenate([conv_state[:, 1:], u[:, None, :]], axis=1),
        jnp.concatenate([ffn_state[:, 1:], a_new[:, None, :]], axis=1),
    )
    return h3.reshape(n_b, 1, d), state


_LAYER_WEIGHTS = ("g_mix", "w_in", "g_q", "g_k", "w_cmp1", "b_cmp1", "w_cmp2", "b_cmp2", "pe_cmp", "w_attn_out",
                  "w_conv_dw", "b_conv_dw", "g_conv_ln", "b_conv_ln", "w_conv_out", "w_out", "g_ffn", "w_up",
                  "w_ffn_dw", "b_ffn_dw", "w_down", "g_ple", "w_ple_gate", "w_ple_proj")


def kernel(x_prompt, x_sample, p_prompt, p_sample, cache_cmp_kv, cache_sel_kv, cache_win_kv, state_conv, state_ffn_conv, page_table, g_mix, w_in, g_q, g_k, w_cmp1, b_cmp1, w_cmp2, b_cmp2, pe_cmp, w_attn_out, w_conv_dw, b_conv_dw, g_conv_ln, b_conv_ln, w_conv_out, w_out, g_ffn, w_up, w_ffn_dw, b_ffn_dw, w_down, g_ple, w_ple_gate, w_ple_proj):
    stacked = dict(zip(_LAYER_WEIGHTS, (g_mix, w_in, g_q, g_k, w_cmp1, b_cmp1, w_cmp2, b_cmp2, pe_cmp, w_attn_out,
                                        w_conv_dw, b_conv_dw, g_conv_ln, b_conv_ln, w_conv_out, w_out, g_ffn, w_up,
                                        w_ffn_dw, b_ffn_dw, w_down, g_ple, w_ple_gate, w_ple_proj)))
    depth = w_in.shape[0]
    hp, hs = x_prompt, x_sample
    states_p, states_s = [], []
    for i in range(depth):
        lw = {k: v[i] for k, v in stacked.items()}
        pw = _prep_weights(lw)
        bias_cmp = _pe_bias(lw["pe_cmp"], lw["w_cmp1"], lw["b_cmp1"])
        hp, st_p = _layer_prompt(hp, p_prompt[i], lw, pw, bias_cmp)
        hs, st_s = _layer_sample(hs, p_sample[i], lw, pw, bias_cmp, cache_cmp_kv[i], cache_sel_kv[i], cache_win_kv[i],
                                 state_conv[i], state_ffn_conv[i], page_table)
        states_p.append(st_p)
        states_s.append(st_s)
    outs = [hp, hs]
    for k in range(5):
        outs.append(jnp.stack([s[k] for s in states_p]))
        outs.append(jnp.stack([s[k] for s in states_s]))
    return tuple(outs)
```

```python
import functools

import numpy as np
import jax
import jax.numpy as jnp
from jax import lax
from jax.experimental import pallas as pl
from jax.experimental.pallas import tpu as pltpu

N_HEADS = 16
HEAD_DIM = 128
N_KV_HEADS = 4
GROUP = N_HEADS // N_KV_HEADS
Q_W = N_HEADS * HEAD_DIM
KV_W = N_KV_HEADS * HEAD_DIM
ROW_W = 2 * KV_W
HEADS_PER_ROW = 2 * N_KV_HEADS
CMP_BLOCK = 32
CMP_STRIDE = 16
SEL_BLOCK = 64
SEL_SHIFT = 6
N_SEL = 16
WINDOW = 512
PAGE_SIZE = 128
ROPE_THETA = 10000.0
EPS = 1e-6
NEG_INF = -1e30
SEL_FORCE = 1e6
MASK_BIG = 2.0 ** 100
CONV_HALO = 32
FFN_HALO = 16
SCORE_CHUNK = 512
ROW_TILE = 128
MAX_PAGES_PER_STEP = 32

LANES = 128
SUBLANES = 8
VMEM_LIMIT = 56 * 1024 * 1024

F32 = jnp.float32
BF16 = jnp.bfloat16
NT_DIMS = (((1,), (1,)), ((), ()))


def _params(*sem):
    return pltpu.CompilerParams(dimension_semantics=sem, vmem_limit_bytes=VMEM_LIMIT)


def _rms(x):
    return x * lax.rsqrt(jnp.mean(x * x, axis=-1, keepdims=True) + EPS)


def _rope(x, cos, sin_signed):
    return x * cos + pltpu.roll(x, HEAD_DIM // 2, 1) * sin_signed


def _sigmoid(x):
    return jax.nn.sigmoid(x)


def _softmax_rows(s):
    m = jnp.max(s, axis=-1, keepdims=True)
    e = jnp.exp(s - m)
    return e, jnp.sum(e, axis=-1, keepdims=True)


def _softmax_rows_scaled(s, scale):
    m = jnp.max(s, axis=-1, keepdims=True)
    e = jnp.exp2((s - m) * np.float32(scale * np.log2(np.e)))
    return e, jnp.sum(e, axis=-1, keepdims=True)


def _rmsnorm_kernel(x_ref, g_ref, o_ref):
    o_ref[...] = (_rms(x_ref[...]) * g_ref[...]).astype(o_ref.dtype)


def _rmsnorm_bf16(x, g, tm):
    m, d = x.shape
    return pl.pallas_call(
        _rmsnorm_kernel,
        grid=(m // tm,),
        in_specs=[pl.BlockSpec((tm, d), lambda i: (i, 0)), pl.BlockSpec((1, d), lambda i: (0, 0))],
        out_specs=pl.BlockSpec((tm, d), lambda i: (i, 0)),
        out_shape=jax.ShapeDtypeStruct((m, d), BF16),
        compiler_params=_params("parallel"),
        name="rmsnorm_in",
    )(x, g.reshape(1, d))


def _matmul_kernel(x_ref, w_ref, o_ref):
    o_ref[...] = jnp.dot(x_ref[...], w_ref[...], preferred_element_type=F32)


def _matmul(x, w, tm, tn):
    m, k = x.shape
    n = w.shape[1]
    return pl.pallas_call(
        _matmul_kernel,
        grid=(m // tm, n // tn),
        in_specs=[pl.BlockSpec((tm, k), lambda i, j: (i, 0)), pl.BlockSpec((k, tn), lambda i, j: (0, j))],
        out_specs=pl.BlockSpec((tm, tn), lambda i, j: (i, j)),
        out_shape=jax.ShapeDtypeStruct((m, n), F32),
        compiler_params=_params("parallel", "arbitrary"),
        name="in_proj_main",
    )(x, w)


def _q_kernel(x_ref, w_ref, gq_ref, cos_ref, sin_ref, q_ref, qr_ref):
    cos = cos_ref[...]
    sin = sin_ref[...]
    gq = gq_ref[...]
    x = x_ref[...]
    cw = 2 * LANES
    for c0 in range(0, w_ref.shape[1], cw):
        acc = jnp.dot(x, w_ref[:, c0 : c0 + cw], preferred_element_type=F32)
        for h0 in range(0, cw, HEAD_DIM):
            sl = slice(c0 + h0, c0 + h0 + HEAD_DIM)
            qn = _rms(acc[:, h0 : h0 + HEAD_DIM]) * gq
            q_ref[:, sl] = qn.astype(q_ref.dtype)
            qr_ref[:, sl] = _rope(qn, cos, sin).astype(qr_ref.dtype)


def _q_proj(xn, w, gq, cos, sin, tm, out_dtype):
    m, k = xn.shape
    n = w.shape[1]
    nt = cos.shape[0] // tm
    ospec = pl.BlockSpec((tm, n), lambda i: (i, 0))
    tspec = pl.BlockSpec((tm, HEAD_DIM), lambda i: (i % nt, 0))
    return pl.pallas_call(
        _q_kernel,
        grid=(m // tm,),
        in_specs=[pl.BlockSpec((tm, k), lambda i: (i, 0)), pl.BlockSpec((k, n), lambda i: (0, 0)),
                  pl.BlockSpec((1, HEAD_DIM), lambda i: (0, 0)), tspec, tspec],
        out_specs=[ospec, ospec],
        out_shape=[jax.ShapeDtypeStruct((m, n), out_dtype), jax.ShapeDtypeStruct((m, n), out_dtype)],
        compiler_params=_params("parallel"),
        name="in_proj_q",
    )(xn, w, gq.reshape(1, HEAD_DIM), cos, sin)


def _store_rows(rows_ref, acc):
    tm = acc.shape[0]
    for j in range(HEADS_PER_ROW):
        rows_ref[pl.ds(j, tm, stride=HEADS_PER_ROW), :] = acc[:, j * HEAD_DIM : (j + 1) * HEAD_DIM]


def _cmp_gate_kernel(x_ref, w_ref, rows_ref, gate_ref):
    acc = jnp.dot(x_ref[...], w_ref[...], preferred_element_type=F32)
    _store_rows(rows_ref, acc[:, :ROW_W])
    gate_ref[...] = _sigmoid(acc[:, ROW_W:])


def _cmp_gate_proj(xn, w, tm):
    m, k = xn.shape
    n = w.shape[1]
    ng = n - ROW_W
    return pl.pallas_call(
        _cmp_gate_kernel,
        grid=(m // tm,),
        in_specs=[pl.BlockSpec((tm, k), lambda i: (i, 0)), pl.BlockSpec((k, n), lambda i: (0, 0))],
        out_specs=[pl.BlockSpec((tm * HEADS_PER_ROW, HEAD_DIM), lambda i: (i, 0)),
                   pl.BlockSpec((tm, ng), lambda i: (i, 0))],
        out_shape=[jax.ShapeDtypeStruct((m * HEADS_PER_ROW, HEAD_DIM), F32), jax.ShapeDtypeStruct((m, ng), F32)],
        compiler_params=_params("parallel"),
        name="in_proj_cmp_gates",
    )(xn, w)


def _kv_kernel(x_ref, w_ref, gk_ref, cos_ref, sin_ref, rows_ref, rows_bf_ref):
    acc = jnp.dot(x_ref[...], w_ref[...], preferred_element_type=F32)
    cos = cos_ref[...]
    sin = sin_ref[...]
    gk = gk_ref[...]
    tm = acc.shape[0]
    for h in range(N_KV_HEADS):
        sl = slice(h * HEAD_DIM, (h + 1) * HEAD_DIM)
        k = _rope(_rms(acc[:, sl]) * gk, cos, sin)
        rows_ref[pl.ds(h, tm, stride=HEADS_PER_ROW), :] = k
        rows_bf_ref[:, sl] = k.astype(BF16)
        sv = slice(KV_W + h * HEAD_DIM, KV_W + (h + 1) * HEAD_DIM)
        rows_ref[pl.ds(N_KV_HEADS + h, tm, stride=HEADS_PER_ROW), :] = acc[:, sv]
    rows_bf_ref[:, KV_W:] = acc[:, KV_W:].astype(BF16)


def _kv_proj(xn, w, gk, cos, sin, tm):
    m, k = xn.shape
    nt = cos.shape[0] // tm
    return pl.pallas_call(
        _kv_kernel,
        grid=(m // tm,),
        in_specs=[
            pl.BlockSpec((tm, k), lambda i: (i, 0)),
            pl.BlockSpec((k, ROW_W), lambda i: (0, 0)),
            pl.BlockSpec((1, HEAD_DIM), lambda i: (0, 0)),
            pl.BlockSpec((tm, HEAD_DIM), lambda i: (i % nt, 0)),
            pl.BlockSpec((tm, HEAD_DIM), lambda i: (i % nt, 0)),
        ],
        out_specs=[pl.BlockSpec((tm * HEADS_PER_ROW, HEAD_DIM), lambda i: (i, 0)),
                   pl.BlockSpec((tm, ROW_W), lambda i: (i, 0))],
        out_shape=[jax.ShapeDtypeStruct((m * HEADS_PER_ROW, HEAD_DIM), F32), jax.ShapeDtypeStruct((m, ROW_W), BF16)],
        compiler_params=_params("parallel"),
        name="in_proj_kv",
    )(xn, w, gk.reshape(1, HEAD_DIM), cos, sin)


def _pe_bias_kernel(pe_ref, w1_ref, b1_ref, o_ref):
    for kv in range(2):
        acc = jnp.zeros((SUBLANES, HEAD_DIM), F32)
        for r in range(CMP_BLOCK):
            row = jnp.broadcast_to(pe_ref[kv, r : r + 1, :], (SUBLANES, HEAD_DIM)).astype(BF16)
            acc = acc + jnp.dot(
                row, w1_ref[kv, r * HEAD_DIM : (r + 1) * HEAD_DIM, :].astype(BF16), preferred_element_type=F32
            )
        o_ref[kv] = acc[0:1, :] + b1_ref[kv]


def _pe_bias(pe, w1, b1):
    hid = w1.shape[-1]
    return pl.pallas_call(
        _pe_bias_kernel,
        out_shape=jax.ShapeDtypeStruct((2, 1, hid), F32),
        compiler_params=pltpu.CompilerParams(vmem_limit_bytes=VMEM_LIMIT),
        name="cmp_pe_bias",
    )(pe, w1, b1.reshape(2, 1, hid))


def _compress_kernel(pt_ref, *refs, n_pg):
    del pt_ref
    page_refs = refs[:n_pg]
    w1_ref, bias_ref, w2_ref, b2_ref, gk_ref, kc_ref, vc_ref, carry_ref, out_scr = refs[n_pg:]
    sub_per_page = PAGE_SIZE // CMP_STRIDE
    nsub = n_pg * sub_per_page
    rows = nsub * HEADS_PER_ROW
    page_rows = sub_per_page * HEADS_PER_ROW

    @pl.when(pl.program_id(1) == 0)
    def _():
        carry_ref[...] = jnp.zeros_like(carry_ref)

    half = rows // 2
    hid2 = w1_ref.shape[1] // 2
    hid_w = hid2 // 2
    top = lax.broadcasted_iota(jnp.int32, (nsub // 2, HEADS_PER_ROW, 2 * HEAD_DIM), 1) < N_KV_HEADS
    acc = [None, None]
    for pp in range(0, CMP_STRIDE, 2):
        pieces = []
        for pg in range(n_pg):
            a = page_refs[pg][0, pl.ds(pp, sub_per_page, stride=CMP_STRIDE), :, :].reshape(page_rows, HEAD_DIM)
            b = page_refs[pg][0, pl.ds(pp + 1, sub_per_page, stride=CMP_STRIDE), :, :].reshape(page_rows, HEAD_DIM)
            pieces.append(jnp.concatenate([a, b], axis=1))
        x3 = jnp.concatenate(pieces, axis=0).reshape(nsub // 2, 2 * HEADS_PER_ROW, 2 * HEAD_DIM)
        even = x3[:, :HEADS_PER_ROW]
        odd = x3[:, HEADS_PER_ROW:]
        xs = (jnp.where(top, even, pltpu.roll(odd, N_KV_HEADS, 1)),
              jnp.where(top, pltpu.roll(even, N_KV_HEADS, 1), odd))
        w1 = w1_ref[pp * HEAD_DIM : (pp + 2) * HEAD_DIM, :]
        for kv in range(2):
            d = jnp.dot(xs[kv].reshape(half, 2 * HEAD_DIM).astype(BF16), w1[:, kv * hid2 : (kv + 1) * hid2],
                        preferred_element_type=F32)
            acc[kv] = d if acc[kv] is None else acc[kv] + d
    first_rows = lax.broadcasted_iota(jnp.int32, (half, hid_w), 0) < N_KV_HEADS
    pad = jnp.zeros((half - SUBLANES, hid_w), F32)
    for kv in range(2):
        p_first = acc[kv][:, :hid_w]
        p_second = acc[kv][:, hid_w:]
        rolled = pltpu.roll(p_first, N_KV_HEADS, 0)
        shifted = jnp.where(first_rows, jnp.concatenate([carry_ref[kv], pad], axis=0), rolled)
        carry_ref[kv] = rolled[0:SUBLANES, :]
        hid = shifted + p_second + bias_ref[kv]
        hid = hid * _sigmoid(hid)
        out = jnp.dot(hid.astype(BF16), w2_ref[:, kv * HEAD_DIM : (kv + 1) * HEAD_DIM],
                      preferred_element_type=F32) + b2_ref[kv]
        if kv == 0:
            out = _rms(out) * gk_ref[...]
        out_scr[kv] = out
        dst = kc_ref if kv == 0 else vc_ref
        for h in range(N_KV_HEADS):
            dst[0, h, :, :] = out_scr[kv, pl.ds(h, nsub, stride=N_KV_HEADS), :].astype(BF16)


def _compress(pages, page_table, w1all, bias, w2all, b2, gk):
    n_seq, n_pp = page_table.shape
    n_pg = max(d for d in range(1, MAX_PAGES_PER_STEP + 1) if n_pp % d == 0)
    steps = n_pp // n_pg
    nsub = n_pg * (PAGE_SIZE // CMP_STRIDE)
    n_slots = steps * nsub
    hid = bias.shape[-1]

    def page_spec(k):
        return pl.BlockSpec((1, PAGE_SIZE, HEADS_PER_ROW, HEAD_DIM),
                            lambda b, s, pt: (pt[b * n_pp + s * n_pg + k], 0, 0, 0))

    const2 = lambda b, s, pt: (0, 0)
    const3 = lambda b, s, pt: (0, 0, 0)
    out_spec = pl.BlockSpec((1, N_KV_HEADS, nsub, HEAD_DIM), lambda b, s, pt: (b, 0, s, 0))
    grid_spec = pltpu.PrefetchScalarGridSpec(
        num_scalar_prefetch=1,
        grid=(n_seq, steps),
        in_specs=[page_spec(k) for k in range(n_pg)]
        + [
            pl.BlockSpec(w1all.shape, const2),
            pl.BlockSpec((2, 1, hid), const3),
            pl.BlockSpec(w2all.shape, const2),
            pl.BlockSpec((2, 1, HEAD_DIM), const3),
            pl.BlockSpec((1, HEAD_DIM), const2),
        ],
        out_specs=[out_spec, out_spec],
        scratch_shapes=[pltpu.VMEM((2, SUBLANES, hid), F32), pltpu.VMEM((2, nsub * N_KV_HEADS, HEAD_DIM), F32)],
    )
    shape = jax.ShapeDtypeStruct((n_seq, N_KV_HEADS, n_slots, HEAD_DIM), BF16)
    return pl.pallas_call(
        functools.partial(_compress_kernel, n_pg=n_pg),
        grid_spec=grid_spec,
        out_shape=[shape, shape],
        compiler_params=_params("parallel", "arbitrary"),
        name="compress",
    )(page_table.reshape(-1), *([pages] * n_pg), w1all, bias, w2all, b2.reshape(2, 1, HEAD_DIM), gk.reshape(1, HEAD_DIM))


def _nsa_prompt_kernel(
    q_ref, qr_ref, gate_ref, kc_ref, vc_ref, ks_ref, vs_ref, kw_ref, vw_ref, onehot_ref,
    o_ref, osel_scr, s_scr, p_scr, m_scr, owin_scr, sw_scr, pw_scr, mw_scr,
    *, tq, t_len, nb, n_sel, n_slots, kt):
    i = pl.program_id(2)
    q0 = i * tq
    rows = GROUP * tq
    scale = HEAD_DIM ** -0.5
    heads = [slice(g * HEAD_DIM, (g + 1) * HEAD_DIM) for g in range(GROUP)]
    q = jnp.concatenate([q_ref[:, sl] for sl in heads], axis=0)
    qr = jnp.concatenate([qr_ref[:, sl] for sl in heads], axis=0)

    def attend(lhs, rhs_chunk, v_ref, k0, n_keys, mask_chunk, s_buf, p_buf, m_buf):
        n_chunks = -(-n_keys // SCORE_CHUNK)
        for c in range(n_chunks):
            c0 = c * SCORE_CHUNK
            w = min(SCORE_CHUNK, n_keys - c0)
            s = lax.dot_general(lhs, rhs_chunk(c0, w), NT_DIMS, preferred_element_type=F32)
            s = mask_chunk(c0, w, s.reshape(GROUP, tq, w)).reshape(rows, w)
            s_buf[:, c0 : c0 + w] = s
            pm = s[:, 0:LANES]
            for j0 in range(LANES, w, LANES):
                pm = jnp.maximum(pm, s[:, j0 : j0 + LANES])
            m_buf[:, c * LANES : (c + 1) * LANES] = pm
        coef = np.float32(scale * np.log2(np.e))
        outs = []
        for g in range(GROUP):
            sums = []
            for r0 in range(g * tq, (g + 1) * tq, ROW_TILE):
                r = slice(r0, r0 + ROW_TILE)
                m_acc = m_buf[r, 0:LANES]
                for c in range(1, n_chunks):
                    m_acc = jnp.maximum(m_acc, m_buf[r, c * LANES : (c + 1) * LANES])
                m_b = jnp.broadcast_to(jnp.max(m_acc, axis=-1, keepdims=True), (ROW_TILE, LANES))
                l_acc = jnp.zeros((ROW_TILE, LANES), F32)
                for c0 in range(0, n_keys, LANES):
                    p = jnp.exp2((s_buf[r, c0 : c0 + LANES] - m_b) * coef)
                    l_acc = l_acc + p
                    p_buf[r, c0 : c0 + LANES] = p.astype(BF16)
                sums.append(jnp.sum(l_acc, axis=-1, keepdims=True))
            rg = slice(g * tq, (g + 1) * tq)
            pv = jnp.dot(p_buf[rg, 0:n_keys], v_ref[0, pl.ds(k0, n_keys), :], preferred_element_type=F32)
            outs.append(pv / jnp.concatenate(sums, axis=0))
        return jnp.concatenate(outs, axis=0)

    s_t = lax.dot_general(kc_ref[0, 0], q, NT_DIMS, preferred_element_type=F32) * scale
    slot = lax.broadcasted_iota(jnp.int32, (n_slots, rows), 0)
    tok = q0 + (lax.broadcasted_iota(jnp.int32, (n_slots, rows), 1) & (tq - 1))
    valid = (slot >= 1) & (slot * CMP_STRIDE + CMP_STRIDE <= tok + 1)
    s_m = jnp.where(valid, s_t, NEG_INF)
    e = jnp.exp(s_m - jnp.max(s_m, axis=0, keepdims=True))
    p_t = jnp.where(valid, e / jnp.sum(e, axis=0, keepdims=True), 0.0)
    o_cmp = jnp.dot(p_t.T.astype(BF16), vc_ref[0, 0], preferred_element_type=F32)

    p_sum = p_t[:, 0:tq]
    for g in range(1, GROUP):
        p_sum = p_sum + p_t[:, g * tq : (g + 1) * tq]
    ratio = SEL_BLOCK // CMP_STRIDE
    jj = lax.broadcasted_iota(jnp.int32, (nb, n_slots), 0)
    mm = lax.broadcasted_iota(jnp.int32, (nb, n_slots), 1)
    overlap = jnp.where((mm >= ratio * jj) & (mm <= ratio * jj + ratio), 1.0, 0.0)
    imp = jnp.dot(overlap, p_sum, precision=lax.Precision.HIGHEST, preferred_element_type=F32)
    jb = lax.broadcasted_iota(jnp.int32, (nb, tq), 0)
    pos = q0 + lax.broadcasted_iota(jnp.int32, (nb, tq), 1)
    cur = lax.shift_right_logical(pos, SEL_SHIFT)
    causal = jb * SEL_BLOCK <= pos
    forced = (jb == 0) | (jb == cur) | (jb == cur - 1)
    score = jnp.where(causal, jnp.where(forced, SEL_FORCE, imp), -1.0)
    cnts = [jnp.zeros((nb, tq), F32) for _ in range(4)]
    for i2 in range(nb):
        r = score[i2 : i2 + 1, :]
        cnts[i2 % 4] = cnts[i2 % 4] + jnp.where(r > score, 1.0, jnp.where((r == score) & (jb > i2), 1.0, 0.0))
    cnt = (cnts[0] + cnts[1]) + (cnts[2] + cnts[3])
    sel = jnp.where((cnt < n_sel) & causal, 1.0, 0.0)
    if nb < LANES:
        sel = jnp.concatenate([sel, jnp.zeros((LANES - nb, tq), F32)], axis=0)
    sel_q = ((sel.T - 1.0) * MASK_BIG).astype(BF16)
    lhs_sel = jnp.concatenate([qr, jnp.concatenate([sel_q] * GROUP, axis=0)], axis=1)

    def sel_rhs(c0, w):
        return jnp.concatenate([ks_ref[0, c0 : c0 + w, :], onehot_ref[c0 : c0 + w, :]], axis=1)

    def sel_mask(c0, w, s, first_diag_key):
        if c0 + w <= first_diag_key:
            return s
        kpos = c0 + lax.broadcasted_iota(jnp.int32, (tq, w), 1)
        qpos = q0 + lax.broadcasted_iota(jnp.int32, (tq, w), 0)
        return jnp.where((kpos <= qpos)[None], s, -MASK_BIG)

    lw = WINDOW + tq
    kw0 = pl.multiple_of(jnp.maximum(q0 - WINDOW, 0), tq)

    def window_mask(c0, w, s):
        diff = (q0 + lax.broadcasted_iota(jnp.int32, (tq, w), 0)) - (
            kw0 + c0 + lax.broadcasted_iota(jnp.int32, (tq, w), 1))
        return s + jnp.where((diff >= 0) & (diff < WINDOW), 0.0, NEG_INF)[None]

    owin_scr[...] = attend(qr, lambda c0, w: kw_ref[0, pl.ds(kw0 + c0, w), :], vw_ref, kw0, lw, window_mask,
                           sw_scr, pw_scr, mw_scr)

    n_tiles = (q0 + tq + kt - 1) // kt
    for nt in range(1, t_len // kt + 1):

        @pl.when(n_tiles == nt)
        def _(nt=nt):
            osel_scr[...] = attend(lhs_sel, sel_rhs, vs_ref, 0, nt * kt,
                                   functools.partial(sel_mask, first_diag_key=(nt - 1) * kt), s_scr, p_scr, m_scr)

    o_sel = osel_scr[...]
    o_win = owin_scr[...]
    gt = gate_ref[...]
    outs = []
    for g in range(GROUP):
        sl = slice(g * tq, (g + 1) * tq)
        outs.append(
            gt[:, g : g + 1] * o_cmp[sl]
            + gt[:, GROUP + g : GROUP + g + 1] * o_sel[sl]
            + gt[:, 2 * GROUP + g : 2 * GROUP + g + 1] * o_win[sl]
        )
    o_ref[...] = jnp.concatenate(outs, axis=1).astype(o_ref.dtype)


def _nsa_prompt(q_bf, qr_bf, gates, kc, vc, sel_bf, win_bf, n_b, t_len):
    tq = 128
    kt = min(512, t_len)
    nb = t_len // SEL_BLOCK
    n_slots = kc.shape[2]
    n_sel = min(N_SEL, nb)
    assert t_len % kt == 0 and t_len >= WINDOW + tq and nb <= LANES and n_slots == t_len // CMP_STRIDE
    ntq = t_len // tq
    m = n_b * t_len
    rows = GROUP * tq
    lw = WINDOW + tq
    onehot = (np.arange(t_len)[:, None] // SEL_BLOCK == np.arange(LANES)[None, :]).astype(np.float32)
    sel3 = sel_bf.reshape(n_b, t_len, ROW_W)
    win3 = win_bf.reshape(n_b, t_len, ROW_W)
    kspec = pl.BlockSpec((1, t_len, HEAD_DIM), lambda b, h, i: (b, 0, h))
    vspec = pl.BlockSpec((1, t_len, HEAD_DIM), lambda b, h, i: (b, 0, N_KV_HEADS + h))
    cspec = pl.BlockSpec((1, 1, n_slots, HEAD_DIM), lambda b, h, i: (b, h, 0, 0))
    qspec = pl.BlockSpec((tq, GROUP * HEAD_DIM), lambda b, h, i: (b * ntq + i, h))
    kern = functools.partial(
        _nsa_prompt_kernel, tq=tq, t_len=t_len, nb=nb, n_sel=n_sel, n_slots=n_slots, kt=kt)
    return pl.pallas_call(
        kern,
        grid=(n_b, N_KV_HEADS, ntq),
        in_specs=[
            qspec,
            qspec,
            pl.BlockSpec((tq, LANES), lambda b, h, i: (b * ntq + i, h)),
            cspec,
            cspec,
            kspec,
            vspec,
            kspec,
            vspec,
            pl.BlockSpec((t_len, LANES), lambda b, h, i: (0, 0)),
        ],
        out_specs=qspec,
        out_shape=jax.ShapeDtypeStruct((m, Q_W), BF16),
        scratch_shapes=[pltpu.VMEM((rows, HEAD_DIM), F32),
                        pltpu.VMEM((rows, t_len), F32), pltpu.VMEM((rows, t_len), BF16),
                        pltpu.VMEM((rows, -(-t_len // SCORE_CHUNK) * LANES), F32),
                        pltpu.VMEM((rows, HEAD_DIM), F32),
                        pltpu.VMEM((rows, lw), F32), pltpu.VMEM((rows, lw), BF16),
                        pltpu.VMEM((rows, -(-lw // SCORE_CHUNK) * LANES), F32)],
        compiler_params=_params("parallel", "parallel", "arbitrary"),
        name="nsa_prompt",
    )(q_bf, qr_bf, gates, kc, vc, sel3, sel3, win3, win3, jnp.asarray(onehot, BF16))


def _layer_norm(y, g, b):
    yc = y - jnp.mean(y, axis=-1, keepdims=True)
    var = jnp.mean(yc * yc, axis=-1, keepdims=True)
    return yc * lax.rsqrt(var + EPS) * g + b


def _glu(z):
    c = z.shape[1] // 2
    return z[:, :c] * _sigmoid(z[:, c:])


def _conv_prompt_kernel(z_ref, halo_ref, wdw_ref, bdw_ref, gln_ref, bln_ref, wout_ref, o_ref, wattn_ref,
                        gm_a_ref, gm_c_ref, mixed_ref, tail_ref, ufull, yconv, yattn, *, tc, width):
    i = pl.program_id(1)
    u = _glu(z_ref[...])
    uh = jnp.where(i == 0, 0.0, _glu(halo_ref[...]))
    ufull[0:CONV_HALO, :] = uh
    ufull[CONV_HALO:, :] = u
    tail_ref[0] = u[tc - CONV_HALO :, :]
    ch = u.shape[1]
    rc, cc = 64, 256
    off = CONV_HALO - (width - 1)
    d_out = wattn_ref.shape[1]
    n_conv_chunks = (tc // rc) * (ch // cc)
    aw = d_out // n_conv_chunks * 2
    step = 0
    for r0 in range(0, tc, rc):
        for c0 in range(0, ch, cc):
            if step % 2 == 0:
                a0 = step // 2 * aw
                yattn[:, a0 : a0 + aw] = jnp.dot(o_ref[...], wattn_ref[:, a0 : a0 + aw], preferred_element_type=F32)
            step += 1
            acc = jnp.zeros((rc, cc), F32)
            span_all = rc + CONV_HALO
            u_rows = ufull[r0 : r0 + span_all, c0 : c0 + cc]
            for b in range(min(SUBLANES, width)):
                n_a = (width - 1 - b) // SUBLANES + 1
                ub = pltpu.roll(u_rows, (span_all - off - b) % span_all, 0)
                for a in range(n_a):
                    k = SUBLANES * a + b
                    acc = acc + wdw_ref[k : k + 1, c0 : c0 + cc] * ub[SUBLANES * a : SUBLANES * a + rc, :]
            yconv[r0 : r0 + rc, c0 : c0 + cc] = acc + bdw_ref[:, c0 : c0 + cc]
    y = _layer_norm(yconv[...], gln_ref[...], bln_ref[...])
    y = y * _sigmoid(y)
    y_conv = jnp.dot(y.astype(BF16), wout_ref[...], preferred_element_type=F32)
    mixed_ref[...] = (_sigmoid(gm_a_ref[...]) * yattn[...] + _sigmoid(gm_c_ref[...]) * y_conv).astype(mixed_ref.dtype)


def _conv_mix_prompt(zmain, o, w_dw, b_dw, g_ln, b_ln, w_out, w_attn, n_b, t_len):
    tc = 256
    width, ch = w_dw.shape
    d_out = w_out.shape[1]
    ntc = t_len // tc
    hb = tc // CONV_HALO
    m = n_b * t_len
    glu_blk = 0
    gm_blk = (zmain.shape[1] - 2 * d_out) // d_out
    row1 = lambda b, i: (0, 0)
    tile = lambda blk: pl.BlockSpec((tc, d_out), lambda b, i: (b * ntc + i, blk))
    kern = functools.partial(_conv_prompt_kernel, tc=tc, width=width)
    return pl.pallas_call(
        kern,
        grid=(n_b, ntc),
        in_specs=[
            pl.BlockSpec((tc, 2 * ch), lambda b, i: (b * ntc + i, glu_blk)),
            pl.BlockSpec((CONV_HALO, 2 * ch), lambda b, i: (jnp.maximum((b * ntc + i) * hb - 1, 0), glu_blk)),
            pl.BlockSpec((width, ch), row1),
            pl.BlockSpec((1, ch), row1),
            pl.BlockSpec((1, ch), row1),
            pl.BlockSpec((1, ch), row1),
            pl.BlockSpec((ch, d_out), row1),
            pl.BlockSpec((tc, o.shape[1]), lambda b, i: (b * ntc + i, 0)),
            pl.BlockSpec(w_attn.shape, row1),
            tile(gm_blk),
            tile(gm_blk + 1),
        ],
        out_specs=[
            tile(0),
            pl.BlockSpec((1, CONV_HALO, ch), lambda b, i: (b, 0, 0)),
        ],
        out_shape=[jax.ShapeDtypeStruct((m, d_out), BF16), jax.ShapeDtypeStruct((n_b, CONV_HALO, ch), F32)],
        scratch_shapes=[pltpu.VMEM((tc + CONV_HALO, ch), F32), pltpu.VMEM((tc, ch), F32), pltpu.VMEM((tc, d_out), F32)],
        compiler_params=_params("parallel", "arbitrary"),
        name="conv_mix_prompt",
    )(zmain, zmain, w_dw, b_dw.reshape(1, ch), g_ln.reshape(1, ch), b_ln.reshape(1, ch), w_out, o, w_attn,
      zmain, zmain)


def _conv_sample_kernel(z_ref, st_ref, wdw_ref, bdw_ref, gln_ref, bln_ref, wout_ref, y_ref, u_ref, *, width):
    u = _glu(z_ref[...])
    u_ref[...] = u
    y = jnp.sum(st_ref[...] * wdw_ref[0 : width - 1, :][None], axis=1) + wdw_ref[width - 1 : width, :] * u
    y = _layer_norm(y + bdw_ref[...], gln_ref[...], bln_ref[...])
    y = y * _sigmoid(y)
    y_ref[...] = jnp.dot(y.astype(BF16), wout_ref[...], preferred_element_type=F32)


def _conv_sample(zmain, state, w_dw, b_dw, g_ln, b_ln, w_out):
    width, ch = w_dw.shape
    n_b = state.shape[0]
    d_out = w_out.shape[1]
    glu_blk = 0
    c2 = lambda i: (0, 0)
    return pl.pallas_call(
        functools.partial(_conv_sample_kernel, width=width),
        grid=(1,),
        in_specs=[
            pl.BlockSpec((n_b, 2 * ch), lambda i: (0, glu_blk)),
            pl.BlockSpec((n_b, width - 1, ch), lambda i: (0, 0, 0)),
            pl.BlockSpec((width, ch), c2),
            pl.BlockSpec((1, ch), c2),
            pl.BlockSpec((1, ch), c2),
            pl.BlockSpec((1, ch), c2),
            pl.BlockSpec((ch, d_out), c2),
        ],
        out_specs=[pl.BlockSpec((n_b, d_out), c2), pl.BlockSpec((n_b, ch), c2)],
        out_shape=[jax.ShapeDtypeStruct((n_b, d_out), F32), jax.ShapeDtypeStruct((n_b, ch), F32)],
        compiler_params=_params("arbitrary"),
        name="conv_sample",
    )(zmain, state, w_dw, b_dw.reshape(1, ch), g_ln.reshape(1, ch), b_ln.reshape(1, ch), w_out)


def _attn_mix_kernel(o_ref, w_ref, gm_a_ref, gm_c_ref, yc_ref, mixed_ref):
    ya = jnp.dot(o_ref[...], w_ref[...], preferred_element_type=F32)
    mixed_ref[...] = (_sigmoid(gm_a_ref[...]) * ya + _sigmoid(gm_c_ref[...]) * yc_ref[...]).astype(mixed_ref.dtype)


def _attn_mix(o, w_attn, zmain, y_conv, tm):
    m, k = o.shape
    d = w_attn.shape[1]
    gm_blk = (zmain.shape[1] - 2 * d) // d
    return pl.pallas_call(
        _attn_mix_kernel,
        grid=(m // tm,),
        in_specs=[
            pl.BlockSpec((tm, k), lambda i: (i, 0)),
            pl.BlockSpec((k, d), lambda i: (0, 0)),
            pl.BlockSpec((tm, d), lambda i: (i, gm_blk)),
            pl.BlockSpec((tm, d), lambda i: (i, gm_blk + 1)),
            pl.BlockSpec((tm, d), lambda i: (i, 0)),
        ],
        out_specs=pl.BlockSpec((tm, d), lambda i: (i, 0)),
        out_shape=jax.ShapeDtypeStruct((m, d), BF16),
        compiler_params=_params("parallel"),
        name="attn_out_mix",
    )(o, w_attn, zmain, zmain, y_conv)


def _out_proj_kernel(mixed_ref, w_ref, x_ref, h_ref):
    h_ref[...] = x_ref[...] + jnp.dot(mixed_ref[...], w_ref[...], preferred_element_type=F32)


def _out_proj(mixed, w_out, x, tm):
    m, k = mixed.shape
    d = w_out.shape[1]
    return pl.pallas_call(
        _out_proj_kernel,
        grid=(m // tm,),
        in_specs=[
            pl.BlockSpec((tm, k), lambda i: (i, 0)),
            pl.BlockSpec((k, d), lambda i: (0, 0)),
            pl.BlockSpec((tm, d), lambda i: (i, 0)),
        ],
        out_specs=pl.BlockSpec((tm, d), lambda i: (i, 0)),
        out_shape=jax.ShapeDtypeStruct((m, d), F32),
        compiler_params=_params("parallel"),
        name="out_proj",
    )(mixed, w_out, x)


def _gelu(x):
    return 0.5 * x * (1.0 + lax.erf(x * np.float32(np.sqrt(0.5))))


def _ffn_kernel(*refs, tm, tiles_per_seq, sample):
    if sample:
        h_ref, g_ref, wa_ref, wb_ref, wdw_ref, bdw_ref, wd_ref, s0_ref, s1_ref, out_ref, a_ref, hn, acc = refs
    else:
        h_ref, halo_ref, g_ref, wa_ref, wb_ref, wdw_ref, bdw_ref, wd_ref, out_ref, a_ref, hn, acc = refs
    i = pl.program_id(0)
    j = pl.program_id(1)
    lead = 0 if sample else FFN_HALO

    @pl.when(j == 0)
    def _():
        hn[lead:, :] = (_rms(h_ref[...]) * g_ref[...]).astype(BF16)
        if not sample:
            hn[0:lead, :] = (_rms(halo_ref[...]) * g_ref[...]).astype(BF16)
        acc[...] = jnp.zeros_like(acc)

    a_ext = jnp.dot(hn[...], wa_ref[...], preferred_element_type=F32)
    gate = jnp.dot(hn[lead:, :], wb_ref[...], preferred_element_type=F32)
    a0 = a_ext[lead:, :]
    if sample:
        a_m2 = s0_ref[...]
        a_m1 = s1_ref[...]
        a_ref[...] = a0
    else:
        first = (i % tiles_per_seq) == 0
        r = lax.broadcasted_iota(jnp.int32, a0.shape, 0)
        a_m2 = jnp.where(first & (r < 2), 0.0, a_ext[lead - 2 : lead - 2 + tm, :])
        a_m1 = jnp.where(first & (r < 1), 0.0, a_ext[lead - 1 : lead - 1 + tm, :])
        a_ref[0] = a0[tm - SUBLANES :, :]
    conv = wdw_ref[0:1, :] * a_m2 + wdw_ref[1:2, :] * a_m1 + wdw_ref[2:3, :] * a0 + bdw_ref[...]
    y = (_gelu(conv) * gate).astype(BF16)
    acc[...] += jnp.dot(y, wd_ref[...], preferred_element_type=F32)

    @pl.when(j == pl.num_programs(1) - 1)
    def _():
        out_ref[...] = h_ref[...] + acc[...]


def _conv_ffn(h, g, w_up_a, w_up_b, w_dw, b_dw, w_down, *, tm, t_len=None, state=None):
    m, d = h.shape
    dff = w_up_a.shape[1]
    tf = 512
    nf = dff // tf
    sample = state is not None
    assert w_dw.shape[0] == 3
    col = lambda i, j: (0, j)
    common = [
        pl.BlockSpec((1, d), lambda i, j: (0, 0)),
        pl.BlockSpec((d, tf), col),
        pl.BlockSpec((d, tf), col),
        pl.BlockSpec((3, tf), col),
        pl.BlockSpec((1, tf), col),
        pl.BlockSpec((tf, d), lambda i, j: (j, 0)),
    ]
    args = [g.reshape(1, d), w_up_a, w_up_b, w_dw, b_dw.reshape(1, dff), w_down]
    hspec = pl.BlockSpec((tm, d), lambda i, j: (i, 0))
    if sample:
        assert m == tm
        st = state.reshape(m, 2 * dff)
        in_specs = [hspec] + common + [pl.BlockSpec((tm, tf), col), pl.BlockSpec((tm, tf), lambda i, j: (0, nf + j))]
        args = [h] + args + [st, st]
        a_spec = pl.BlockSpec((tm, tf), col)
        a_shape = jax.ShapeDtypeStruct((m, dff), F32)
        tiles_per_seq = 1
        lead = 0
    else:
        tiles_per_seq = t_len // tm
        hb = tm // FFN_HALO
        in_specs = [hspec, pl.BlockSpec((FFN_HALO, d), lambda i, j: (jnp.maximum(i * hb - 1, 0), 0))] + common
        args = [h, h] + args
        a_spec = pl.BlockSpec((1, SUBLANES, tf), lambda i, j: (i, 0, j))
        a_shape = jax.ShapeDtypeStruct((m // tm, SUBLANES, dff), F32)
        lead = FFN_HALO
    kern = functools.partial(_ffn_kernel, tm=tm, tiles_per_seq=tiles_per_seq, sample=sample)
    return pl.pallas_call(
        kern,
        grid=(m // tm, nf),
        in_specs=in_specs,
        out_specs=[hspec, a_spec],
        out_shape=[jax.ShapeDtypeStruct((m, d), F32), a_shape],
        scratch_shapes=[pltpu.VMEM((tm + lead, d), BF16), pltpu.VMEM((tm, d), F32)],
        compiler_params=_params("parallel", "arbitrary"),
        name="conv_ffn",
    )(*args)


def _ple_kernel(h_ref, g_ref, wg_ref, pe_ref, wp_ref, o_ref):
    h = h_ref[...]
    hn = (_rms(h) * g_ref[...]).astype(BF16)
    gate = _sigmoid(jnp.dot(hn, wg_ref[...], preferred_element_type=F32))
    proj = jnp.dot(pe_ref[...], wp_ref[...], preferred_element_type=F32)
    o_ref[...] = h + gate * proj


def _ple(h, g, w_gate, pe, w_proj, tm):
    m, d = h.shape
    pd = pe.shape[1]
    return pl.pallas_call(
        _ple_kernel,
        grid=(m // tm,),
        in_specs=[
            pl.BlockSpec((tm, d), lambda i: (i, 0)),
            pl.BlockSpec((1, d), lambda i: (0, 0)),
            pl.BlockSpec((d, d), lambda i: (0, 0)),
            pl.BlockSpec((tm, pd), lambda i: (i, 0)),
            pl.BlockSpec((pd, d), lambda i: (0, 0)),
        ],
        out_specs=pl.BlockSpec((tm, d), lambda i: (i, 0)),
        out_shape=jax.ShapeDtypeStruct((m, d), F32),
        compiler_params=_params("parallel"),
        name="ple",
    )(h, g.reshape(1, d), w_gate, pe, w_proj)


def _cmp_sample_kernel(q_ref, kc_ref, vc_ref, ov_ref, score_ref, ocmp_ref, *, pos, nb, n_slots):
    scale = HEAD_DIM ** -0.5
    qn = q_ref[0]
    slot = lax.broadcasted_iota(jnp.int32, (GROUP, n_slots), 1)
    valid = (slot >= 1) & (slot * CMP_STRIDE + CMP_STRIDE <= pos + 1)
    sums = []
    for h in range(N_KV_HEADS):
        qh = qn[h * GROUP : (h + 1) * GROUP, :].astype(BF16)
        s = lax.dot_general(qh, kc_ref[0, h], NT_DIMS, preferred_element_type=F32) * scale
        e, l = _softmax_rows(jnp.where(valid, s, NEG_INF))
        p = jnp.where(valid, e / l, 0.0)
        ocmp_ref[0, h * GROUP : (h + 1) * GROUP, :] = jnp.dot(p.astype(BF16), vc_ref[0, h], preferred_element_type=F32)
        sums.append(jnp.sum(p, axis=0, keepdims=True))
    p_sum = jnp.concatenate(sums + [jnp.zeros((SUBLANES - N_KV_HEADS, n_slots), F32)], axis=0)
    imp = jnp.dot(p_sum, ov_ref[...], precision=lax.Precision.HIGHEST, preferred_element_type=F32)
    jb = lax.broadcasted_iota(jnp.int32, imp.shape, 1)
    cur = pos // SEL_BLOCK
    causal = (jb * SEL_BLOCK <= pos) & (jb < nb)
    forced = (jb == 0) | (jb == cur) | (jb == cur - 1)
    score_ref[0] = jnp.where(jb < nb, jnp.where(causal, jnp.where(forced, SEL_FORCE, imp), -1.0), -2.0)


def _cmp_sample(q3, kc, vc, pos, nb):
    n_b = q3.shape[0]
    n_slots = kc.shape[2]
    nbp = -(-nb // LANES) * LANES
    ratio = SEL_BLOCK // CMP_STRIDE
    mm = np.arange(n_slots)[:, None]
    jj = np.arange(nbp)[None, :]
    overlap = ((mm >= ratio * jj) & (mm <= ratio * jj + ratio) & (jj < nb)).astype(np.float32)
    qspec = pl.BlockSpec((1, N_HEADS, HEAD_DIM), lambda b: (b, 0, 0))
    cspec = pl.BlockSpec((1, N_KV_HEADS, n_slots, HEAD_DIM), lambda b: (b, 0, 0, 0))
    return pl.pallas_call(
        functools.partial(_cmp_sample_kernel, pos=pos, nb=nb, n_slots=n_slots),
        grid=(n_b,),
        in_specs=[qspec, cspec, cspec, pl.BlockSpec((n_slots, nbp), lambda b: (0, 0))],
        out_specs=[pl.BlockSpec((1, SUBLANES, nbp), lambda b: (b, 0, 0)), qspec],
        out_shape=[
            jax.ShapeDtypeStruct((n_b, SUBLANES, nbp), F32),
            jax.ShapeDtypeStruct((n_b, N_HEADS, HEAD_DIM), F32),
        ],
        compiler_params=_params("parallel"),
        name="cmp_sample",
    )(q3, kc, vc, jnp.asarray(overlap))


def _topk_kernel(score_ref, idx_ref, *, n_sel):
    sc = score_ref[...]
    lane = lax.broadcasted_iota(jnp.int32, sc.shape, 1)
    out_lane = lax.broadcasted_iota(jnp.int32, idx_ref.shape, 1)
    out = jnp.zeros(idx_ref.shape, jnp.int32)
    for it in range(n_sel):
        mx = jnp.max(sc, axis=-1, keepdims=True)
        idx = jnp.min(jnp.where(sc == mx, lane, sc.shape[1]), axis=-1, keepdims=True)
        out = jnp.where(out_lane == it, idx, out)
        sc = jnp.where(lane == idx, -3.0, sc)
    idx_ref[...] = out


def _topk(score, n_sel):
    rows = score.shape[0]
    return pl.pallas_call(
        functools.partial(_topk_kernel, n_sel=n_sel),
        out_shape=jax.ShapeDtypeStruct((rows, LANES), jnp.int32),
        compiler_params=pltpu.CompilerParams(vmem_limit_bytes=VMEM_LIMIT),
        name="topk_sample",
    )(score)


def _sel_sample_kernel(idx_ref, pt_ref, *refs, n_sel, n_past_blocks):
    del pt_ref
    blk_refs = refs[:n_sel]
    qr_ref, new_ref, o_ref = refs[n_sel:]
    b = pl.program_id(0)
    h = pl.program_id(1)
    scale = HEAD_DIM ** -0.5
    blk_rows = SEL_BLOCK * HEADS_PER_ROW
    qr = qr_ref[0, 0].astype(BF16)
    rows = jnp.concatenate([r[...] for r in blk_refs], axis=0).astype(BF16)
    s = lax.dot_general(qr, rows, NT_DIMS, preferred_element_type=F32) * scale
    lane = lax.broadcasted_iota(jnp.int32, s.shape, 1)
    slot = lane // blk_rows
    bias = jnp.where((lane & (HEADS_PER_ROW - 1)) == h, 0.0, NEG_INF)
    n_new = jnp.int32(0)
    for j in range(n_sel):
        is_new = idx_ref[(b * N_KV_HEADS + h) * n_sel + j] >= n_past_blocks
        bias = jnp.where(slot == j, jnp.where(is_new, NEG_INF, bias), bias)
        n_new = n_new + jnp.where(is_new, 1, 0)
    s = s + bias
    k_new = new_ref[0, pl.ds(h, 1), :].astype(BF16).astype(F32)
    v_new = new_ref[0, pl.ds(N_KV_HEADS + h, 1), :].astype(BF16).astype(F32)
    s_new = jnp.sum(qr.astype(F32) * k_new, axis=-1, keepdims=True) * scale
    s_new = jnp.where(n_new > 0, s_new, NEG_INF)
    m = jnp.maximum(jnp.max(s, axis=-1, keepdims=True), s_new)
    e = jnp.exp(s - m)
    e_new = jnp.exp(s_new - m)
    l = jnp.sum(e, axis=-1, keepdims=True) + e_new
    e_v = pltpu.roll(e, N_KV_HEADS, 1)
    o = jnp.dot(e_v.astype(BF16), rows, preferred_element_type=F32) + e_new.astype(BF16).astype(F32) * v_new
    o_ref[0, 0] = o / l


def _sel_sample(idx_flat, pt_flat, cache_rows, qr4, new_rows, n_sel, n_past_blocks, n_pp):
    n_b = qr4.shape[0]
    per_page = PAGE_SIZE // SEL_BLOCK
    blk_rows = SEL_BLOCK * HEADS_PER_ROW

    def bspec(j):
        def imap(b, h, idx, pt):
            blk_id = jnp.minimum(idx[(b * N_KV_HEADS + h) * n_sel + j], n_past_blocks - 1)
            return (pt[b * n_pp + blk_id // per_page] * per_page + blk_id % per_page, 0)
        return pl.BlockSpec((blk_rows, HEAD_DIM), imap)

    grid_spec = pltpu.PrefetchScalarGridSpec(
        num_scalar_prefetch=2,
        grid=(n_b, N_KV_HEADS),
        in_specs=[bspec(j) for j in range(n_sel)]
        + [
            pl.BlockSpec((1, 1, GROUP, HEAD_DIM), lambda b, h, idx, pt: (b, h, 0, 0)),
            pl.BlockSpec((1, HEADS_PER_ROW, HEAD_DIM), lambda b, h, idx, pt: (b, 0, 0)),
        ],
        out_specs=pl.BlockSpec((1, 1, GROUP, HEAD_DIM), lambda b, h, idx, pt: (b, h, 0, 0)),
    )
    return pl.pallas_call(
        functools.partial(_sel_sample_kernel, n_sel=n_sel, n_past_blocks=n_past_blocks),
        grid_spec=grid_spec,
        out_shape=jax.ShapeDtypeStruct((n_b, N_KV_HEADS, GROUP, HEAD_DIM), F32),
        compiler_params=_params("parallel", "arbitrary"),
        name="sel_sample",
    )(idx_flat, pt_flat, *([cache_rows] * n_sel), qr4, new_rows)


def _head_lane_mask(shape, kv_head_of_row):
    lane = lax.broadcasted_iota(jnp.int32, shape, 1)
    return (lane & (HEADS_PER_ROW - 1)) == kv_head_of_row


def _win_sample_kernel(win_ref, new_ref, qr_ref, ocmp_ref, osel_ref, g_ref, nw_ref, o_ref, *, wb):
    scale = HEAD_DIM ** -0.5
    keep = (wb - 1) * HEADS_PER_ROW
    nw_ref[0, 0:keep, :] = win_ref[0, HEADS_PER_ROW:, :]
    nw_ref[0, keep:, :] = new_ref[0]
    rows = nw_ref[0].astype(BF16)
    qr = qr_ref[0].astype(BF16)
    s = lax.dot_general(qr, rows, NT_DIMS, preferred_element_type=F32) * scale
    kv_head = lax.broadcasted_iota(jnp.int32, s.shape, 0) // GROUP
    s = jnp.where(_head_lane_mask(s.shape, kv_head), s, NEG_INF)
    e, l = _softmax_rows(s)
    e_v = pltpu.roll(e, N_KV_HEADS, 1)
    o_win = jnp.dot(e_v.astype(BF16), rows, preferred_element_type=F32) / l
    g = g_ref[0]
    o = g[:, 0:1] * ocmp_ref[0] + g[:, 1:2] * osel_ref[0] + g[:, 2:3] * o_win
    o_ref[0] = o.astype(o_ref.dtype)


def _win_sample(win_rows, new_rows, qr3, o_cmp, o_sel, g3):
    n_b, wr, _ = win_rows.shape
    wb = wr // HEADS_PER_ROW
    hspec = pl.BlockSpec((1, N_HEADS, HEAD_DIM), lambda b: (b, 0, 0))
    wspec = pl.BlockSpec((1, wr, HEAD_DIM), lambda b: (b, 0, 0))
    return pl.pallas_call(
        functools.partial(_win_sample_kernel, wb=wb),
        grid=(n_b,),
        in_specs=[
            wspec,
            pl.BlockSpec((1, HEADS_PER_ROW, HEAD_DIM), lambda b: (b, 0, 0)),
            hspec,
            hspec,
            hspec,
            pl.BlockSpec((1, N_HEADS, 3), lambda b: (b, 0, 0)),
        ],
        out_specs=[wspec, hspec],
        out_shape=[jax.ShapeDtypeStruct((n_b, wr, HEAD_DIM), F32), jax.ShapeDtypeStruct((n_b, N_HEADS, HEAD_DIM), BF16)],
        compiler_params=_params("parallel"),
        name="win_sample",
    )(win_rows, new_rows, qr3, o_cmp, o_sel, g3)


def _rope_tables(pos):
    half = HEAD_DIM // 2
    inv = ROPE_THETA ** (-jnp.arange(half, dtype=F32) / half)
    ang = pos.astype(F32)[:, None] * inv
    cos = jnp.cos(ang)
    sin = jnp.sin(ang)
    return jnp.concatenate([cos, cos], axis=-1), jnp.concatenate([-sin, sin], axis=-1)


def _prep_weights(lw):
    w_in = lw["w_in"]
    c = 0
    parts = {}
    conv_ch = lw["w_conv_dw"].shape[1]
    d_model = w_in.shape[0]
    for name, width in (("q", Q_W), ("cmp", ROW_W), ("sel", ROW_W), ("win", ROW_W), ("gnsa", 3 * N_HEADS),
                        ("glu", 2 * conv_ch), ("gm", 2 * d_model)):
        parts[name] = w_in[:, c : c + width]
        c += width
    wg = parts["gnsa"].reshape(d_model, 3, N_KV_HEADS, GROUP).transpose(0, 2, 1, 3).reshape(d_model, N_KV_HEADS, 3 * GROUP)
    wg = jnp.pad(wg, ((0, 0), (0, 0), (0, LANES - 3 * GROUP))).reshape(d_model, N_KV_HEADS * LANES)
    w1 = lw["w_cmp1"]
    hid = w1.shape[-1]
    r = CMP_BLOCK // CMP_STRIDE
    w1all = w1.reshape(2, r, CMP_STRIDE, HEAD_DIM, hid).transpose(2, 3, 0, 1, 4).reshape(CMP_STRIDE * HEAD_DIM, 2 * r * hid)
    dff = lw["w_up"].shape[1] // 2
    return {
        "w_q": parts["q"].astype(BF16),
        "w_main": jnp.concatenate([parts["glu"], parts["gm"]], axis=1).astype(BF16),
        "w_cmp_gate": jnp.concatenate([parts["cmp"], wg], axis=1).astype(BF16),
        "w_sel": parts["sel"].astype(BF16),
        "w_win": parts["win"].astype(BF16),
        "w1all": w1all.astype(BF16),
        "w2all": jnp.concatenate([lw["w_cmp2"][0], lw["w_cmp2"][1]], axis=1).astype(BF16),
        "w_attn_out": lw["w_attn_out"].astype(BF16),
        "w_conv_out": lw["w_conv_out"].astype(BF16),
        "w_out": lw["w_out"].astype(BF16),
        "w_up_a": lw["w_up"][:, :dff].astype(BF16),
        "w_up_b": lw["w_up"][:, dff:].astype(BF16),
        "w_down": lw["w_down"].astype(BF16),
        "w_ple_gate": lw["w_ple_gate"].astype(BF16),
        "w_ple_proj": lw["w_ple_proj"].astype(BF16),
    }


def _dense_tail(x, mixed, pe, lw, pw, *, tm, t_len=None, ffn_state=None):
    h1 = _out_proj(mixed, pw["w_out"], x, tm)
    h2, a_tail = _conv_ffn(h1, lw["g_ffn"], pw["w_up_a"], pw["w_up_b"], lw["w_ffn_dw"], lw["b_ffn_dw"], pw["w_down"],
                           tm=tm, t_len=t_len, state=ffn_state)
    h3 = _ple(h2, lw["g_ple"], pw["w_ple_gate"], pe.astype(BF16), pw["w_ple_proj"], tm)
    return h3, a_tail


def _layer_prompt(x3, pe3, lw, pw, bias_cmp):
    n_b, t_len, d = x3.shape
    m = n_b * t_len
    x = x3.reshape(m, d)
    cos, sin = _rope_tables(jnp.arange(t_len, dtype=jnp.int32))
    xn = _rmsnorm_bf16(x, lw["g_mix"], 512)
    zmain = _matmul(xn, pw["w_main"], 1024, 1024)
    rows_cmp, gates = _cmp_gate_proj(xn, pw["w_cmp_gate"], 512)
    rows_sel, sel_bf = _kv_proj(xn, pw["w_sel"], lw["g_k"][1], cos, sin, 512)
    rows_win, win_bf = _kv_proj(xn, pw["w_win"], lw["g_k"][2], cos, sin, 512)
    n_pp = t_len // PAGE_SIZE
    ident = jnp.arange(n_b * n_pp, dtype=jnp.int32).reshape(n_b, n_pp)
    kc, vc = _compress(rows_cmp.reshape(n_b * n_pp, PAGE_SIZE, HEADS_PER_ROW, HEAD_DIM), ident, pw["w1all"], bias_cmp,
                       pw["w2all"], lw["b_cmp2"], lw["g_k"][0])
    q_bf, qr_bf = _q_proj(xn, pw["w_q"], lw["g_q"], cos, sin, 512, BF16)
    o = _nsa_prompt(q_bf, qr_bf, gates, kc, vc, sel_bf, win_bf, n_b, t_len)
    mixed, u_tail = _conv_mix_prompt(zmain, o, lw["w_conv_dw"], lw["b_conv_dw"], lw["g_conv_ln"], lw["b_conv_ln"],
                                     pw["w_conv_out"], pw["w_attn_out"], n_b, t_len)
    h3, a_tail = _dense_tail(x, mixed, pe3.reshape(m, -1), lw, pw, tm=512, t_len=t_len)
    kvshape = (n_b, t_len, 2, N_KV_HEADS, HEAD_DIM)
    w_keep = min(WINDOW, t_len)
    conv_w = lw["w_conv_dw"].shape[0]
    state = (
        rows_cmp.reshape(kvshape),
        rows_sel.reshape(kvshape),
        rows_win.reshape(n_b, t_len * HEADS_PER_ROW, HEAD_DIM)[:, (t_len - w_keep) * HEADS_PER_ROW :].reshape(
            n_b, w_keep, 2, N_KV_HEADS, HEAD_DIM),
        u_tail[:, CONV_HALO - (conv_w - 1) :],
        a_tail.reshape(n_b, -1, SUBLANES, a_tail.shape[-1])[:, -1, SUBLANES - (lw["w_ffn_dw"].shape[0] - 1) :],
    )
    return h3.reshape(n_b, t_len, d), state


def _layer_sample(x3, pe3, lw, pw, bias_cmp, cache_cmp, cache_sel, cache_win, conv_state, ffn_state, page_table):
    n_b, t_new, d = x3.shape
    assert t_new == 1
    n_pp = page_table.shape[1]
    past_len = n_pp * PAGE_SIZE
    pos = past_len
    wb = cache_win.shape[1]
    assert wb == WINDOW and past_len % SEL_BLOCK == 0
    x = x3.reshape(n_b, d)
    cos, sin = _rope_tables(jnp.full((n_b,), pos, dtype=jnp.int32))
    xn = _rmsnorm_bf16(x, lw["g_mix"], n_b)
    zmain = _matmul(xn, pw["w_main"], n_b, 1024)
    rows_cmp, gates = _cmp_gate_proj(xn, pw["w_cmp_gate"], n_b)
    rows_sel, _ = _kv_proj(xn, pw["w_sel"], lw["g_k"][1], cos, sin, n_b)
    rows_win, _ = _kv_proj(xn, pw["w_win"], lw["g_k"][2], cos, sin, n_b)
    n_pool = cache_cmp.shape[0]
    kc, vc = _compress(cache_cmp.reshape(n_pool, PAGE_SIZE, HEADS_PER_ROW, HEAD_DIM), page_table, pw["w1all"], bias_cmp,
                       pw["w2all"], lw["b_cmp2"], lw["g_k"][0])
    n_past_blocks = past_len // SEL_BLOCK
    nb = n_past_blocks + 1
    n_sel = min(N_SEL, nb)
    q_s, qr_s = _q_proj(xn, pw["w_q"], lw["g_q"], cos, sin, n_b, F32)
    q3 = q_s.reshape(n_b, N_HEADS, HEAD_DIM)
    qr3 = qr_s.reshape(n_b, N_HEADS, HEAD_DIM)
    score, o_cmp = _cmp_sample(q3, kc, vc, pos, nb)
    idx = _topk(score.reshape(n_b * SUBLANES, -1), n_sel)
    idx = idx.reshape(n_b, SUBLANES, LANES)[:, :N_KV_HEADS, :n_sel]
    o_sel = _sel_sample(idx.reshape(-1), page_table.reshape(-1),
                        cache_sel.reshape(n_pool * PAGE_SIZE * HEADS_PER_ROW, HEAD_DIM),
                        qr3.reshape(n_b, N_KV_HEADS, GROUP, HEAD_DIM), rows_sel.reshape(n_b, HEADS_PER_ROW, HEAD_DIM),
                        n_sel, n_past_blocks, n_pp)
    g3 = gates.reshape(n_b, N_KV_HEADS, LANES)[:, :, : 3 * GROUP].reshape(n_b, N_KV_HEADS, 3, GROUP)
    g3 = g3.transpose(0, 1, 3, 2).reshape(n_b, N_HEADS, 3)
    new_win, o = _win_sample(cache_win.reshape(n_b, wb * HEADS_PER_ROW, HEAD_DIM),
                             rows_win.reshape(n_b, HEADS_PER_ROW, HEAD_DIM), qr3, o_cmp,
                             o_sel.reshape(n_b, N_HEADS, HEAD_DIM), g3)
    y_conv, u = _conv_sample(zmain, conv_state, lw["w_conv_dw"], lw["b_conv_dw"], lw["g_conv_ln"], lw["b_conv_ln"],
                             pw["w_conv_out"])
    mixed = _attn_mix(o.reshape(n_b, Q_W), pw["w_attn_out"], zmain, y_conv, n_b)
    h3, a_new = _dense_tail(x, mixed, pe3.reshape(n_b, -1), lw, pw, tm=n_b, ffn_state=ffn_state)
    kvshape = (n_b, 1, 2, N_KV_HEADS, HEAD_DIM)
    state = (
        rows_cmp.reshape(kvshape),
        rows_sel.reshape(kvshape),
        new_win.reshape(n_b, wb, 2, N_KV_HEADS, HEAD_DIM),
        jnp.concatenate([conv_state[:, 1:], u[:, None, :]], axis=1),
        jnp.concatenate([ffn_state[:, 1:], a_new[:, None, :]], axis=1),
    )
    return h3.reshape(n_b, 1, d), state


_LAYER_WEIGHTS = ("g_mix", "w_in", "g_q", "g_k", "w_cmp1", "b_cmp1", "w_cmp2", "b_cmp2", "pe_cmp", "w_attn_out",
                  "w_conv_dw", "b_conv_dw", "g_conv_ln", "b_conv_ln", "w_conv_out", "w_out", "g_ffn", "w_up",
                  "w_ffn_dw", "b_ffn_dw", "w_down", "g_ple", "w_ple_gate", "w_ple_proj")


def kernel(x_prompt, x_sample, p_prompt, p_sample, cache_cmp_kv, cache_sel_kv, cache_win_kv, state_conv, state_ffn_conv, page_table, g_mix, w_in, g_q, g_k, w_cmp1, b_cmp1, w_cmp2, b_cmp2, pe_cmp, w_attn_out, w_conv_dw, b_conv_dw, g_conv_ln, b_conv_ln, w_conv_out, w_out, g_ffn, w_up, w_ffn_dw, b_ffn_dw, w_down, g_ple, w_ple_gate, w_ple_proj):
    stacked = dict(zip(_LAYER_WEIGHTS, (g_mix, w_in, g_q, g_k, w_cmp1, b_cmp1, w_cmp2, b_cmp2, pe_cmp, w_attn_out,
                                        w_conv_dw, b_conv_dw, g_conv_ln, b_conv_ln, w_conv_out, w_out, g_ffn, w_up,
                                        w_ffn_dw, b_ffn_dw, w_down, g_ple, w_ple_gate, w_ple_proj)))
    depth = w_in.shape[0]
    hp, hs = x_prompt, x_sample
    states_p, states_s = [], []
    for i in range(depth):
        lw = {k: v[i] for k, v in stacked.items()}
        pw = _prep_weights(lw)
        bias_cmp = _pe_bias(lw["pe_cmp"], lw["w_cmp1"], lw["b_cmp1"])
        hp, st_p = _layer_prompt(hp, p_prompt[i], lw, pw, bias_cmp)
        hs, st_s = _layer_sample(hs, p_sample[i], lw, pw, bias_cmp, cache_cmp_kv[i], cache_sel_kv[i], cache_win_kv[i],
                                 state_conv[i], state_ffn_conv[i], page_table)
        states_p.append(st_p)
        states_s.append(st_s)
    outs = [hp, hs]
    for k in range(5):
        outs.append(jnp.stack([s[k] for s in states_p]))
        outs.append(jnp.stack([s[k] for s in states_s]))
    return tuple(outs)
```

```python
import functools

import numpy as np
import jax
import jax.numpy as jnp
from jax import lax
from jax.experimental import pallas as pl
from jax.experimental.pallas import tpu as pltpu

N_HEADS = 16
HEAD_DIM = 128
N_KV_HEADS = 4
GROUP = N_HEADS // N_KV_HEADS
Q_W = N_HEADS * HEAD_DIM
KV_W = N_KV_HEADS * HEAD_DIM
ROW_W = 2 * KV_W
HEADS_PER_ROW = 2 * N_KV_HEADS
CMP_BLOCK = 32
CMP_STRIDE = 16
SEL_BLOCK = 64
SEL_SHIFT = 6
N_SEL = 16
WINDOW = 512
PAGE_SIZE = 128
ROPE_THETA = 10000.0
EPS = 1e-6
NEG_INF = -1e30
SEL_FORCE = 1e6
MASK_BIG = 2.0 ** 100
CONV_HALO = 32
FFN_HALO = 16
SCORE_CHUNK = 512
ROW_TILE = 128
MAX_PAGES_PER_STEP = 32

LANES = 128
SUBLANES = 8
VMEM_LIMIT = 56 * 1024 * 1024

F32 = jnp.float32
BF16 = jnp.bfloat16
NT_DIMS = (((1,), (1,)), ((), ()))


def _params(*sem):
    return pltpu.CompilerParams(dimension_semantics=sem, vmem_limit_bytes=VMEM_LIMIT)


def _rms(x):
    return x * lax.rsqrt(jnp.mean(x * x, axis=-1, keepdims=True) + EPS)


def _rope(x, cos, sin_signed):
    return x * cos + pltpu.roll(x, HEAD_DIM // 2, 1) * sin_signed


def _sigmoid(x):
    return jax.nn.sigmoid(x)


def _softmax_rows(s):
    m = jnp.max(s, axis=-1, keepdims=True)
    e = jnp.exp(s - m)
    return e, jnp.sum(e, axis=-1, keepdims=True)


def _softmax_rows_scaled(s, scale):
    m = jnp.max(s, axis=-1, keepdims=True)
    e = jnp.exp2((s - m) * np.float32(scale * np.log2(np.e)))
    return e, jnp.sum(e, axis=-1, keepdims=True)


def _rmsnorm_kernel(x_ref, g_ref, o_ref):
    o_ref[...] = (_rms(x_ref[...]) * g_ref[...]).astype(o_ref.dtype)


def _rmsnorm_bf16(x, g, tm):
    m, d = x.shape
    return pl.pallas_call(
        _rmsnorm_kernel,
        grid=(m // tm,),
        in_specs=[pl.BlockSpec((tm, d), lambda i: (i, 0)), pl.BlockSpec((1, d), lambda i: (0, 0))],
        out_specs=pl.BlockSpec((tm, d), lambda i: (i, 0)),
        out_shape=jax.ShapeDtypeStruct((m, d), BF16),
        compiler_params=_params("parallel"),
        name="rmsnorm_in",
    )(x, g.reshape(1, d))


def _matmul_kernel(x_ref, w_ref, o_ref):
    o_ref[...] = jnp.dot(x_ref[...], w_ref[...], preferred_element_type=F32)


def _matmul(x, w, tm, tn):
    m, k = x.shape
    n = w.shape[1]
    return pl.pallas_call(
        _matmul_kernel,
        grid=(m // tm, n // tn),
        in_specs=[pl.BlockSpec((tm, k), lambda i, j: (i, 0)), pl.BlockSpec((k, tn), lambda i, j: (0, j))],
        out_specs=pl.BlockSpec((tm, tn), lambda i, j: (i, j)),
        out_shape=jax.ShapeDtypeStruct((m, n), F32),
        compiler_params=_params("parallel", "arbitrary"),
        name="in_proj_main",
    )(x, w)


def _q_epilogue(acc, gq_ref, cos_ref, sin_ref, q_ref, qr_ref):
    cos = cos_ref[...]
    sin = sin_ref[...]
    gq = gq_ref[...]
    for h0 in range(0, acc.shape[1], HEAD_DIM):
        sl = slice(h0, h0 + HEAD_DIM)
        qn = _rms(acc[:, sl]) * gq
        q_ref[:, sl] = qn.astype(q_ref.dtype)
        qr_ref[:, sl] = _rope(qn, cos, sin).astype(qr_ref.dtype)


def _q_kernel(x_ref, w_ref, gq_ref, cos_ref, sin_ref, q_ref, qr_ref):
    acc = jnp.dot(x_ref[...], w_ref[...], preferred_element_type=F32)
    _q_epilogue(acc, gq_ref, cos_ref, sin_ref, q_ref, qr_ref)


def _q_kernel_skewed(x_ref, w_ref, gq_ref, cos_ref, sin_ref, q_ref, qr_ref, acc_scr):
    i = pl.program_id(0)

    @pl.when(i == 0)
    def _():
        acc_scr[1] = jnp.zeros(acc_scr.shape[1:], F32)

    for parity in (0, 1):

        @pl.when(i % 2 == parity)
        def _(parity=parity):
            _q_epilogue(acc_scr[1 - parity], gq_ref, cos_ref, sin_ref, q_ref, qr_ref)
            acc_scr[parity] = jnp.dot(x_ref[...], w_ref[...], preferred_element_type=F32)


def _q_proj(xn, w, gq, cos, sin, tm, out_dtype):
    m, k = xn.shape
    n = w.shape[1]
    nt = cos.shape[0] // tm
    steps = m // tm
    skew = steps > 1
    prev = (lambda i: jnp.maximum(i - 1, 0)) if skew else (lambda i: i)
    ospec = pl.BlockSpec((tm, n), lambda i: (prev(i), 0))
    tspec = pl.BlockSpec((tm, HEAD_DIM), lambda i: (prev(i) % nt, 0))
    return pl.pallas_call(
        _q_kernel_skewed if skew else _q_kernel,
        grid=(steps + 1 if skew else steps,),
        in_specs=[pl.BlockSpec((tm, k), lambda i: (jnp.minimum(i, steps - 1), 0)),
                  pl.BlockSpec((k, n), lambda i: (0, 0)),
                  pl.BlockSpec((1, HEAD_DIM), lambda i: (0, 0)), tspec, tspec],
        out_specs=[ospec, ospec],
        out_shape=[jax.ShapeDtypeStruct((m, n), out_dtype), jax.ShapeDtypeStruct((m, n), out_dtype)],
        scratch_shapes=[pltpu.VMEM((2, tm, n), F32)] if skew else [],
        compiler_params=_params("arbitrary"),
        name="in_proj_q",
    )(xn, w, gq.reshape(1, HEAD_DIM), cos, sin)


def _store_rows(rows_ref, acc):
    tm = acc.shape[0]
    for j in range(HEADS_PER_ROW):
        rows_ref[pl.ds(j, tm, stride=HEADS_PER_ROW), :] = acc[:, j * HEAD_DIM : (j + 1) * HEAD_DIM]


def _cmp_gate_kernel(x_ref, w_ref, rows_ref, gate_ref):
    acc = jnp.dot(x_ref[...], w_ref[...], preferred_element_type=F32)
    _store_rows(rows_ref, acc[:, :ROW_W])
    gate_ref[...] = _sigmoid(acc[:, ROW_W:])


def _cmp_gate_proj(xn, w, tm):
    m, k = xn.shape
    n = w.shape[1]
    ng = n - ROW_W
    return pl.pallas_call(
        _cmp_gate_kernel,
        grid=(m // tm,),
        in_specs=[pl.BlockSpec((tm, k), lambda i: (i, 0)), pl.BlockSpec((k, n), lambda i: (0, 0))],
        out_specs=[pl.BlockSpec((tm * HEADS_PER_ROW, HEAD_DIM), lambda i: (i, 0)),
                   pl.BlockSpec((tm, ng), lambda i: (i, 0))],
        out_shape=[jax.ShapeDtypeStruct((m * HEADS_PER_ROW, HEAD_DIM), F32), jax.ShapeDtypeStruct((m, ng), F32)],
        compiler_params=_params("parallel"),
        name="in_proj_cmp_gates",
    )(xn, w)


def _kv_epilogue(acc, gk_ref, cos_ref, sin_ref, rows_ref, rows_bf_ref):
    cos = cos_ref[...]
    sin = sin_ref[...]
    gk = gk_ref[...]
    tm = acc.shape[0]
    for h in range(N_KV_HEADS):
        sl = slice(h * HEAD_DIM, (h + 1) * HEAD_DIM)
        k = _rope(_rms(acc[:, sl]) * gk, cos, sin)
        rows_ref[pl.ds(h, tm, stride=HEADS_PER_ROW), :] = k
        rows_bf_ref[:, sl] = k.astype(BF16)
        sv = slice(KV_W + h * HEAD_DIM, KV_W + (h + 1) * HEAD_DIM)
        rows_ref[pl.ds(N_KV_HEADS + h, tm, stride=HEADS_PER_ROW), :] = acc[:, sv]
    rows_bf_ref[:, KV_W:] = acc[:, KV_W:].astype(BF16)


def _kv_kernel(x_ref, w_ref, gk_ref, cos_ref, sin_ref, rows_ref, rows_bf_ref):
    acc = jnp.dot(x_ref[...], w_ref[...], preferred_element_type=F32)
    _kv_epilogue(acc, gk_ref, cos_ref, sin_ref, rows_ref, rows_bf_ref)


def _kv_proj(xn, w, gk, cos, sin, tm):
    m, k = xn.shape
    nt = cos.shape[0] // tm
    return pl.pallas_call(
        _kv_kernel,
        grid=(m // tm,),
        in_specs=[
            pl.BlockSpec((tm, k), lambda i: (i, 0)),
            pl.BlockSpec((k, ROW_W), lambda i: (0, 0)),
            pl.BlockSpec((1, HEAD_DIM), lambda i: (0, 0)),
            pl.BlockSpec((tm, HEAD_DIM), lambda i: (i % nt, 0)),
            pl.BlockSpec((tm, HEAD_DIM), lambda i: (i % nt, 0)),
        ],
        out_specs=[pl.BlockSpec((tm * HEADS_PER_ROW, HEAD_DIM), lambda i: (i, 0)),
                   pl.BlockSpec((tm, ROW_W), lambda i: (i, 0))],
        out_shape=[jax.ShapeDtypeStruct((m * HEADS_PER_ROW, HEAD_DIM), F32), jax.ShapeDtypeStruct((m, ROW_W), BF16)],
        compiler_params=_params("parallel"),
        name="in_proj_kv",
    )(xn, w, gk.reshape(1, HEAD_DIM), cos, sin)


def _pe_bias_kernel(pe_ref, w1_ref, b1_ref, o_ref):
    for kv in range(2):
        acc = jnp.zeros((SUBLANES, HEAD_DIM), F32)
        for r in range(CMP_BLOCK):
            row = jnp.broadcast_to(pe_ref[kv, r : r + 1, :], (SUBLANES, HEAD_DIM)).astype(BF16)
            acc = acc + jnp.dot(
                row, w1_ref[kv, r * HEAD_DIM : (r + 1) * HEAD_DIM, :].astype(BF16), preferred_element_type=F32
            )
        o_ref[kv] = acc[0:1, :] + b1_ref[kv]


def _pe_bias(pe, w1, b1):
    hid = w1.shape[-1]
    return pl.pallas_call(
        _pe_bias_kernel,
        out_shape=jax.ShapeDtypeStruct((2, 1, hid), F32),
        compiler_params=pltpu.CompilerParams(vmem_limit_bytes=VMEM_LIMIT),
        name="cmp_pe_bias",
    )(pe, w1, b1.reshape(2, 1, hid))


def _compress_kernel(pt_ref, *refs, n_pg):
    del pt_ref
    page_refs = refs[:n_pg]
    w1_ref, bias_ref, w2_ref, b2_ref, gk_ref, kc_ref, vc_ref, carry_ref, out_scr = refs[n_pg:]
    sub_per_page = PAGE_SIZE // CMP_STRIDE
    nsub = n_pg * sub_per_page
    rows = nsub * HEADS_PER_ROW
    page_rows = sub_per_page * HEADS_PER_ROW

    @pl.when(pl.program_id(1) == 0)
    def _():
        carry_ref[...] = jnp.zeros_like(carry_ref)

    half = rows // 2
    hid2 = w1_ref.shape[1] // 2
    hid_w = hid2 // 2
    top = lax.broadcasted_iota(jnp.int32, (nsub // 2, HEADS_PER_ROW, 2 * HEAD_DIM), 1) < N_KV_HEADS
    acc = [None, None]
    for pp in range(0, CMP_STRIDE, 2):
        pieces = []
        for pg in range(n_pg):
            a = page_refs[pg][0, pl.ds(pp, sub_per_page, stride=CMP_STRIDE), :, :].reshape(page_rows, HEAD_DIM)
            b = page_refs[pg][0, pl.ds(pp + 1, sub_per_page, stride=CMP_STRIDE), :, :].reshape(page_rows, HEAD_DIM)
            pieces.append(jnp.concatenate([a, b], axis=1))
        x3 = jnp.concatenate(pieces, axis=0).reshape(nsub // 2, 2 * HEADS_PER_ROW, 2 * HEAD_DIM)
        even = x3[:, :HEADS_PER_ROW]
        odd = x3[:, HEADS_PER_ROW:]
        xs = (jnp.where(top, even, pltpu.roll(odd, N_KV_HEADS, 1)),
              jnp.where(top, pltpu.roll(even, N_KV_HEADS, 1), odd))
        w1 = w1_ref[pp * HEAD_DIM : (pp + 2) * HEAD_DIM, :]
        for kv in range(2):
            d = jnp.dot(xs[kv].reshape(half, 2 * HEAD_DIM).astype(BF16), w1[:, kv * hid2 : (kv + 1) * hid2],
                        preferred_element_type=F32)
            acc[kv] = d if acc[kv] is None else acc[kv] + d
    first_rows = lax.broadcasted_iota(jnp.int32, (half, hid_w), 0) < N_KV_HEADS
    pad = jnp.zeros((half - SUBLANES, hid_w), F32)
    for kv in range(2):
        p_first = acc[kv][:, :hid_w]
        p_second = acc[kv][:, hid_w:]
        rolled = pltpu.roll(p_first, N_KV_HEADS, 0)
        shifted = jnp.where(first_rows, jnp.concatenate([carry_ref[kv], pad], axis=0), rolled)
        carry_ref[kv] = rolled[0:SUBLANES, :]
        hid = shifted + p_second + bias_ref[kv]
        hid = hid * _sigmoid(hid)
        out = jnp.dot(hid.astype(BF16), w2_ref[:, kv * HEAD_DIM : (kv + 1) * HEAD_DIM],
                      preferred_element_type=F32) + b2_ref[kv]
        if kv == 0:
            out = _rms(out) * gk_ref[...]
        out_scr[kv] = out
        dst = kc_ref if kv == 0 else vc_ref
        for h in range(N_KV_HEADS):
            dst[0, h, :, :] = out_scr[kv, pl.ds(h, nsub, stride=N_KV_HEADS), :].astype(BF16)


def _compress(pages, page_table, w1all, bias, w2all, b2, gk):
    n_seq, n_pp = page_table.shape
    n_pg = max(d for d in range(1, MAX_PAGES_PER_STEP + 1) if n_pp % d == 0)
    steps = n_pp // n_pg
    nsub = n_pg * (PAGE_SIZE // CMP_STRIDE)
    n_slots = steps * nsub
    hid = bias.shape[-1]

    def page_spec(k):
        return pl.BlockSpec((1, PAGE_SIZE, HEADS_PER_ROW, HEAD_DIM),
                            lambda b, s, pt: (pt[b * n_pp + s * n_pg + k], 0, 0, 0))

    const2 = lambda b, s, pt: (0, 0)
    const3 = lambda b, s, pt: (0, 0, 0)
    out_spec = pl.BlockSpec((1, N_KV_HEADS, nsub, HEAD_DIM), lambda b, s, pt: (b, 0, s, 0))
    grid_spec = pltpu.PrefetchScalarGridSpec(
        num_scalar_prefetch=1,
        grid=(n_seq, steps),
        in_specs=[page_spec(k) for k in range(n_pg)]
        + [
            pl.BlockSpec(w1all.shape, const2),
            pl.BlockSpec((2, 1, hid), const3),
            pl.BlockSpec(w2all.shape, const2),
            pl.BlockSpec((2, 1, HEAD_DIM), const3),
            pl.BlockSpec((1, HEAD_DIM), const2),
        ],
        out_specs=[out_spec, out_spec],
        scratch_shapes=[pltpu.VMEM((2, SUBLANES, hid), F32), pltpu.VMEM((2, nsub * N_KV_HEADS, HEAD_DIM), F32)],
    )
    shape = jax.ShapeDtypeStruct((n_seq, N_KV_HEADS, n_slots, HEAD_DIM), BF16)
    return pl.pallas_call(
        functools.partial(_compress_kernel, n_pg=n_pg),
        grid_spec=grid_spec,
        out_shape=[shape, shape],
        compiler_params=_params("parallel", "arbitrary"),
        name="compress",
    )(page_table.reshape(-1), *([pages] * n_pg), w1all, bias, w2all, b2.reshape(2, 1, HEAD_DIM), gk.reshape(1, HEAD_DIM))


def _nsa_prompt_kernel(
    q_ref, qr_ref, gate_ref, kc_ref, vc_ref, ks_ref, vs_ref, kw_ref, vw_ref, onehot_ref,
    o_ref, osel_scr, s_scr, p_scr, m_scr, owin_scr, sw_scr, pw_scr, mw_scr,
    *, tq, t_len, nb, n_sel, n_slots, kt):
    i = pl.program_id(2)
    q0 = i * tq
    rows = GROUP * tq
    scale = HEAD_DIM ** -0.5
    heads = [slice(g * HEAD_DIM, (g + 1) * HEAD_DIM) for g in range(GROUP)]
    q = jnp.concatenate([q_ref[:, sl] for sl in heads], axis=0)
    qr = jnp.concatenate([qr_ref[:, sl] for sl in heads], axis=0)

    def attend(lhs, rhs_chunk, v_ref, k0, n_keys, mask_chunk, s_buf, p_buf, m_buf):
        n_chunks = -(-n_keys // SCORE_CHUNK)
        for c in range(n_chunks):
            c0 = c * SCORE_CHUNK
            w = min(SCORE_CHUNK, n_keys - c0)
            s = lax.dot_general(lhs, rhs_chunk(c0, w), NT_DIMS, preferred_element_type=F32)
            s = mask_chunk(c0, w, s.reshape(GROUP, tq, w)).reshape(rows, w)
            s_buf[:, c0 : c0 + w] = s
            pm = s[:, 0:LANES]
            for j0 in range(LANES, w, LANES):
                pm = jnp.maximum(pm, s[:, j0 : j0 + LANES])
            m_buf[:, c * LANES : (c + 1) * LANES] = pm
        coef = np.float32(scale * np.log2(np.e))
        outs = []
        for g in range(GROUP):
            sums = []
            for r0 in range(g * tq, (g + 1) * tq, ROW_TILE):
                r = slice(r0, r0 + ROW_TILE)
                m_acc = m_buf[r, 0:LANES]
                for c in range(1, n_chunks):
                    m_acc = jnp.maximum(m_acc, m_buf[r, c * LANES : (c + 1) * LANES])
                m_b = jnp.broadcast_to(jnp.max(m_acc, axis=-1, keepdims=True), (ROW_TILE, LANES))
                l_acc = jnp.zeros((ROW_TILE, LANES), F32)
                for c0 in range(0, n_keys, LANES):
                    p = jnp.exp2((s_buf[r, c0 : c0 + LANES] - m_b) * coef)
                    l_acc = l_acc + p
                    p_buf[r, c0 : c0 + LANES] = p.astype(BF16)
                sums.append(jnp.sum(l_acc, axis=-1, keepdims=True))
            rg = slice(g * tq, (g + 1) * tq)
            pv = jnp.dot(p_buf[rg, 0:n_keys], v_ref[0, pl.ds(k0, n_keys), :], preferred_element_type=F32)
            outs.append(pv / jnp.concatenate(sums, axis=0))
        return jnp.concatenate(outs, axis=0)

    s_t = lax.dot_general(kc_ref[0, 0], q, NT_DIMS, preferred_element_type=F32) * scale
    slot = lax.broadcasted_iota(jnp.int32, (n_slots, rows), 0)
    tok = q0 + (lax.broadcasted_iota(jnp.int32, (n_slots, rows), 1) & (tq - 1))
    valid = (slot >= 1) & (slot * CMP_STRIDE + CMP_STRIDE <= tok + 1)
    s_m = jnp.where(valid, s_t, NEG_INF)
    e = jnp.exp(s_m - jnp.max(s_m, axis=0, keepdims=True))
    p_t = jnp.where(valid, e / jnp.sum(e, axis=0, keepdims=True), 0.0)
    o_cmp = jnp.dot(p_t.T.astype(BF16), vc_ref[0, 0], preferred_element_type=F32)

    p_sum = p_t[:, 0:tq]
    for g in range(1, GROUP):
        p_sum = p_sum + p_t[:, g * tq : (g + 1) * tq]
    ratio = SEL_BLOCK // CMP_STRIDE
    jj = lax.broadcasted_iota(jnp.int32, (nb, n_slots), 0)
    mm = lax.broadcasted_iota(jnp.int32, (nb, n_slots), 1)
    overlap = jnp.where((mm >= ratio * jj) & (mm <= ratio * jj + ratio), 1.0, 0.0)
    imp = jnp.dot(overlap, p_sum, precision=lax.Precision.HIGHEST, preferred_element_type=F32)
    jb = lax.broadcasted_iota(jnp.int32, (nb, tq), 0)
    pos = q0 + lax.broadcasted_iota(jnp.int32, (nb, tq), 1)
    cur = lax.shift_right_logical(pos, SEL_SHIFT)
    causal = jb * SEL_BLOCK <= pos
    forced = (jb == 0) | (jb == cur) | (jb == cur - 1)
    score = jnp.where(causal, jnp.where(forced, SEL_FORCE, imp), -1.0)
    cnts = [jnp.zeros((nb, tq), F32) for _ in range(4)]
    for i2 in range(nb):
        r = score[i2 : i2 + 1, :]
        cnts[i2 % 4] = cnts[i2 % 4] + jnp.where(r > score, 1.0, jnp.where((r == score) & (jb > i2), 1.0, 0.0))
    cnt = (cnts[0] + cnts[1]) + (cnts[2] + cnts[3])
    sel = jnp.where((cnt < n_sel) & causal, 1.0, 0.0)
    if nb < LANES:
        sel = jnp.concatenate([sel, jnp.zeros((LANES - nb, tq), F32)], axis=0)
    sel_q = ((sel.T - 1.0) * MASK_BIG).astype(BF16)
    lhs_sel = jnp.concatenate([qr, jnp.concatenate([sel_q] * GROUP, axis=0)], axis=1)

    def sel_rhs(c0, w):
        return jnp.concatenate([ks_ref[0, c0 : c0 + w, :], onehot_ref[c0 : c0 + w, :]], axis=1)

    def sel_mask(c0, w, s, first_diag_key):
        if c0 + w <= first_diag_key:
            return s
        kpos = c0 + lax.broadcasted_iota(jnp.int32, (tq, w), 1)
        qpos = q0 + lax.broadcasted_iota(jnp.int32, (tq, w), 0)
        return jnp.where((kpos <= qpos)[None], s, -MASK_BIG)

    lw = WINDOW + tq
    kw0 = pl.multiple_of(jnp.maximum(q0 - WINDOW, 0), tq)

    def window_mask(c0, w, s):
        diff = (q0 + lax.broadcasted_iota(jnp.int32, (tq, w), 0)) - (
            kw0 + c0 + lax.broadcasted_iota(jnp.int32, (tq, w), 1))
        return s + jnp.where((diff >= 0) & (diff < WINDOW), 0.0, NEG_INF)[None]

    owin_scr[...] = attend(qr, lambda c0, w: kw_ref[0, pl.ds(kw0 + c0, w), :], vw_ref, kw0, lw, window_mask,
                           sw_scr, pw_scr, mw_scr)

    n_tiles = (q0 + tq + kt - 1) // kt
    for nt in range(1, t_len // kt + 1):

        @pl.when(n_tiles == nt)
        def _(nt=nt):
            osel_scr[...] = attend(lhs_sel, sel_rhs, vs_ref, 0, nt * kt,
                                   functools.partial(sel_mask, first_diag_key=(nt - 1) * kt), s_scr, p_scr, m_scr)

    o_sel = osel_scr[...]
    o_win = owin_scr[...]
    gt = gate_ref[...]
    outs = []
    for g in range(GROUP):
        sl = slice(g * tq, (g + 1) * tq)
        outs.append(
            gt[:, g : g + 1] * o_cmp[sl]
            + gt[:, GROUP + g : GROUP + g + 1] * o_sel[sl]
            + gt[:, 2 * GROUP + g : 2 * GROUP + g + 1] * o_win[sl]
        )
    o_ref[...] = jnp.concatenate(outs, axis=1).astype(o_ref.dtype)


def _nsa_prompt(q_bf, qr_bf, gates, kc, vc, sel_bf, win_bf, n_b, t_len):
    tq = 128
    kt = min(512, t_len)
    nb = t_len // SEL_BLOCK
    n_slots = kc.shape[2]
    n_sel = min(N_SEL, nb)
    assert t_len % kt == 0 and t_len >= WINDOW + tq and nb <= LANES and n_slots == t_len // CMP_STRIDE
    ntq = t_len // tq
    m = n_b * t_len
    rows = GROUP * tq
    lw = WINDOW + tq
    onehot = (np.arange(t_len)[:, None] // SEL_BLOCK == np.arange(LANES)[None, :]).astype(np.float32)
    sel3 = sel_bf.reshape(n_b, t_len, ROW_W)
    win3 = win_bf.reshape(n_b, t_len, ROW_W)
    kspec = pl.BlockSpec((1, t_len, HEAD_DIM), lambda b, h, i: (b, 0, h))
    vspec = pl.BlockSpec((1, t_len, HEAD_DIM), lambda b, h, i: (b, 0, N_KV_HEADS + h))
    cspec = pl.BlockSpec((1, 1, n_slots, HEAD_DIM), lambda b, h, i: (b, h, 0, 0))
    qspec = pl.BlockSpec((tq, GROUP * HEAD_DIM), lambda b, h, i: (b * ntq + i, h))
    kern = functools.partial(
        _nsa_prompt_kernel, tq=tq, t_len=t_len, nb=nb, n_sel=n_sel, n_slots=n_slots, kt=kt)
    return pl.pallas_call(
        kern,
        grid=(n_b, N_KV_HEADS, ntq),
        in_specs=[
            qspec,
            qspec,
            pl.BlockSpec((tq, LANES), lambda b, h, i: (b * ntq + i, h)),
            cspec,
            cspec,
            kspec,
            vspec,
            kspec,
            vspec,
            pl.BlockSpec((t_len, LANES), lambda b, h, i: (0, 0)),
        ],
        out_specs=qspec,
        out_shape=jax.ShapeDtypeStruct((m, Q_W), BF16),
        scratch_shapes=[pltpu.VMEM((rows, HEAD_DIM), F32),
                        pltpu.VMEM((rows, t_len), F32), pltpu.VMEM((rows, t_len), BF16),
                        pltpu.VMEM((rows, -(-t_len // SCORE_CHUNK) * LANES), F32),
                        pltpu.VMEM((rows, HEAD_DIM), F32),
                        pltpu.VMEM((rows, lw), F32), pltpu.VMEM((rows, lw), BF16),
                        pltpu.VMEM((rows, -(-lw // SCORE_CHUNK) * LANES), F32)],
        compiler_params=_params("parallel", "parallel", "arbitrary"),
        name="nsa_prompt",
    )(q_bf, qr_bf, gates, kc, vc, sel3, sel3, win3, win3, jnp.asarray(onehot, BF16))


def _layer_norm(y, g, b):
    yc = y - jnp.mean(y, axis=-1, keepdims=True)
    var = jnp.mean(yc * yc, axis=-1, keepdims=True)
    return yc * lax.rsqrt(var + EPS) * g + b


def _glu(z):
    c = z.shape[1] // 2
    return z[:, :c] * _sigmoid(z[:, c:])


def _conv_prompt_kernel(z_ref, halo_ref, wdw_ref, bdw_ref, gln_ref, bln_ref, wout_ref, o_ref, wattn_ref,
                        gm_a_ref, gm_c_ref, mixed_ref, tail_ref, ufull, yconv, yattn, *, tc, width):
    i = pl.program_id(1)
    u = _glu(z_ref[...])
    uh = jnp.where(i == 0, 0.0, _glu(halo_ref[...]))
    ufull[0:CONV_HALO, :] = uh
    ufull[CONV_HALO:, :] = u
    tail_ref[0] = u[tc - CONV_HALO :, :]
    ch = u.shape[1]
    rc, cc = 64, 256
    off = CONV_HALO - (width - 1)
    d_out = wattn_ref.shape[1]
    n_conv_chunks = (tc // rc) * (ch // cc)
    aw = d_out // n_conv_chunks * 2
    step = 0
    for r0 in range(0, tc, rc):
        for c0 in range(0, ch, cc):
            if step % 2 == 0:
                a0 = step // 2 * aw
                yattn[:, a0 : a0 + aw] = jnp.dot(o_ref[...], wattn_ref[:, a0 : a0 + aw], preferred_element_type=F32)
            step += 1
            acc = jnp.zeros((rc, cc), F32)
            span_all = rc + CONV_HALO
            u_rows = ufull[r0 : r0 + span_all, c0 : c0 + cc]
            for b in range(min(SUBLANES, width)):
                n_a = (width - 1 - b) // SUBLANES + 1
                ub = pltpu.roll(u_rows, (span_all - off - b) % span_all, 0)
                for a in range(n_a):
                    k = SUBLANES * a + b
                    acc = acc + wdw_ref[k : k + 1, c0 : c0 + cc] * ub[SUBLANES * a : SUBLANES * a + rc, :]
            yconv[r0 : r0 + rc, c0 : c0 + cc] = acc + bdw_ref[:, c0 : c0 + cc]
    y = _layer_norm(yconv[...], gln_ref[...], bln_ref[...])
    y = y * _sigmoid(y)
    y_conv = jnp.dot(y.astype(BF16), wout_ref[...], preferred_element_type=F32)
    mixed_ref[...] = (_sigmoid(gm_a_ref[...]) * yattn[...] + _sigmoid(gm_c_ref[...]) * y_conv).astype(mixed_ref.dtype)


def _conv_mix_prompt(zmain, o, w_dw, b_dw, g_ln, b_ln, w_out, w_attn, n_b, t_len):
    tc = 256
    width, ch = w_dw.shape
    d_out = w_out.shape[1]
    ntc = t_len // tc
    hb = tc // CONV_HALO
    m = n_b * t_len
    glu_blk = 0
    gm_blk = (zmain.shape[1] - 2 * d_out) // d_out
    row1 = lambda b, i: (0, 0)
    tile = lambda blk: pl.BlockSpec((tc, d_out), lambda b, i: (b * ntc + i, blk))
    kern = functools.partial(_conv_prompt_kernel, tc=tc, width=width)
    return pl.pallas_call(
        kern,
        grid=(n_b, ntc),
        in_specs=[
            pl.BlockSpec((tc, 2 * ch), lambda b, i: (b * ntc + i, glu_blk)),
            pl.BlockSpec((CONV_HALO, 2 * ch), lambda b, i: (jnp.maximum((b * ntc + i) * hb - 1, 0), glu_blk)),
            pl.BlockSpec((width, ch), row1),
            pl.BlockSpec((1, ch), row1),
            pl.BlockSpec((1, ch), row1),
            pl.BlockSpec((1, ch), row1),
            pl.BlockSpec((ch, d_out), row1),
            pl.BlockSpec((tc, o.shape[1]), lambda b, i: (b * ntc + i, 0)),
            pl.BlockSpec(w_attn.shape, row1),
            tile(gm_blk),
            tile(gm_blk + 1),
        ],
        out_specs=[
            tile(0),
            pl.BlockSpec((1, CONV_HALO, ch), lambda b, i: (b, 0, 0)),
        ],
        out_shape=[jax.ShapeDtypeStruct((m, d_out), BF16), jax.ShapeDtypeStruct((n_b, CONV_HALO, ch), F32)],
        scratch_shapes=[pltpu.VMEM((tc + CONV_HALO, ch), F32), pltpu.VMEM((tc, ch), F32), pltpu.VMEM((tc, d_out), F32)],
        compiler_params=_params("parallel", "arbitrary"),
        name="conv_mix_prompt",
    )(zmain, zmain, w_dw, b_dw.reshape(1, ch), g_ln.reshape(1, ch), b_ln.reshape(1, ch), w_out, o, w_attn,
      zmain, zmain)


def _conv_sample_kernel(z_ref, st_ref, wdw_ref, bdw_ref, gln_ref, bln_ref, wout_ref, y_ref, u_ref, *, width):
    u = _glu(z_ref[...])
    u_ref[...] = u
    y = jnp.sum(st_ref[...] * wdw_ref[0 : width - 1, :][None], axis=1) + wdw_ref[width - 1 : width, :] * u
    y = _layer_norm(y + bdw_ref[...], gln_ref[...], bln_ref[...])
    y = y * _sigmoid(y)
    y_ref[...] = jnp.dot(y.astype(BF16), wout_ref[...], preferred_element_type=F32)


def _conv_sample(zmain, state, w_dw, b_dw, g_ln, b_ln, w_out):
    width, ch = w_dw.shape
    n_b = state.shape[0]
    d_out = w_out.shape[1]
    glu_blk = 0
    c2 = lambda i: (0, 0)
    return pl.pallas_call(
        functools.partial(_conv_sample_kernel, width=width),
        grid=(1,),
        in_specs=[
            pl.BlockSpec((n_b, 2 * ch), lambda i: (0, glu_blk)),
            pl.BlockSpec((n_b, width - 1, ch), lambda i: (0, 0, 0)),
            pl.BlockSpec((width, ch), c2),
            pl.BlockSpec((1, ch), c2),
            pl.BlockSpec((1, ch), c2),
            pl.BlockSpec((1, ch), c2),
            pl.BlockSpec((ch, d_out), c2),
        ],
        out_specs=[pl.BlockSpec((n_b, d_out), c2), pl.BlockSpec((n_b, ch), c2)],
        out_shape=[jax.ShapeDtypeStruct((n_b, d_out), F32), jax.ShapeDtypeStruct((n_b, ch), F32)],
        compiler_params=_params("arbitrary"),
        name="conv_sample",
    )(zmain, state, w_dw, b_dw.reshape(1, ch), g_ln.reshape(1, ch), b_ln.reshape(1, ch), w_out)


def _attn_mix_kernel(o_ref, w_ref, gm_a_ref, gm_c_ref, yc_ref, mixed_ref):
    ya = jnp.dot(o_ref[...], w_ref[...], preferred_element_type=F32)
    mixed_ref[...] = (_sigmoid(gm_a_ref[...]) * ya + _sigmoid(gm_c_ref[...]) * yc_ref[...]).astype(mixed_ref.dtype)


def _attn_mix(o, w_attn, zmain, y_conv, tm):
    m, k = o.shape
    d = w_attn.shape[1]
    gm_blk = (zmain.shape[1] - 2 * d) // d
    return pl.pallas_call(
        _attn_mix_kernel,
        grid=(m // tm,),
        in_specs=[
            pl.BlockSpec((tm, k), lambda i: (i, 0)),
            pl.BlockSpec((k, d), lambda i: (0, 0)),
            pl.BlockSpec((tm, d), lambda i: (i, gm_blk)),
            pl.BlockSpec((tm, d), lambda i: (i, gm_blk + 1)),
            pl.BlockSpec((tm, d), lambda i: (i, 0)),
        ],
        out_specs=pl.BlockSpec((tm, d), lambda i: (i, 0)),
        out_shape=jax.ShapeDtypeStruct((m, d), BF16),
        compiler_params=_params("parallel"),
        name="attn_out_mix",
    )(o, w_attn, zmain, zmain, y_conv)


def _out_proj_kernel(mixed_ref, w_ref, x_ref, h_ref):
    h_ref[...] = x_ref[...] + jnp.dot(mixed_ref[...], w_ref[...], preferred_element_type=F32)


def _out_proj(mixed, w_out, x, tm):
    m, k = mixed.shape
    d = w_out.shape[1]
    return pl.pallas_call(
        _out_proj_kernel,
        grid=(m // tm,),
        in_specs=[
            pl.BlockSpec((tm, k), lambda i: (i, 0)),
            pl.BlockSpec((k, d), lambda i: (0, 0)),
            pl.BlockSpec((tm, d), lambda i: (i, 0)),
        ],
        out_specs=pl.BlockSpec((tm, d), lambda i: (i, 0)),
        out_shape=jax.ShapeDtypeStruct((m, d), F32),
        compiler_params=_params("parallel"),
        name="out_proj",
    )(mixed, w_out, x)


def _gelu(x):
    return 0.5 * x * (1.0 + lax.erf(x * np.float32(np.sqrt(0.5))))


def _ffn_kernel(*refs, tm, tiles_per_seq, sample):
    if sample:
        h_ref, g_ref, wa_ref, wb_ref, wdw_ref, bdw_ref, wd_ref, s0_ref, s1_ref, out_ref, a_ref, hn, acc = refs
    else:
        h_ref, halo_ref, g_ref, wa_ref, wb_ref, wdw_ref, bdw_ref, wd_ref, out_ref, a_ref, hn, acc = refs
    i = pl.program_id(0)
    j = pl.program_id(1)
    lead = 0 if sample else FFN_HALO

    @pl.when(j == 0)
    def _():
        hn[lead:, :] = (_rms(h_ref[...]) * g_ref[...]).astype(BF16)
        if not sample:
            hn[0:lead, :] = (_rms(halo_ref[...]) * g_ref[...]).astype(BF16)
        acc[...] = jnp.zeros_like(acc)

    a_ext = jnp.dot(hn[...], wa_ref[...], preferred_element_type=F32)
    gate = jnp.dot(hn[lead:, :], wb_ref[...], preferred_element_type=F32)
    a0 = a_ext[lead:, :]
    if sample:
        a_m2 = s0_ref[...]
        a_m1 = s1_ref[...]
        a_ref[...] = a0
    else:
        first = (i % tiles_per_seq) == 0
        r = lax.broadcasted_iota(jnp.int32, a0.shape, 0)
        a_m2 = jnp.where(first & (r < 2), 0.0, a_ext[lead - 2 : lead - 2 + tm, :])
        a_m1 = jnp.where(first & (r < 1), 0.0, a_ext[lead - 1 : lead - 1 + tm, :])
        a_ref[0] = a0[tm - SUBLANES :, :]
    conv = wdw_ref[0:1, :] * a_m2 + wdw_ref[1:2, :] * a_m1 + wdw_ref[2:3, :] * a0 + bdw_ref[...]
    y = (_gelu(conv) * gate).astype(BF16)
    acc[...] += jnp.dot(y, wd_ref[...], preferred_element_type=F32)

    @pl.when(j == pl.num_programs(1) - 1)
    def _():
        out_ref[...] = h_ref[...] + acc[...]


def _conv_ffn(h, g, w_up_a, w_up_b, w_dw, b_dw, w_down, *, tm, t_len=None, state=None):
    m, d = h.shape
    dff = w_up_a.shape[1]
    tf = 512
    nf = dff // tf
    sample = state is not None
    assert w_dw.shape[0] == 3
    col = lambda i, j: (0, j)
    common = [
        pl.BlockSpec((1, d), lambda i, j: (0, 0)),
        pl.BlockSpec((d, tf), col),
        pl.BlockSpec((d, tf), col),
        pl.BlockSpec((3, tf), col),
        pl.BlockSpec((1, tf), col),
        pl.BlockSpec((tf, d), lambda i, j: (j, 0)),
    ]
    args = [g.reshape(1, d), w_up_a, w_up_b, w_dw, b_dw.reshape(1, dff), w_down]
    hspec = pl.BlockSpec((tm, d), lambda i, j: (i, 0))
    if sample:
        assert m == tm
        st = state.reshape(m, 2 * dff)
        in_specs = [hspec] + common + [pl.BlockSpec((tm, tf), col), pl.BlockSpec((tm, tf), lambda i, j: (0, nf + j))]
        args = [h] + args + [st, st]
        a_spec = pl.BlockSpec((tm, tf), col)
        a_shape = jax.ShapeDtypeStruct((m, dff), F32)
        tiles_per_seq = 1
        lead = 0
    else:
        tiles_per_seq = t_len // tm
        hb = tm // FFN_HALO
        in_specs = [hspec, pl.BlockSpec((FFN_HALO, d), lambda i, j: (jnp.maximum(i * hb - 1, 0), 0))] + common
        args = [h, h] + args
        a_spec = pl.BlockSpec((1, SUBLANES, tf), lambda i, j: (i, 0, j))
        a_shape = jax.ShapeDtypeStruct((m // tm, SUBLANES, dff), F32)
        lead = FFN_HALO
    kern = functools.partial(_ffn_kernel, tm=tm, tiles_per_seq=tiles_per_seq, sample=sample)
    return pl.pallas_call(
        kern,
        grid=(m // tm, nf),
        in_specs=in_specs,
        out_specs=[hspec, a_spec],
        out_shape=[jax.ShapeDtypeStruct((m, d), F32), a_shape],
        scratch_shapes=[pltpu.VMEM((tm + lead, d), BF16), pltpu.VMEM((tm, d), F32)],
        compiler_params=_params("parallel", "arbitrary"),
        name="conv_ffn",
    )(*args)


def _ple_kernel(h_ref, g_ref, wg_ref, pe_ref, wp_ref, o_ref):
    h = h_ref[...]
    hn = (_rms(h) * g_ref[...]).astype(BF16)
    gate = _sigmoid(jnp.dot(hn, wg_ref[...], preferred_element_type=F32))
    proj = jnp.dot(pe_ref[...], wp_ref[...], preferred_element_type=F32)
    o_ref[...] = h + gate * proj


def _ple(h, g, w_gate, pe, w_proj, tm):
    m, d = h.shape
    pd = pe.shape[1]
    return pl.pallas_call(
        _ple_kernel,
        grid=(m // tm,),
        in_specs=[
            pl.BlockSpec((tm, d), lambda i: (i, 0)),
            pl.BlockSpec((1, d), lambda i: (0, 0)),
            pl.BlockSpec((d, d), lambda i: (0, 0)),
            pl.BlockSpec((tm, pd), lambda i: (i, 0)),
            pl.BlockSpec((pd, d), lambda i: (0, 0)),
        ],
        out_specs=pl.BlockSpec((tm, d), lambda i: (i, 0)),
        out_shape=jax.ShapeDtypeStruct((m, d), F32),
        compiler_params=_params("parallel"),
        name="ple",
    )(h, g.reshape(1, d), w_gate, pe, w_proj)


def _cmp_sample_kernel(q_ref, kc_ref, vc_ref, ov_ref, score_ref, ocmp_ref, *, pos, nb, n_slots):
    scale = HEAD_DIM ** -0.5
    qn = q_ref[0]
    slot = lax.broadcasted_iota(jnp.int32, (GROUP, n_slots), 1)
    valid = (slot >= 1) & (slot * CMP_STRIDE + CMP_STRIDE <= pos + 1)
    sums = []
    for h in range(N_KV_HEADS):
        qh = qn[h * GROUP : (h + 1) * GROUP, :].astype(BF16)
        s = lax.dot_general(qh, kc_ref[0, h], NT_DIMS, preferred_element_type=F32) * scale
        e, l = _softmax_rows(jnp.where(valid, s, NEG_INF))
        p = jnp.where(valid, e / l, 0.0)
        ocmp_ref[0, h * GROUP : (h + 1) * GROUP, :] = jnp.dot(p.astype(BF16), vc_ref[0, h], preferred_element_type=F32)
        sums.append(jnp.sum(p, axis=0, keepdims=True))
    p_sum = jnp.concatenate(sums + [jnp.zeros((SUBLANES - N_KV_HEADS, n_slots), F32)], axis=0)
    imp = jnp.dot(p_sum, ov_ref[...], precision=lax.Precision.HIGHEST, preferred_element_type=F32)
    jb = lax.broadcasted_iota(jnp.int32, imp.shape, 1)
    cur = pos // SEL_BLOCK
    causal = (jb * SEL_BLOCK <= pos) & (jb < nb)
    forced = (jb == 0) | (jb == cur) | (jb == cur - 1)
    score_ref[0] = jnp.where(jb < nb, jnp.where(causal, jnp.where(forced, SEL_FORCE, imp), -1.0), -2.0)


def _cmp_sample(q3, kc, vc, pos, nb):
    n_b = q3.shape[0]
    n_slots = kc.shape[2]
    nbp = -(-nb // LANES) * LANES
    ratio = SEL_BLOCK // CMP_STRIDE
    mm = np.arange(n_slots)[:, None]
    jj = np.arange(nbp)[None, :]
    overlap = ((mm >= ratio * jj) & (mm <= ratio * jj + ratio) & (jj < nb)).astype(np.float32)
    qspec = pl.BlockSpec((1, N_HEADS, HEAD_DIM), lambda b: (b, 0, 0))
    cspec = pl.BlockSpec((1, N_KV_HEADS, n_slots, HEAD_DIM), lambda b: (b, 0, 0, 0))
    return pl.pallas_call(
        functools.partial(_cmp_sample_kernel, pos=pos, nb=nb, n_slots=n_slots),
        grid=(n_b,),
        in_specs=[qspec, cspec, cspec, pl.BlockSpec((n_slots, nbp), lambda b: (0, 0))],
        out_specs=[pl.BlockSpec((1, SUBLANES, nbp), lambda b: (b, 0, 0)), qspec],
        out_shape=[
            jax.ShapeDtypeStruct((n_b, SUBLANES, nbp), F32),
            jax.ShapeDtypeStruct((n_b, N_HEADS, HEAD_DIM), F32),
        ],
        compiler_params=_params("parallel"),
        name="cmp_sample",
    )(q3, kc, vc, jnp.asarray(overlap))


def _topk_kernel(score_ref, idx_ref, *, n_sel):
    sc = score_ref[...]
    lane = lax.broadcasted_iota(jnp.int32, sc.shape, 1)
    out_lane = lax.broadcasted_iota(jnp.int32, idx_ref.shape, 1)
    out = jnp.zeros(idx_ref.shape, jnp.int32)
    for it in range(n_sel):
        mx = jnp.max(sc, axis=-1, keepdims=True)
        idx = jnp.min(jnp.where(sc == mx, lane, sc.shape[1]), axis=-1, keepdims=True)
        out = jnp.where(out_lane == it, idx, out)
        sc = jnp.where(lane == idx, -3.0, sc)
    idx_ref[...] = out


def _topk(score, n_sel):
    rows = score.shape[0]
    return pl.pallas_call(
        functools.partial(_topk_kernel, n_sel=n_sel),
        out_shape=jax.ShapeDtypeStruct((rows, LANES), jnp.int32),
        compiler_params=pltpu.CompilerParams(vmem_limit_bytes=VMEM_LIMIT),
        name="topk_sample",
    )(score)


def _sel_sample_kernel(idx_ref, pt_ref, *refs, n_sel, n_past_blocks):
    del pt_ref
    blk_refs = refs[:n_sel]
    qr_ref, new_ref, o_ref = refs[n_sel:]
    b = pl.program_id(0)
    h = pl.program_id(1)
    scale = HEAD_DIM ** -0.5
    blk_rows = SEL_BLOCK * HEADS_PER_ROW
    qr = qr_ref[0, 0].astype(BF16)
    rows = jnp.concatenate([r[...] for r in blk_refs], axis=0).astype(BF16)
    s = lax.dot_general(qr, rows, NT_DIMS, preferred_element_type=F32) * scale
    lane = lax.broadcasted_iota(jnp.int32, s.shape, 1)
    slot = lane // blk_rows
    bias = jnp.where((lane & (HEADS_PER_ROW - 1)) == h, 0.0, NEG_INF)
    n_new = jnp.int32(0)
    for j in range(n_sel):
        is_new = idx_ref[(b * N_KV_HEADS + h) * n_sel + j] >= n_past_blocks
        bias = jnp.where(slot == j, jnp.where(is_new, NEG_INF, bias), bias)
        n_new = n_new + jnp.where(is_new, 1, 0)
    s = s + bias
    k_new = new_ref[0, pl.ds(h, 1), :].astype(BF16).astype(F32)
    v_new = new_ref[0, pl.ds(N_KV_HEADS + h, 1), :].astype(BF16).astype(F32)
    s_new = jnp.sum(qr.astype(F32) * k_new, axis=-1, keepdims=True) * scale
    s_new = jnp.where(n_new > 0, s_new, NEG_INF)
    m = jnp.maximum(jnp.max(s, axis=-1, keepdims=True), s_new)
    e = jnp.exp(s - m)
    e_new = jnp.exp(s_new - m)
    l = jnp.sum(e, axis=-1, keepdims=True) + e_new
    e_v = pltpu.roll(e, N_KV_HEADS, 1)
    o = jnp.dot(e_v.astype(BF16), rows, preferred_element_type=F32) + e_new.astype(BF16).astype(F32) * v_new
    o_ref[0, 0] = o / l


def _sel_sample(idx_flat, pt_flat, cache_rows, qr4, new_rows, n_sel, n_past_blocks, n_pp):
    n_b = qr4.shape[0]
    per_page = PAGE_SIZE // SEL_BLOCK
    blk_rows = SEL_BLOCK * HEADS_PER_ROW

    def bspec(j):
        def imap(b, h, idx, pt):
            blk_id = jnp.minimum(idx[(b * N_KV_HEADS + h) * n_sel + j], n_past_blocks - 1)
            return (pt[b * n_pp + blk_id // per_page] * per_page + blk_id % per_page, 0)
        return pl.BlockSpec((blk_rows, HEAD_DIM), imap)

    grid_spec = pltpu.PrefetchScalarGridSpec(
        num_scalar_prefetch=2,
        grid=(n_b, N_KV_HEADS),
        in_specs=[bspec(j) for j in range(n_sel)]
        + [
            pl.BlockSpec((1, 1, GROUP, HEAD_DIM), lambda b, h, idx, pt: (b, h, 0, 0)),
            pl.BlockSpec((1, HEADS_PER_ROW, HEAD_DIM), lambda b, h, idx, pt: (b, 0, 0)),
        ],
        out_specs=pl.BlockSpec((1, 1, GROUP, HEAD_DIM), lambda b, h, idx, pt: (b, h, 0, 0)),
    )
    return pl.pallas_call(
        functools.partial(_sel_sample_kernel, n_sel=n_sel, n_past_blocks=n_past_blocks),
        grid_spec=grid_spec,
        out_shape=jax.ShapeDtypeStruct((n_b, N_KV_HEADS, GROUP, HEAD_DIM), F32),
        compiler_params=_params("parallel", "arbitrary"),
        name="sel_sample",
    )(idx_flat, pt_flat, *([cache_rows] * n_sel), qr4, new_rows)


def _head_lane_mask(shape, kv_head_of_row):
    lane = lax.broadcasted_iota(jnp.int32, shape, 1)
    return (lane & (HEADS_PER_ROW - 1)) == kv_head_of_row


def _win_sample_kernel(win_ref, new_ref, qr_ref, ocmp_ref, osel_ref, g_ref, nw_ref, o_ref, *, wb):
    scale = HEAD_DIM ** -0.5
    keep = (wb - 1) * HEADS_PER_ROW
    nw_ref[0, 0:keep, :] = win_ref[0, HEADS_PER_ROW:, :]
    nw_ref[0, keep:, :] = new_ref[0]
    rows = nw_ref[0].astype(BF16)
    qr = qr_ref[0].astype(BF16)
    s = lax.dot_general(qr, rows, NT_DIMS, preferred_element_type=F32) * scale
    kv_head = lax.broadcasted_iota(jnp.int32, s.shape, 0) // GROUP
    s = jnp.where(_head_lane_mask(s.shape, kv_head), s, NEG_INF)
    e, l = _softmax_rows(s)
    e_v = pltpu.roll(e, N_KV_HEADS, 1)
    o_win = jnp.dot(e_v.astype(BF16), rows, preferred_element_type=F32) / l
    g = g_ref[0]
    o = g[:, 0:1] * ocmp_ref[0] + g[:, 1:2] * osel_ref[0] + g[:, 2:3] * o_win
    o_ref[0] = o.astype(o_ref.dtype)


def _win_sample(win_rows, new_rows, qr3, o_cmp, o_sel, g3):
    n_b, wr, _ = win_rows.shape
    wb = wr // HEADS_PER_ROW
    hspec = pl.BlockSpec((1, N_HEADS, HEAD_DIM), lambda b: (b, 0, 0))
    wspec = pl.BlockSpec((1, wr, HEAD_DIM), lambda b: (b, 0, 0))
    return pl.pallas_call(
        functools.partial(_win_sample_kernel, wb=wb),
        grid=(n_b,),
        in_specs=[
            wspec,
            pl.BlockSpec((1, HEADS_PER_ROW, HEAD_DIM), lambda b: (b, 0, 0)),
            hspec,
            hspec,
            hspec,
            pl.BlockSpec((1, N_HEADS, 3), lambda b: (b, 0, 0)),
        ],
        out_specs=[wspec, hspec],
        out_shape=[jax.ShapeDtypeStruct((n_b, wr, HEAD_DIM), F32), jax.ShapeDtypeStruct((n_b, N_HEADS, HEAD_DIM), BF16)],
        compiler_params=_params("parallel"),
        name="win_sample",
    )(win_rows, new_rows, qr3, o_cmp, o_sel, g3)


def _rope_tables(pos):
    half = HEAD_DIM // 2
    inv = ROPE_THETA ** (-jnp.arange(half, dtype=F32) / half)
    ang = pos.astype(F32)[:, None] * inv
    cos = jnp.cos(ang)
    sin = jnp.sin(ang)
    return jnp.concatenate([cos, cos], axis=-1), jnp.concatenate([-sin, sin], axis=-1)


def _prep_weights(lw):
    w_in = lw["w_in"]
    c = 0
    parts = {}
    conv_ch = lw["w_conv_dw"].shape[1]
    d_model = w_in.shape[0]
    for name, width in (("q", Q_W), ("cmp", ROW_W), ("sel", ROW_W), ("win", ROW_W), ("gnsa", 3 * N_HEADS),
                        ("glu", 2 * conv_ch), ("gm", 2 * d_model)):
        parts[name] = w_in[:, c : c + width]
        c += width
    wg = parts["gnsa"].reshape(d_model, 3, N_KV_HEADS, GROUP).transpose(0, 2, 1, 3).reshape(d_model, N_KV_HEADS, 3 * GROUP)
    wg = jnp.pad(wg, ((0, 0), (0, 0), (0, LANES - 3 * GROUP))).reshape(d_model, N_KV_HEADS * LANES)
    w1 = lw["w_cmp1"]
    hid = w1.shape[-1]
    r = CMP_BLOCK // CMP_STRIDE
    w1all = w1.reshape(2, r, CMP_STRIDE, HEAD_DIM, hid).transpose(2, 3, 0, 1, 4).reshape(CMP_STRIDE * HEAD_DIM, 2 * r * hid)
    dff = lw["w_up"].shape[1] // 2
    return {
        "w_q": parts["q"].astype(BF16),
        "w_main": jnp.concatenate([parts["glu"], parts["gm"]], axis=1).astype(BF16),
        "w_cmp_gate": jnp.concatenate([parts["cmp"], wg], axis=1).astype(BF16),
        "w_sel": parts["sel"].astype(BF16),
        "w_win": parts["win"].astype(BF16),
        "w1all": w1all.astype(BF16),
        "w2all": jnp.concatenate([lw["w_cmp2"][0], lw["w_cmp2"][1]], axis=1).astype(BF16),
        "w_attn_out": lw["w_attn_out"].astype(BF16),
        "w_conv_out": lw["w_conv_out"].astype(BF16),
        "w_out": lw["w_out"].astype(BF16),
        "w_up_a": lw["w_up"][:, :dff].astype(BF16),
        "w_up_b": lw["w_up"][:, dff:].astype(BF16),
        "w_down": lw["w_down"].astype(BF16),
        "w_ple_gate": lw["w_ple_gate"].astype(BF16),
        "w_ple_proj": lw["w_ple_proj"].astype(BF16),
    }


def _dense_tail(x, mixed, pe, lw, pw, *, tm, t_len=None, ffn_state=None):
    h1 = _out_proj(mixed, pw["w_out"], x, tm)
    h2, a_tail = _conv_ffn(h1, lw["g_ffn"], pw["w_up_a"], pw["w_up_b"], lw["w_ffn_dw"], lw["b_ffn_dw"], pw["w_down"],
                           tm=tm, t_len=t_len, state=ffn_state)
    h3 = _ple(h2, lw["g_ple"], pw["w_ple_gate"], pe.astype(BF16), pw["w_ple_proj"], tm)
    return h3, a_tail


def _layer_prompt(x3, pe3, lw, pw, bias_cmp):
    n_b, t_len, d = x3.shape
    m = n_b * t_len
    x = x3.reshape(m, d)
    cos, sin = _rope_tables(jnp.arange(t_len, dtype=jnp.int32))
    xn = _rmsnorm_bf16(x, lw["g_mix"], 512)
    zmain = _matmul(xn, pw["w_main"], 1024, 1024)
    rows_cmp, gates = _cmp_gate_proj(xn, pw["w_cmp_gate"], 512)
    rows_sel, sel_bf = _kv_proj(xn, pw["w_sel"], lw["g_k"][1], cos, sin, 512)
    rows_win, win_bf = _kv_proj(xn, pw["w_win"], lw["g_k"][2], cos, sin, 512)
    n_pp = t_len // PAGE_SIZE
    ident = jnp.arange(n_b * n_pp, dtype=jnp.int32).reshape(n_b, n_pp)
    kc, vc = _compress(rows_cmp.reshape(n_b * n_pp, PAGE_SIZE, HEADS_PER_ROW, HEAD_DIM), ident, pw["w1all"], bias_cmp,
                       pw["w2all"], lw["b_cmp2"], lw["g_k"][0])
    q_bf, qr_bf = _q_proj(xn, pw["w_q"], lw["g_q"], cos, sin, 512, BF16)
    o = _nsa_prompt(q_bf, qr_bf, gates, kc, vc, sel_bf, win_bf, n_b, t_len)
    mixed, u_tail = _conv_mix_prompt(zmain, o, lw["w_conv_dw"], lw["b_conv_dw"], lw["g_conv_ln"], lw["b_conv_ln"],
                                     pw["w_conv_out"], pw["w_attn_out"], n_b, t_len)
    h3, a_tail = _dense_tail(x, mixed, pe3.reshape(m, -1), lw, pw, tm=512, t_len=t_len)
    kvshape = (n_b, t_len, 2, N_KV_HEADS, HEAD_DIM)
    w_keep = min(WINDOW, t_len)
    conv_w = lw["w_conv_dw"].shape[0]
    state = (
        rows_cmp.reshape(kvshape),
        rows_sel.reshape(kvshape),
        rows_win.reshape(n_b, t_len * HEADS_PER_ROW, HEAD_DIM)[:, (t_len - w_keep) * HEADS_PER_ROW :].reshape(
            n_b, w_keep, 2, N_KV_HEADS, HEAD_DIM),
        u_tail[:, CONV_HALO - (conv_w - 1) :],
        a_tail.reshape(n_b, -1, SUBLANES, a_tail.shape[-1])[:, -1, SUBLANES - (lw["w_ffn_dw"].shape[0] - 1) :],
    )
    return h3.reshape(n_b, t_len, d), state


def _layer_sample(x3, pe3, lw, pw, bias_cmp, cache_cmp, cache_sel, cache_win, conv_state, ffn_state, page_table):
    n_b, t_new, d = x3.shape
    assert t_new == 1
    n_pp = page_table.shape[1]
    past_len = n_pp * PAGE_SIZE
    pos = past_len
    wb = cache_win.shape[1]
    assert wb == WINDOW and past_len % SEL_BLOCK == 0
    x = x3.reshape(n_b, d)
    cos, sin = _rope_tables(jnp.full((n_b,), pos, dtype=jnp.int32))
    xn = _rmsnorm_bf16(x, lw["g_mix"], n_b)
    zmain = _matmul(xn, pw["w_main"], n_b, 1024)
    rows_cmp, gates = _cmp_gate_proj(xn, pw["w_cmp_gate"], n_b)
    rows_sel, _ = _kv_proj(xn, pw["w_sel"], lw["g_k"][1], cos, sin, n_b)
    rows_win, _ = _kv_proj(xn, pw["w_win"], lw["g_k"][2], cos, sin, n_b)
    n_pool = cache_cmp.shape[0]
    kc, vc = _compress(cache_cmp.reshape(n_pool, PAGE_SIZE, HEADS_PER_ROW, HEAD_DIM), page_table, pw["w1all"], bias_cmp,
                       pw["w2all"], lw["b_cmp2"], lw["g_k"][0])
    n_past_blocks = past_len // SEL_BLOCK
    nb = n_past_blocks + 1
    n_sel = min(N_SEL, nb)
    q_s, qr_s = _q_proj(xn, pw["w_q"], lw["g_q"], cos, sin, n_b, F32)
    q3 = q_s.reshape(n_b, N_HEADS, HEAD_DIM)
    qr3 = qr_s.reshape(n_b, N_HEADS, HEAD_DIM)
    score, o_cmp = _cmp_sample(q3, kc, vc, pos, nb)
    idx = _topk(score.reshape(n_b * SUBLANES, -1), n_sel)
    idx = idx.reshape(n_b, SUBLANES, LANES)[:, :N_KV_HEADS, :n_sel]
    o_sel = _sel_sample(idx.reshape(-1), page_table.reshape(-1),
                        cache_sel.reshape(n_pool * PAGE_SIZE * HEADS_PER_ROW, HEAD_DIM),
                        qr3.reshape(n_b, N_KV_HEADS, GROUP, HEAD_DIM), rows_sel.reshape(n_b, HEADS_PER_ROW, HEAD_DIM),
                        n_sel, n_past_blocks, n_pp)
    g3 = gates.reshape(n_b, N_KV_HEADS, LANES)[:, :, : 3 * GROUP].reshape(n_b, N_KV_HEADS, 3, GROUP)
    g3 = g3.transpose(0, 1, 3, 2).reshape(n_b, N_HEADS, 3)
    new_win, o = _win_sample(cache_win.reshape(n_b, wb * HEADS_PER_ROW, HEAD_DIM),
                             rows_win.reshape(n_b, HEADS_PER_ROW, HEAD_DIM), qr3, o_cmp,
                             o_sel.reshape(n_b, N_HEADS, HEAD_DIM), g3)
    y_conv, u = _conv_sample(zmain, conv_state, lw["w_conv_dw"], lw["b_conv_dw"], lw["g_conv_ln"], lw["b_conv_ln"],
                             pw["w_conv_out"])
    mixed = _attn_mix(o.reshape(n_b, Q_W), pw["w_attn_out"], zmain, y_conv, n_b)
    h3, a_new = _dense_tail(x, mixed, pe3.reshape(n_b, -1), lw, pw, tm=n_b, ffn_state=ffn_state)
    kvshape = (n_b, 1, 2, N_KV_HEADS, HEAD_DIM)
    state = (
        rows_cmp.reshape(kvshape),
        rows_sel.reshape(kvshape),
        new_win.reshape(n_b, wb, 2, N_KV_HEADS, HEAD_DIM),
        jnp.concatenate([conv_state[:, 1:], u[:, None, :]], axis=1),
        jnp.concatenate([ffn_state[:, 1:], a_new[:, None, :]], axis=1),
    )
    return h3.reshape(n_b, 1, d), state


_LAYER_WEIGHTS = ("g_mix", "w_in", "g_q", "g_k", "w_cmp1", "b_cmp1", "w_cmp2", "b_cmp2", "pe_cmp", "w_attn_out",
                  "w_conv_dw", "b_conv_dw", "g_conv_ln", "b_conv_ln", "w_conv_out", "w_out", "g_ffn", "w_up",
                  "w_ffn_dw", "b_ffn_dw", "w_down", "g_ple", "w_ple_gate", "w_ple_proj")


def kernel(x_prompt, x_sample, p_prompt, p_sample, cache_cmp_kv, cache_sel_kv, cache_win_kv, state_conv, state_ffn_conv, page_table, g_mix, w_in, g_q, g_k, w_cmp1, b_cmp1, w_cmp2, b_cmp2, pe_cmp, w_attn_out, w_conv_dw, b_conv_dw, g_conv_ln, b_conv_ln, w_conv_out, w_out, g_ffn, w_up, w_ffn_dw, b_ffn_dw, w_down, g_ple, w_ple_gate, w_ple_proj):
    stacked = dict(zip(_LAYER_WEIGHTS, (g_mix, w_in, g_q, g_k, w_cmp1, b_cmp1, w_cmp2, b_cmp2, pe_cmp, w_attn_out,
                                        w_conv_dw, b_conv_dw, g_conv_ln, b_conv_ln, w_conv_out, w_out, g_ffn, w_up,
                                        w_ffn_dw, b_ffn_dw, w_down, g_ple, w_ple_gate, w_ple_proj)))
    depth = w_in.shape[0]
    hp, hs = x_prompt, x_sample
    states_p, states_s = [], []
    for i in range(depth):
        lw = {k: v[i] for k, v in stacked.items()}
        pw = _prep_weights(lw)
        bias_cmp = _pe_bias(lw["pe_cmp"], lw["w_cmp1"], lw["b_cmp1"])
        hp, st_p = _layer_prompt(hp, p_prompt[i], lw, pw, bias_cmp)
        hs, st_s = _layer_sample(hs, p_sample[i], lw, pw, bias_cmp, cache_cmp_kv[i], cache_sel_kv[i], cache_win_kv[i],
                                 state_conv[i], state_ffn_conv[i], page_table)
        states_p.append(st_p)
        states_s.append(st_s)
    outs = [hp, hs]
    for k in range(5):
        outs.append(jnp.stack([s[k] for s in states_p]))
        outs.append(jnp.stack([s[k] for s in states_s]))
    return tuple(outs)
```

```python
import functools

import numpy as np
import jax
import jax.numpy as jnp
from jax import lax
from jax.experimental import pallas as pl
from jax.experimental.pallas import tpu as pltpu

N_HEADS = 16
HEAD_DIM = 128
N_KV_HEADS = 4
GROUP = N_HEADS // N_KV_HEADS
Q_W = N_HEADS * HEAD_DIM
KV_W = N_KV_HEADS * HEAD_DIM
ROW_W = 2 * KV_W
HEADS_PER_ROW = 2 * N_KV_HEADS
CMP_BLOCK = 32
CMP_STRIDE = 16
SEL_BLOCK = 64
SEL_SHIFT = 6
N_SEL = 16
WINDOW = 512
PAGE_SIZE = 128
ROPE_THETA = 10000.0
EPS = 1e-6
NEG_INF = -1e30
SEL_FORCE = 1e6
MASK_BIG = 2.0 ** 100
CONV_HALO = 32
FFN_HALO = 16
SCORE_CHUNK = 512
ROW_TILE = 128
MAX_PAGES_PER_STEP = 32

LANES = 128
SUBLANES = 8
VMEM_LIMIT = 56 * 1024 * 1024

F32 = jnp.float32
BF16 = jnp.bfloat16
NT_DIMS = (((1,), (1,)), ((), ()))


def _params(*sem):
    return pltpu.CompilerParams(dimension_semantics=sem, vmem_limit_bytes=VMEM_LIMIT)


def _rms(x):
    return x * lax.rsqrt(jnp.mean(x * x, axis=-1, keepdims=True) + EPS)


def _rope(x, cos, sin_signed):
    return x * cos + pltpu.roll(x, HEAD_DIM // 2, 1) * sin_signed


def _sigmoid(x):
    return jax.nn.sigmoid(x)


def _softmax_rows(s):
    m = jnp.max(s, axis=-1, keepdims=True)
    e = jnp.exp(s - m)
    return e, jnp.sum(e, axis=-1, keepdims=True)


def _softmax_rows_scaled(s, scale):
    m = jnp.max(s, axis=-1, keepdims=True)
    e = jnp.exp2((s - m) * np.float32(scale * np.log2(np.e)))
    return e, jnp.sum(e, axis=-1, keepdims=True)


def _rmsnorm_kernel(x_ref, g_ref, o_ref):
    o_ref[...] = (_rms(x_ref[...]) * g_ref[...]).astype(o_ref.dtype)


def _rmsnorm_bf16(x, g, tm):
    m, d = x.shape
    return pl.pallas_call(
        _rmsnorm_kernel,
        grid=(m // tm,),
        in_specs=[pl.BlockSpec((tm, d), lambda i: (i, 0)), pl.BlockSpec((1, d), lambda i: (0, 0))],
        out_specs=pl.BlockSpec((tm, d), lambda i: (i, 0)),
        out_shape=jax.ShapeDtypeStruct((m, d), BF16),
        compiler_params=_params("parallel"),
        name="rmsnorm_in",
    )(x, g.reshape(1, d))


def _matmul_kernel(x_ref, w_ref, o_ref):
    o_ref[...] = jnp.dot(x_ref[...], w_ref[...], preferred_element_type=F32)


def _matmul(x, w, tm, tn):
    m, k = x.shape
    n = w.shape[1]
    return pl.pallas_call(
        _matmul_kernel,
        grid=(m // tm, n // tn),
        in_specs=[pl.BlockSpec((tm, k), lambda i, j: (i, 0)), pl.BlockSpec((k, tn), lambda i, j: (0, j))],
        out_specs=pl.BlockSpec((tm, tn), lambda i, j: (i, j)),
        out_shape=jax.ShapeDtypeStruct((m, n), F32),
        compiler_params=_params("parallel", "arbitrary"),
        name="in_proj_main",
    )(x, w)


def _q_epilogue(acc, gq_ref, cos_ref, sin_ref, q_ref, qr_ref):
    cos = cos_ref[...]
    sin = sin_ref[...]
    gq = gq_ref[...]
    for h0 in range(0, acc.shape[1], HEAD_DIM):
        sl = slice(h0, h0 + HEAD_DIM)
        qn = _rms(acc[:, sl]) * gq
        q_ref[:, sl] = qn.astype(q_ref.dtype)
        qr_ref[:, sl] = _rope(qn, cos, sin).astype(qr_ref.dtype)


def _q_kernel(x_ref, w_ref, gq_ref, cos_ref, sin_ref, q_ref, qr_ref):
    acc = jnp.dot(x_ref[...], w_ref[...], preferred_element_type=F32)
    _q_epilogue(acc, gq_ref, cos_ref, sin_ref, q_ref, qr_ref)


def _q_kernel_skewed(x_ref, w_ref, gq_ref, cos_ref, sin_ref, q_ref, qr_ref, acc_scr):
    i = pl.program_id(0)

    @pl.when(i == 0)
    def _():
        acc_scr[1] = jnp.zeros(acc_scr.shape[1:], F32)

    for parity in (0, 1):

        @pl.when(i % 2 == parity)
        def _(parity=parity):
            _q_epilogue(acc_scr[1 - parity], gq_ref, cos_ref, sin_ref, q_ref, qr_ref)
            acc_scr[parity] = jnp.dot(x_ref[...], w_ref[...], preferred_element_type=F32)


def _q_proj(xn, w, gq, cos, sin, tm, out_dtype):
    m, k = xn.shape
    n = w.shape[1]
    nt = cos.shape[0] // tm
    steps = m // tm
    skew = steps > 1
    prev = (lambda i: jnp.maximum(i - 1, 0)) if skew else (lambda i: i)
    ospec = pl.BlockSpec((tm, n), lambda i: (prev(i), 0))
    tspec = pl.BlockSpec((tm, HEAD_DIM), lambda i: (prev(i) % nt, 0))
    return pl.pallas_call(
        _q_kernel_skewed if skew else _q_kernel,
        grid=(steps + 1 if skew else steps,),
        in_specs=[pl.BlockSpec((tm, k), lambda i: (jnp.minimum(i, steps - 1), 0)),
                  pl.BlockSpec((k, n), lambda i: (0, 0)),
                  pl.BlockSpec((1, HEAD_DIM), lambda i: (0, 0)), tspec, tspec],
        out_specs=[ospec, ospec],
        out_shape=[jax.ShapeDtypeStruct((m, n), out_dtype), jax.ShapeDtypeStruct((m, n), out_dtype)],
        scratch_shapes=[pltpu.VMEM((2, tm, n), F32)] if skew else [],
        compiler_params=_params("arbitrary"),
        name="in_proj_q",
    )(xn, w, gq.reshape(1, HEAD_DIM), cos, sin)


def _store_rows(rows_ref, acc):
    tm = acc.shape[0]
    for j in range(HEADS_PER_ROW):
        rows_ref[pl.ds(j, tm, stride=HEADS_PER_ROW), :] = acc[:, j * HEAD_DIM : (j + 1) * HEAD_DIM]


def _cmp_gate_kernel(x_ref, w_ref, rows_ref, gate_ref):
    acc = jnp.dot(x_ref[...], w_ref[...], preferred_element_type=F32)
    _store_rows(rows_ref, acc[:, :ROW_W])
    gate_ref[...] = _sigmoid(acc[:, ROW_W:])


def _cmp_gate_proj(xn, w, tm):
    m, k = xn.shape
    n = w.shape[1]
    ng = n - ROW_W
    return pl.pallas_call(
        _cmp_gate_kernel,
        grid=(m // tm,),
        in_specs=[pl.BlockSpec((tm, k), lambda i: (i, 0)), pl.BlockSpec((k, n), lambda i: (0, 0))],
        out_specs=[pl.BlockSpec((tm * HEADS_PER_ROW, HEAD_DIM), lambda i: (i, 0)),
                   pl.BlockSpec((tm, ng), lambda i: (i, 0))],
        out_shape=[jax.ShapeDtypeStruct((m * HEADS_PER_ROW, HEAD_DIM), F32), jax.ShapeDtypeStruct((m, ng), F32)],
        compiler_params=_params("parallel"),
        name="in_proj_cmp_gates",
    )(xn, w)


def _kv_epilogue(acc, gk_ref, cos_ref, sin_ref, rows_ref, rows_bf_ref):
    cos = cos_ref[...]
    sin = sin_ref[...]
    gk = gk_ref[...]
    tm = acc.shape[0]
    for h in range(N_KV_HEADS):
        sl = slice(h * HEAD_DIM, (h + 1) * HEAD_DIM)
        k = _rope(_rms(acc[:, sl]) * gk, cos, sin)
        rows_ref[pl.ds(h, tm, stride=HEADS_PER_ROW), :] = k
        rows_bf_ref[:, sl] = k.astype(BF16)
        sv = slice(KV_W + h * HEAD_DIM, KV_W + (h + 1) * HEAD_DIM)
        rows_ref[pl.ds(N_KV_HEADS + h, tm, stride=HEADS_PER_ROW), :] = acc[:, sv]
    rows_bf_ref[:, KV_W:] = acc[:, KV_W:].astype(BF16)


def _kv_kernel(x_ref, w_ref, gk_ref, cos_ref, sin_ref, rows_ref, rows_bf_ref):
    acc = jnp.dot(x_ref[...], w_ref[...], preferred_element_type=F32)
    _kv_epilogue(acc, gk_ref, cos_ref, sin_ref, rows_ref, rows_bf_ref)


def _kv_proj(xn, w, gk, cos, sin, tm):
    m, k = xn.shape
    nt = cos.shape[0] // tm
    return pl.pallas_call(
        _kv_kernel,
        grid=(m // tm,),
        in_specs=[
            pl.BlockSpec((tm, k), lambda i: (i, 0)),
            pl.BlockSpec((k, ROW_W), lambda i: (0, 0)),
            pl.BlockSpec((1, HEAD_DIM), lambda i: (0, 0)),
            pl.BlockSpec((tm, HEAD_DIM), lambda i: (i % nt, 0)),
            pl.BlockSpec((tm, HEAD_DIM), lambda i: (i % nt, 0)),
        ],
        out_specs=[pl.BlockSpec((tm * HEADS_PER_ROW, HEAD_DIM), lambda i: (i, 0)),
                   pl.BlockSpec((tm, ROW_W), lambda i: (i, 0))],
        out_shape=[jax.ShapeDtypeStruct((m * HEADS_PER_ROW, HEAD_DIM), F32), jax.ShapeDtypeStruct((m, ROW_W), BF16)],
        compiler_params=_params("parallel"),
        name="in_proj_kv",
    )(xn, w, gk.reshape(1, HEAD_DIM), cos, sin)


def _pe_bias_kernel(pe_ref, w1_ref, b1_ref, o_ref):
    for kv in range(2):
        acc = jnp.zeros((SUBLANES, HEAD_DIM), F32)
        for r in range(CMP_BLOCK):
            row = jnp.broadcast_to(pe_ref[kv, r : r + 1, :], (SUBLANES, HEAD_DIM)).astype(BF16)
            acc = acc + jnp.dot(
                row, w1_ref[kv, r * HEAD_DIM : (r + 1) * HEAD_DIM, :].astype(BF16), preferred_element_type=F32
            )
        o_ref[kv] = acc[0:1, :] + b1_ref[kv]


def _pe_bias(pe, w1, b1):
    hid = w1.shape[-1]
    return pl.pallas_call(
        _pe_bias_kernel,
        out_shape=jax.ShapeDtypeStruct((2, 1, hid), F32),
        compiler_params=pltpu.CompilerParams(vmem_limit_bytes=VMEM_LIMIT),
        name="cmp_pe_bias",
    )(pe, w1, b1.reshape(2, 1, hid))


def _compress_kernel(pt_ref, *refs, n_pg):
    del pt_ref
    page_refs = refs[:n_pg]
    w1_ref, bias_ref, w2_ref, b2_ref, gk_ref, kc_ref, vc_ref, carry_ref, out_scr = refs[n_pg:]
    sub_per_page = PAGE_SIZE // CMP_STRIDE
    nsub = n_pg * sub_per_page
    rows = nsub * HEADS_PER_ROW
    page_rows = sub_per_page * HEADS_PER_ROW

    @pl.when(pl.program_id(1) == 0)
    def _():
        carry_ref[...] = jnp.zeros_like(carry_ref)

    half = rows // 2
    hid2 = w1_ref.shape[1] // 2
    hid_w = hid2 // 2
    top = lax.broadcasted_iota(jnp.int32, (nsub // 2, HEADS_PER_ROW, 2 * HEAD_DIM), 1) < N_KV_HEADS
    acc = [None, None]
    for pp in range(0, CMP_STRIDE, 2):
        pieces = []
        for pg in range(n_pg):
            a = page_refs[pg][0, pl.ds(pp, sub_per_page, stride=CMP_STRIDE), :, :].reshape(page_rows, HEAD_DIM)
            b = page_refs[pg][0, pl.ds(pp + 1, sub_per_page, stride=CMP_STRIDE), :, :].reshape(page_rows, HEAD_DIM)
            pieces.append(jnp.concatenate([a, b], axis=1))
        x3 = jnp.concatenate(pieces, axis=0).reshape(nsub // 2, 2 * HEADS_PER_ROW, 2 * HEAD_DIM)
        even = x3[:, :HEADS_PER_ROW]
        odd = x3[:, HEADS_PER_ROW:]
        xs = (jnp.where(top, even, pltpu.roll(odd, N_KV_HEADS, 1)),
              jnp.where(top, pltpu.roll(even, N_KV_HEADS, 1), odd))
        w1 = w1_ref[pp * HEAD_DIM : (pp + 2) * HEAD_DIM, :]
        for kv in range(2):
            d = jnp.dot(xs[kv].reshape(half, 2 * HEAD_DIM).astype(BF16), w1[:, kv * hid2 : (kv + 1) * hid2],
                        preferred_element_type=F32)
            acc[kv] = d if acc[kv] is None else acc[kv] + d
    first_rows = lax.broadcasted_iota(jnp.int32, (half, hid_w), 0) < N_KV_HEADS
    pad = jnp.zeros((half - SUBLANES, hid_w), F32)
    for kv in range(2):
        p_first = acc[kv][:, :hid_w]
        p_second = acc[kv][:, hid_w:]
        rolled = pltpu.roll(p_first, N_KV_HEADS, 0)
        shifted = jnp.where(first_rows, jnp.concatenate([carry_ref[kv], pad], axis=0), rolled)
        carry_ref[kv] = rolled[0:SUBLANES, :]
        hid = shifted + p_second + bias_ref[kv]
        hid = hid * _sigmoid(hid)
        out = jnp.dot(hid.astype(BF16), w2_ref[:, kv * HEAD_DIM : (kv + 1) * HEAD_DIM],
                      preferred_element_type=F32) + b2_ref[kv]
        if kv == 0:
            out = _rms(out) * gk_ref[...]
        out_scr[kv] = out
        dst = kc_ref if kv == 0 else vc_ref
        for h in range(N_KV_HEADS):
            dst[0, h, :, :] = out_scr[kv, pl.ds(h, nsub, stride=N_KV_HEADS), :].astype(BF16)


def _compress(pages, page_table, w1all, bias, w2all, b2, gk):
    n_seq, n_pp = page_table.shape
    n_pg = max(d for d in range(1, MAX_PAGES_PER_STEP + 1) if n_pp % d == 0)
    steps = n_pp // n_pg
    nsub = n_pg * (PAGE_SIZE // CMP_STRIDE)
    n_slots = steps * nsub
    hid = bias.shape[-1]

    def page_spec(k):
        return pl.BlockSpec((1, PAGE_SIZE, HEADS_PER_ROW, HEAD_DIM),
                            lambda b, s, pt: (pt[b * n_pp + s * n_pg + k], 0, 0, 0))

    const2 = lambda b, s, pt: (0, 0)
    const3 = lambda b, s, pt: (0, 0, 0)
    out_spec = pl.BlockSpec((1, N_KV_HEADS, nsub, HEAD_DIM), lambda b, s, pt: (b, 0, s, 0))
    grid_spec = pltpu.PrefetchScalarGridSpec(
        num_scalar_prefetch=1,
        grid=(n_seq, steps),
        in_specs=[page_spec(k) for k in range(n_pg)]
        + [
            pl.BlockSpec(w1all.shape, const2),
            pl.BlockSpec((2, 1, hid), const3),
            pl.BlockSpec(w2all.shape, const2),
            pl.BlockSpec((2, 1, HEAD_DIM), const3),
            pl.BlockSpec((1, HEAD_DIM), const2),
        ],
        out_specs=[out_spec, out_spec],
        scratch_shapes=[pltpu.VMEM((2, SUBLANES, hid), F32), pltpu.VMEM((2, nsub * N_KV_HEADS, HEAD_DIM), F32)],
    )
    shape = jax.ShapeDtypeStruct((n_seq, N_KV_HEADS, n_slots, HEAD_DIM), BF16)
    return pl.pallas_call(
        functools.partial(_compress_kernel, n_pg=n_pg),
        grid_spec=grid_spec,
        out_shape=[shape, shape],
        compiler_params=_params("parallel", "arbitrary"),
        name="compress",
    )(page_table.reshape(-1), *([pages] * n_pg), w1all, bias, w2all, b2.reshape(2, 1, HEAD_DIM), gk.reshape(1, HEAD_DIM))


def _nsa_prompt_kernel(
    q_ref, qr_ref, gate_ref, kc_ref, vc_ref, ks_ref, vs_ref, kw_ref, vw_ref, onehot_ref,
    o_ref, osel_scr, s_scr, p_scr, m_scr, owin_scr, sw_scr, pw_scr, mw_scr,
    *, tq, t_len, nb, n_sel, n_slots, kt):
    i = pl.program_id(2)
    q0 = i * tq
    rows = GROUP * tq
    scale = HEAD_DIM ** -0.5
    heads = [slice(g * HEAD_DIM, (g + 1) * HEAD_DIM) for g in range(GROUP)]
    q = jnp.concatenate([q_ref[:, sl] for sl in heads], axis=0)
    qr = jnp.concatenate([qr_ref[:, sl] for sl in heads], axis=0)

    def attend(lhs, rhs_chunk, v_ref, k0, n_keys, mask_chunk, s_buf, p_buf, m_buf):
        n_chunks = -(-n_keys // SCORE_CHUNK)
        for c in range(n_chunks):
            c0 = c * SCORE_CHUNK
            w = min(SCORE_CHUNK, n_keys - c0)
            s = lax.dot_general(lhs, rhs_chunk(c0, w), NT_DIMS, preferred_element_type=F32)
            s = mask_chunk(c0, w, s.reshape(GROUP, tq, w)).reshape(rows, w)
            s_buf[:, c0 : c0 + w] = s
            pm = s[:, 0:LANES]
            for j0 in range(LANES, w, LANES):
                pm = jnp.maximum(pm, s[:, j0 : j0 + LANES])
            m_buf[:, c * LANES : (c + 1) * LANES] = pm
        coef = np.float32(scale * np.log2(np.e))
        outs = []
        for g in range(GROUP):
            sums = []
            for r0 in range(g * tq, (g + 1) * tq, ROW_TILE):
                r = slice(r0, r0 + ROW_TILE)
                m_acc = m_buf[r, 0:LANES]
                for c in range(1, n_chunks):
                    m_acc = jnp.maximum(m_acc, m_buf[r, c * LANES : (c + 1) * LANES])
                m_b = jnp.broadcast_to(jnp.max(m_acc, axis=-1, keepdims=True), (ROW_TILE, LANES))
                l_acc = jnp.zeros((ROW_TILE, LANES), F32)
                for c0 in range(0, n_keys, LANES):
                    p = jnp.exp2((s_buf[r, c0 : c0 + LANES] - m_b) * coef)
                    l_acc = l_acc + p
                    p_buf[r, c0 : c0 + LANES] = p.astype(BF16)
                sums.append(jnp.sum(l_acc, axis=-1, keepdims=True))
            rg = slice(g * tq, (g + 1) * tq)
            pv = jnp.dot(p_buf[rg, 0:n_keys], v_ref[0, pl.ds(k0, n_keys), :], preferred_element_type=F32)
            outs.append(pv / jnp.concatenate(sums, axis=0))
        return jnp.concatenate(outs, axis=0)

    s_t = lax.dot_general(kc_ref[0, 0], q, NT_DIMS, preferred_element_type=F32) * scale
    slot = lax.broadcasted_iota(jnp.int32, (n_slots, rows), 0)
    tok = q0 + (lax.broadcasted_iota(jnp.int32, (n_slots, rows), 1) & (tq - 1))
    valid = (slot >= 1) & (slot * CMP_STRIDE + CMP_STRIDE <= tok + 1)
    s_m = jnp.where(valid, s_t, NEG_INF)
    e = jnp.exp(s_m - jnp.max(s_m, axis=0, keepdims=True))
    p_t = jnp.where(valid, e / jnp.sum(e, axis=0, keepdims=True), 0.0)
    o_cmp = jnp.dot(p_t.T.astype(BF16), vc_ref[0, 0], preferred_element_type=F32)

    p_sum = p_t[:, 0:tq]
    for g in range(1, GROUP):
        p_sum = p_sum + p_t[:, g * tq : (g + 1) * tq]
    ratio = SEL_BLOCK // CMP_STRIDE
    jj = lax.broadcasted_iota(jnp.int32, (nb, n_slots), 0)
    mm = lax.broadcasted_iota(jnp.int32, (nb, n_slots), 1)
    overlap = jnp.where((mm >= ratio * jj) & (mm <= ratio * jj + ratio), 1.0, 0.0)
    imp = jnp.dot(overlap, p_sum, precision=lax.Precision.HIGHEST, preferred_element_type=F32)
    jb = lax.broadcasted_iota(jnp.int32, (nb, tq), 0)
    pos = q0 + lax.broadcasted_iota(jnp.int32, (nb, tq), 1)
    cur = lax.shift_right_logical(pos, SEL_SHIFT)
    causal = jb * SEL_BLOCK <= pos
    forced = (jb == 0) | (jb == cur) | (jb == cur - 1)
    score = jnp.where(causal, jnp.where(forced, SEL_FORCE, imp), -1.0)
    cnts = [jnp.zeros((nb, tq), F32) for _ in range(4)]
    for i2 in range(nb):
        r = score[i2 : i2 + 1, :]
        cnts[i2 % 4] = cnts[i2 % 4] + jnp.where(r > score, 1.0, jnp.where((r == score) & (jb > i2), 1.0, 0.0))
    cnt = (cnts[0] + cnts[1]) + (cnts[2] + cnts[3])
    sel = jnp.where((cnt < n_sel) & causal, 1.0, 0.0)
    if nb < LANES:
        sel = jnp.concatenate([sel, jnp.zeros((LANES - nb, tq), F32)], axis=0)
    sel_q = ((sel.T - 1.0) * MASK_BIG).astype(BF16)
    lhs_sel = jnp.concatenate([qr, jnp.concatenate([sel_q] * GROUP, axis=0)], axis=1)

    def sel_rhs(c0, w):
        return jnp.concatenate([ks_ref[0, c0 : c0 + w, :], onehot_ref[c0 : c0 + w, :]], axis=1)

    def sel_mask(c0, w, s, first_diag_key):
        if c0 + w <= first_diag_key:
            return s
        kpos = c0 + lax.broadcasted_iota(jnp.int32, (tq, w), 1)
        qpos = q0 + lax.broadcasted_iota(jnp.int32, (tq, w), 0)
        return jnp.where((kpos <= qpos)[None], s, -MASK_BIG)

    lw = WINDOW + tq
    kw0 = pl.multiple_of(jnp.maximum(q0 - WINDOW, 0), tq)

    def window_mask(c0, w, s):
        diff = (q0 + lax.broadcasted_iota(jnp.int32, (tq, w), 0)) - (
            kw0 + c0 + lax.broadcasted_iota(jnp.int32, (tq, w), 1))
        return s + jnp.where((diff >= 0) & (diff < WINDOW), 0.0, NEG_INF)[None]

    owin_scr[...] = attend(qr, lambda c0, w: kw_ref[0, pl.ds(kw0 + c0, w), :], vw_ref, kw0, lw, window_mask,
                           sw_scr, pw_scr, mw_scr)

    n_tiles = (q0 + tq + kt - 1) // kt
    for nt in range(1, t_len // kt + 1):

        @pl.when(n_tiles == nt)
        def _(nt=nt):
            osel_scr[...] = attend(lhs_sel, sel_rhs, vs_ref, 0, nt * kt,
                                   functools.partial(sel_mask, first_diag_key=(nt - 1) * kt), s_scr, p_scr, m_scr)

    o_sel = osel_scr[...]
    o_win = owin_scr[...]
    gt = gate_ref[...]
    outs = []
    for g in range(GROUP):
        sl = slice(g * tq, (g + 1) * tq)
        outs.append(
            gt[:, g : g + 1] * o_cmp[sl]
            + gt[:, GROUP + g : GROUP + g + 1] * o_sel[sl]
            + gt[:, 2 * GROUP + g : 2 * GROUP + g + 1] * o_win[sl]
        )
    o_ref[...] = jnp.concatenate(outs, axis=1).astype(o_ref.dtype)


def _nsa_prompt(q_bf, qr_bf, gates, kc, vc, sel_bf, win_bf, n_b, t_len):
    tq = 128
    kt = min(512, t_len)
    nb = t_len // SEL_BLOCK
    n_slots = kc.shape[2]
    n_sel = min(N_SEL, nb)
    assert t_len % kt == 0 and t_len >= WINDOW + tq and nb <= LANES and n_slots == t_len // CMP_STRIDE
    ntq = t_len // tq
    m = n_b * t_len
    rows = GROUP * tq
    lw = WINDOW + tq
    onehot = (np.arange(t_len)[:, None] // SEL_BLOCK == np.arange(LANES)[None, :]).astype(np.float32)
    sel3 = sel_bf.reshape(n_b, t_len, ROW_W)
    win3 = win_bf.reshape(n_b, t_len, ROW_W)
    kspec = pl.BlockSpec((1, t_len, HEAD_DIM), lambda b, h, i: (b, 0, h))
    vspec = pl.BlockSpec((1, t_len, HEAD_DIM), lambda b, h, i: (b, 0, N_KV_HEADS + h))
    cspec = pl.BlockSpec((1, 1, n_slots, HEAD_DIM), lambda b, h, i: (b, h, 0, 0))
    qspec = pl.BlockSpec((tq, GROUP * HEAD_DIM), lambda b, h, i: (b * ntq + i, h))
    kern = functools.partial(
        _nsa_prompt_kernel, tq=tq, t_len=t_len, nb=nb, n_sel=n_sel, n_slots=n_slots, kt=kt)
    return pl.pallas_call(
        kern,
        grid=(n_b, N_KV_HEADS, ntq),
        in_specs=[
            qspec,
            qspec,
            pl.BlockSpec((tq, LANES), lambda b, h, i: (b * ntq + i, h)),
            cspec,
            cspec,
            kspec,
            vspec,
            kspec,
            vspec,
            pl.BlockSpec((t_len, LANES), lambda b, h, i: (0, 0)),
        ],
        out_specs=qspec,
        out_shape=jax.ShapeDtypeStruct((m, Q_W), BF16),
        scratch_shapes=[pltpu.VMEM((rows, HEAD_DIM), F32),
                        pltpu.VMEM((rows, t_len), F32), pltpu.VMEM((rows, t_len), BF16),
                        pltpu.VMEM((rows, -(-t_len // SCORE_CHUNK) * LANES), F32),
                        pltpu.VMEM((rows, HEAD_DIM), F32),
                        pltpu.VMEM((rows, lw), F32), pltpu.VMEM((rows, lw), BF16),
                        pltpu.VMEM((rows, -(-lw // SCORE_CHUNK) * LANES), F32)],
        compiler_params=_params("parallel", "parallel", "arbitrary"),
        name="nsa_prompt",
    )(q_bf, qr_bf, gates, kc, vc, sel3, sel3, win3, win3, jnp.asarray(onehot, BF16))


def _layer_norm(y, g, b):
    yc = y - jnp.mean(y, axis=-1, keepdims=True)
    var = jnp.mean(yc * yc, axis=-1, keepdims=True)
    return yc * lax.rsqrt(var + EPS) * g + b


def _glu(z):
    c = z.shape[1] // 2
    return z[:, :c] * _sigmoid(z[:, c:])


def _conv_prompt_kernel(z_ref, halo_ref, wdw_ref, bdw_ref, gln_ref, bln_ref, wout_ref, o_ref, wattn_ref,
                        gm_a_ref, gm_c_ref, mixed_ref, tail_ref, ufull, yconv, yattn, *, tc, width):
    i = pl.program_id(1)
    u = _glu(z_ref[...])
    uh = jnp.where(i == 0, 0.0, _glu(halo_ref[...]))
    ufull[0:CONV_HALO, :] = uh
    ufull[CONV_HALO:, :] = u
    tail_ref[0] = u[tc - CONV_HALO :, :]
    ch = u.shape[1]
    rc, cc = 64, 256
    off = CONV_HALO - (width - 1)
    d_out = wattn_ref.shape[1]
    n_conv_chunks = (tc // rc) * (ch // cc)
    aw = d_out // n_conv_chunks * 2
    step = 0
    acc = None
    kw = 2 * LANES
    for r0 in range(0, tc, rc):
        for c0 in range(0, ch, cc):
            if step % 2 == 0:
                a0 = step // 2 * aw
                o_first = o_ref[:, 0:kw]
                if acc is not None:
                    words = pltpu.bitcast(acc[0:2 * SUBLANES, 0:kw], jnp.uint32)
                    zero = lax.shift_right_logical(lax.shift_right_logical(words, jnp.uint32(16)), jnp.uint32(16))
                    zero = zero.astype(F32).astype(BF16)
                    o_first = (o_first.reshape(tc // (2 * SUBLANES), 2 * SUBLANES, kw) + zero[None]).reshape(tc, kw)
                lhs = jnp.concatenate([o_first, o_ref[:, kw:]], axis=1)
                yattn[:, a0 : a0 + aw] = jnp.dot(lhs, wattn_ref[:, a0 : a0 + aw], preferred_element_type=F32)
            step += 1
            acc = jnp.zeros((rc, cc), F32)
            span_all = rc + CONV_HALO
            u_rows = ufull[r0 : r0 + span_all, c0 : c0 + cc]
            for b in range(min(SUBLANES, width)):
                n_a = (width - 1 - b) // SUBLANES + 1
                ub = pltpu.roll(u_rows, (span_all - off - b) % span_all, 0)
                for a in range(n_a):
                    k = SUBLANES * a + b
                    acc = acc + wdw_ref[k : k + 1, c0 : c0 + cc] * ub[SUBLANES * a : SUBLANES * a + rc, :]
            yconv[r0 : r0 + rc, c0 : c0 + cc] = acc + bdw_ref[:, c0 : c0 + cc]
    y = _layer_norm(yconv[...], gln_ref[...], bln_ref[...])
    y = y * _sigmoid(y)
    y_conv = jnp.dot(y.astype(BF16), wout_ref[...], preferred_element_type=F32)
    mixed_ref[...] = (_sigmoid(gm_a_ref[...]) * yattn[...] + _sigmoid(gm_c_ref[...]) * y_conv).astype(mixed_ref.dtype)


def _conv_mix_prompt(zmain, o, w_dw, b_dw, g_ln, b_ln, w_out, w_attn, n_b, t_len):
    tc = 256
    width, ch = w_dw.shape
    d_out = w_out.shape[1]
    ntc = t_len // tc
    hb = tc // CONV_HALO
    m = n_b * t_len
    glu_blk = 0
    gm_blk = (zmain.shape[1] - 2 * d_out) // d_out
    row1 = lambda b, i: (0, 0)
    tile = lambda blk: pl.BlockSpec((tc, d_out), lambda b, i: (b * ntc + i, blk))
    kern = functools.partial(_conv_prompt_kernel, tc=tc, width=width)
    return pl.pallas_call(
        kern,
        grid=(n_b, ntc),
        in_specs=[
            pl.BlockSpec((tc, 2 * ch), lambda b, i: (b * ntc + i, glu_blk)),
            pl.BlockSpec((CONV_HALO, 2 * ch), lambda b, i: (jnp.maximum((b * ntc + i) * hb - 1, 0), glu_blk)),
            pl.BlockSpec((width, ch), row1),
            pl.BlockSpec((1, ch), row1),
            pl.BlockSpec((1, ch), row1),
            pl.BlockSpec((1, ch), row1),
            pl.BlockSpec((ch, d_out), row1),
            pl.BlockSpec((tc, o.shape[1]), lambda b, i: (b * ntc + i, 0)),
            pl.BlockSpec(w_attn.shape, row1),
            tile(gm_blk),
            tile(gm_blk + 1),
        ],
        out_specs=[
            tile(0),
            pl.BlockSpec((1, CONV_HALO, ch), lambda b, i: (b, 0, 0)),
        ],
        out_shape=[jax.ShapeDtypeStruct((m, d_out), BF16), jax.ShapeDtypeStruct((n_b, CONV_HALO, ch), F32)],
        scratch_shapes=[pltpu.VMEM((tc + CONV_HALO, ch), F32), pltpu.VMEM((tc, ch), F32), pltpu.VMEM((tc, d_out), F32)],
        compiler_params=_params("parallel", "arbitrary"),
        name="conv_mix_prompt",
    )(zmain, zmain, w_dw, b_dw.reshape(1, ch), g_ln.reshape(1, ch), b_ln.reshape(1, ch), w_out, o, w_attn,
      zmain, zmain)


def _conv_sample_kernel(z_ref, st_ref, wdw_ref, bdw_ref, gln_ref, bln_ref, wout_ref, y_ref, u_ref, *, width):
    u = _glu(z_ref[...])
    u_ref[...] = u
    y = jnp.sum(st_ref[...] * wdw_ref[0 : width - 1, :][None], axis=1) + wdw_ref[width - 1 : width, :] * u
    y = _layer_norm(y + bdw_ref[...], gln_ref[...], bln_ref[...])
    y = y * _sigmoid(y)
    y_ref[...] = jnp.dot(y.astype(BF16), wout_ref[...], preferred_element_type=F32)


def _conv_sample(zmain, state, w_dw, b_dw, g_ln, b_ln, w_out):
    width, ch = w_dw.shape
    n_b = state.shape[0]
    d_out = w_out.shape[1]
    glu_blk = 0
    c2 = lambda i: (0, 0)
    return pl.pallas_call(
        functools.partial(_conv_sample_kernel, width=width),
        grid=(1,),
        in_specs=[
            pl.BlockSpec((n_b, 2 * ch), lambda i: (0, glu_blk)),
            pl.BlockSpec((n_b, width - 1, ch), lambda i: (0, 0, 0)),
            pl.BlockSpec((width, ch), c2),
            pl.BlockSpec((1, ch), c2),
            pl.BlockSpec((1, ch), c2),
            pl.BlockSpec((1, ch), c2),
            pl.BlockSpec((ch, d_out), c2),
        ],
        out_specs=[pl.BlockSpec((n_b, d_out), c2), pl.BlockSpec((n_b, ch), c2)],
        out_shape=[jax.ShapeDtypeStruct((n_b, d_out), F32), jax.ShapeDtypeStruct((n_b, ch), F32)],
        compiler_params=_params("arbitrary"),
        name="conv_sample",
    )(zmain, state, w_dw, b_dw.reshape(1, ch), g_ln.reshape(1, ch), b_ln.reshape(1, ch), w_out)


def _attn_mix_kernel(o_ref, w_ref, gm_a_ref, gm_c_ref, yc_ref, mixed_ref):
    ya = jnp.dot(o_ref[...], w_ref[...], preferred_element_type=F32)
    mixed_ref[...] = (_sigmoid(gm_a_ref[...]) * ya + _sigmoid(gm_c_ref[...]) * yc_ref[...]).astype(mixed_ref.dtype)


def _attn_mix(o, w_attn, zmain, y_conv, tm):
    m, k = o.shape
    d = w_attn.shape[1]
    gm_blk = (zmain.shape[1] - 2 * d) // d
    return pl.pallas_call(
        _attn_mix_kernel,
        grid=(m // tm,),
        in_specs=[
            pl.BlockSpec((tm, k), lambda i: (i, 0)),
            pl.BlockSpec((k, d), lambda i: (0, 0)),
            pl.BlockSpec((tm, d), lambda i: (i, gm_blk)),
            pl.BlockSpec((tm, d), lambda i: (i, gm_blk + 1)),
            pl.BlockSpec((tm, d), lambda i: (i, 0)),
        ],
        out_specs=pl.BlockSpec((tm, d), lambda i: (i, 0)),
        out_shape=jax.ShapeDtypeStruct((m, d), BF16),
        compiler_params=_params("parallel"),
        name="attn_out_mix",
    )(o, w_attn, zmain, zmain, y_conv)


def _out_proj_kernel(mixed_ref, w_ref, x_ref, h_ref):
    h_ref[...] = x_ref[...] + jnp.dot(mixed_ref[...], w_ref[...], preferred_element_type=F32)


def _out_proj(mixed, w_out, x, tm):
    m, k = mixed.shape
    d = w_out.shape[1]
    return pl.pallas_call(
        _out_proj_kernel,
        grid=(m // tm,),
        in_specs=[
            pl.BlockSpec((tm, k), lambda i: (i, 0)),
            pl.BlockSpec((k, d), lambda i: (0, 0)),
            pl.BlockSpec((tm, d), lambda i: (i, 0)),
        ],
        out_specs=pl.BlockSpec((tm, d), lambda i: (i, 0)),
        out_shape=jax.ShapeDtypeStruct((m, d), F32),
        compiler_params=_params("parallel"),
        name="out_proj",
    )(mixed, w_out, x)


def _gelu(x):
    return 0.5 * x * (1.0 + lax.erf(x * np.float32(np.sqrt(0.5))))


def _ffn_kernel(*refs, tm, tiles_per_seq, sample):
    if sample:
        h_ref, g_ref, wa_ref, wb_ref, wdw_ref, bdw_ref, wd_ref, s0_ref, s1_ref, out_ref, a_ref, hn, acc = refs
    else:
        h_ref, halo_ref, g_ref, wa_ref, wb_ref, wdw_ref, bdw_ref, wd_ref, out_ref, a_ref, hn, acc = refs
    i = pl.program_id(0)
    j = pl.program_id(1)
    lead = 0 if sample else FFN_HALO

    @pl.when(j == 0)
    def _():
        hn[lead:, :] = (_rms(h_ref[...]) * g_ref[...]).astype(BF16)
        if not sample:
            hn[0:lead, :] = (_rms(halo_ref[...]) * g_ref[...]).astype(BF16)
        acc[...] = jnp.zeros_like(acc)

    a_ext = jnp.dot(hn[...], wa_ref[...], preferred_element_type=F32)
    gate = jnp.dot(hn[lead:, :], wb_ref[...], preferred_element_type=F32)
    a0 = a_ext[lead:, :]
    if sample:
        a_m2 = s0_ref[...]
        a_m1 = s1_ref[...]
        a_ref[...] = a0
    else:
        first = (i % tiles_per_seq) == 0
        r = lax.broadcasted_iota(jnp.int32, a0.shape, 0)
        a_m2 = jnp.where(first & (r < 2), 0.0, a_ext[lead - 2 : lead - 2 + tm, :])
        a_m1 = jnp.where(first & (r < 1), 0.0, a_ext[lead - 1 : lead - 1 + tm, :])
        a_ref[0] = a0[tm - SUBLANES :, :]
    conv = wdw_ref[0:1, :] * a_m2 + wdw_ref[1:2, :] * a_m1 + wdw_ref[2:3, :] * a0 + bdw_ref[...]
    y = (_gelu(conv) * gate).astype(BF16)
    acc[...] += jnp.dot(y, wd_ref[...], preferred_element_type=F32)

    @pl.when(j == pl.num_programs(1) - 1)
    def _():
        out_ref[...] = h_ref[...] + acc[...]


def _conv_ffn(h, g, w_up_a, w_up_b, w_dw, b_dw, w_down, *, tm, t_len=None, state=None):
    m, d = h.shape
    dff = w_up_a.shape[1]
    tf = 512
    nf = dff // tf
    sample = state is not None
    assert w_dw.shape[0] == 3
    col = lambda i, j: (0, j)
    common = [
        pl.BlockSpec((1, d), lambda i, j: (0, 0)),
        pl.BlockSpec((d, tf), col),
        pl.BlockSpec((d, tf), col),
        pl.BlockSpec((3, tf), col),
        pl.BlockSpec((1, tf), col),
        pl.BlockSpec((tf, d), lambda i, j: (j, 0)),
    ]
    args = [g.reshape(1, d), w_up_a, w_up_b, w_dw, b_dw.reshape(1, dff), w_down]
    hspec = pl.BlockSpec((tm, d), lambda i, j: (i, 0))
    if sample:
        assert m == tm
        st = state.reshape(m, 2 * dff)
        in_specs = [hspec] + common + [pl.BlockSpec((tm, tf), col), pl.BlockSpec((tm, tf), lambda i, j: (0, nf + j))]
        args = [h] + args + [st, st]
        a_spec = pl.BlockSpec((tm, tf), col)
        a_shape = jax.ShapeDtypeStruct((m, dff), F32)
        tiles_per_seq = 1
        lead = 0
    else:
        tiles_per_seq = t_len // tm
        hb = tm // FFN_HALO
        in_specs = [hspec, pl.BlockSpec((FFN_HALO, d), lambda i, j: (jnp.maximum(i * hb - 1, 0), 0))] + common
        args = [h, h] + args
        a_spec = pl.BlockSpec((1, SUBLANES, tf), lambda i, j: (i, 0, j))
        a_shape = jax.ShapeDtypeStruct((m // tm, SUBLANES, dff), F32)
        lead = FFN_HALO
    kern = functools.partial(_ffn_kernel, tm=tm, tiles_per_seq=tiles_per_seq, sample=sample)
    return pl.pallas_call(
        kern,
        grid=(m // tm, nf),
        in_specs=in_specs,
        out_specs=[hspec, a_spec],
        out_shape=[jax.ShapeDtypeStruct((m, d), F32), a_shape],
        scratch_shapes=[pltpu.VMEM((tm + lead, d), BF16), pltpu.VMEM((tm, d), F32)],
        compiler_params=_params("parallel", "arbitrary"),
        name="conv_ffn",
    )(*args)


def _ple_kernel(h_ref, g_ref, wg_ref, pe_ref, wp_ref, o_ref):
    h = h_ref[...]
    hn = (_rms(h) * g_ref[...]).astype(BF16)
    gate = _sigmoid(jnp.dot(hn, wg_ref[...], preferred_element_type=F32))
    proj = jnp.dot(pe_ref[...], wp_ref[...], preferred_element_type=F32)
    o_ref[...] = h + gate * proj


def _ple(h, g, w_gate, pe, w_proj, tm):
    m, d = h.shape
    pd = pe.shape[1]
    return pl.pallas_call(
        _ple_kernel,
        grid=(m // tm,),
        in_specs=[
            pl.BlockSpec((tm, d), lambda i: (i, 0)),
            pl.BlockSpec((1, d), lambda i: (0, 0)),
            pl.BlockSpec((d, d), lambda i: (0, 0)),
            pl.BlockSpec((tm, pd), lambda i: (i, 0)),
            pl.BlockSpec((pd, d), lambda i: (0, 0)),
        ],
        out_specs=pl.BlockSpec((tm, d), lambda i: (i, 0)),
        out_shape=jax.ShapeDtypeStruct((m, d), F32),
        compiler_params=_params("parallel"),
        name="ple",
    )(h, g.reshape(1, d), w_gate, pe, w_proj)


def _cmp_sample_kernel(q_ref, kc_ref, vc_ref, ov_ref, score_ref, ocmp_ref, *, pos, nb, n_slots):
    scale = HEAD_DIM ** -0.5
    qn = q_ref[0]
    slot = lax.broadcasted_iota(jnp.int32, (GROUP, n_slots), 1)
    valid = (slot >= 1) & (slot * CMP_STRIDE + CMP_STRIDE <= pos + 1)
    sums = []
    for h in range(N_KV_HEADS):
        qh = qn[h * GROUP : (h + 1) * GROUP, :].astype(BF16)
        s = lax.dot_general(qh, kc_ref[0, h], NT_DIMS, preferred_element_type=F32) * scale
        e, l = _softmax_rows(jnp.where(valid, s, NEG_INF))
        p = jnp.where(valid, e / l, 0.0)
        ocmp_ref[0, h * GROUP : (h + 1) * GROUP, :] = jnp.dot(p.astype(BF16), vc_ref[0, h], preferred_element_type=F32)
        sums.append(jnp.sum(p, axis=0, keepdims=True))
    p_sum = jnp.concatenate(sums + [jnp.zeros((SUBLANES - N_KV_HEADS, n_slots), F32)], axis=0)
    imp = jnp.dot(p_sum, ov_ref[...], precision=lax.Precision.HIGHEST, preferred_element_type=F32)
    jb = lax.broadcasted_iota(jnp.int32, imp.shape, 1)
    cur = pos // SEL_BLOCK
    causal = (jb * SEL_BLOCK <= pos) & (jb < nb)
    forced = (jb == 0) | (jb == cur) | (jb == cur - 1)
    score_ref[0] = jnp.where(jb < nb, jnp.where(causal, jnp.where(forced, SEL_FORCE, imp), -1.0), -2.0)


def _cmp_sample(q3, kc, vc, pos, nb):
    n_b = q3.shape[0]
    n_slots = kc.shape[2]
    nbp = -(-nb // LANES) * LANES
    ratio = SEL_BLOCK // CMP_STRIDE
    mm = np.arange(n_slots)[:, None]
    jj = np.arange(nbp)[None, :]
    overlap = ((mm >= ratio * jj) & (mm <= ratio * jj + ratio) & (jj < nb)).astype(np.float32)
    qspec = pl.BlockSpec((1, N_HEADS, HEAD_DIM), lambda b: (b, 0, 0))
    cspec = pl.BlockSpec((1, N_KV_HEADS, n_slots, HEAD_DIM), lambda b: (b, 0, 0, 0))
    return pl.pallas_call(
        functools.partial(_cmp_sample_kernel, pos=pos, nb=nb, n_slots=n_slots),
        grid=(n_b,),
        in_specs=[qspec, cspec, cspec, pl.BlockSpec((n_slots, nbp), lambda b: (0, 0))],
        out_specs=[pl.BlockSpec((1, SUBLANES, nbp), lambda b: (b, 0, 0)), qspec],
        out_shape=[
            jax.ShapeDtypeStruct((n_b, SUBLANES, nbp), F32),
            jax.ShapeDtypeStruct((n_b, N_HEADS, HEAD_DIM), F32),
        ],
        compiler_params=_params("parallel"),
        name="cmp_sample",
    )(q3, kc, vc, jnp.asarray(overlap))


def _topk_kernel(score_ref, idx_ref, *, n_sel):
    sc = score_ref[...]
    lane = lax.broadcasted_iota(jnp.int32, sc.shape, 1)
    out_lane = lax.broadcasted_iota(jnp.int32, idx_ref.shape, 1)
    out = jnp.zeros(idx_ref.shape, jnp.int32)
    for it in range(n_sel):
        mx = jnp.max(sc, axis=-1, keepdims=True)
        idx = jnp.min(jnp.where(sc == mx, lane, sc.shape[1]), axis=-1, keepdims=True)
        out = jnp.where(out_lane == it, idx, out)
        sc = jnp.where(lane == idx, -3.0, sc)
    idx_ref[...] = out


def _topk(score, n_sel):
    rows = score.shape[0]
    return pl.pallas_call(
        functools.partial(_topk_kernel, n_sel=n_sel),
        out_shape=jax.ShapeDtypeStruct((rows, LANES), jnp.int32),
        compiler_params=pltpu.CompilerParams(vmem_limit_bytes=VMEM_LIMIT),
        name="topk_sample",
    )(score)


def _sel_sample_kernel(idx_ref, pt_ref, *refs, n_sel, n_past_blocks):
    del pt_ref
    blk_refs = refs[:n_sel]
    qr_ref, new_ref, o_ref = refs[n_sel:]
    b = pl.program_id(0)
    h = pl.program_id(1)
    scale = HEAD_DIM ** -0.5
    blk_rows = SEL_BLOCK * HEADS_PER_ROW
    qr = qr_ref[0, 0].astype(BF16)
    rows = jnp.concatenate([r[...] for r in blk_refs], axis=0).astype(BF16)
    s = lax.dot_general(qr, rows, NT_DIMS, preferred_element_type=F32) * scale
    lane = lax.broadcasted_iota(jnp.int32, s.shape, 1)
    slot = lane // blk_rows
    bias = jnp.where((lane & (HEADS_PER_ROW - 1)) == h, 0.0, NEG_INF)
    n_new = jnp.int32(0)
    for j in range(n_sel):
        is_new = idx_ref[(b * N_KV_HEADS + h) * n_sel + j] >= n_past_blocks
        bias = jnp.where(slot == j, jnp.where(is_new, NEG_INF, bias), bias)
        n_new = n_new + jnp.where(is_new, 1, 0)
    s = s + bias
    k_new = new_ref[0, pl.ds(h, 1), :].astype(BF16).astype(F32)
    v_new = new_ref[0, pl.ds(N_KV_HEADS + h, 1), :].astype(BF16).astype(F32)
    s_new = jnp.sum(qr.astype(F32) * k_new, axis=-1, keepdims=True) * scale
    s_new = jnp.where(n_new > 0, s_new, NEG_INF)
    m = jnp.maximum(jnp.max(s, axis=-1, keepdims=True), s_new)
    e = jnp.exp(s - m)
    e_new = jnp.exp(s_new - m)
    l = jnp.sum(e, axis=-1, keepdims=True) + e_new
    e_v = pltpu.roll(e, N_KV_HEADS, 1)
    o = jnp.dot(e_v.astype(BF16), rows, preferred_element_type=F32) + e_new.astype(BF16).astype(F32) * v_new
    o_ref[0, 0] = o / l


def _sel_sample(idx_flat, pt_flat, cache_rows, qr4, new_rows, n_sel, n_past_blocks, n_pp):
    n_b = qr4.shape[0]
    per_page = PAGE_SIZE // SEL_BLOCK
    blk_rows = SEL_BLOCK * HEADS_PER_ROW

    def bspec(j):
        def imap(b, h, idx, pt):
            blk_id = jnp.minimum(idx[(b * N_KV_HEADS + h) * n_sel + j], n_past_blocks - 1)
            return (pt[b * n_pp + blk_id // per_page] * per_page + blk_id % per_page, 0)
        return pl.BlockSpec((blk_rows, HEAD_DIM), imap)

    grid_spec = pltpu.PrefetchScalarGridSpec(
        num_scalar_prefetch=2,
        grid=(n_b, N_KV_HEADS),
        in_specs=[bspec(j) for j in range(n_sel)]
        + [
            pl.BlockSpec((1, 1, GROUP, HEAD_DIM), lambda b, h, idx, pt: (b, h, 0, 0)),
            pl.BlockSpec((1, HEADS_PER_ROW, HEAD_DIM), lambda b, h, idx, pt: (b, 0, 0)),
        ],
        out_specs=pl.BlockSpec((1, 1, GROUP, HEAD_DIM), lambda b, h, idx, pt: (b, h, 0, 0)),
    )
    return pl.pallas_call(
        functools.partial(_sel_sample_kernel, n_sel=n_sel, n_past_blocks=n_past_blocks),
        grid_spec=grid_spec,
        out_shape=jax.ShapeDtypeStruct((n_b, N_KV_HEADS, GROUP, HEAD_DIM), F32),
        compiler_params=_params("parallel", "arbitrary"),
        name="sel_sample",
    )(idx_flat, pt_flat, *([cache_rows] * n_sel), qr4, new_rows)


def _head_lane_mask(shape, kv_head_of_row):
    lane = lax.broadcasted_iota(jnp.int32, shape, 1)
    return (lane & (HEADS_PER_ROW - 1)) == kv_head_of_row


def _win_sample_kernel(win_ref, new_ref, qr_ref, ocmp_ref, osel_ref, g_ref, nw_ref, o_ref, *, wb):
    scale = HEAD_DIM ** -0.5
    keep = (wb - 1) * HEADS_PER_ROW
    nw_ref[0, 0:keep, :] = win_ref[0, HEADS_PER_ROW:, :]
    nw_ref[0, keep:, :] = new_ref[0]
    rows = nw_ref[0].astype(BF16)
    qr = qr_ref[0].astype(BF16)
    s = lax.dot_general(qr, rows, NT_DIMS, preferred_element_type=F32) * scale
    kv_head = lax.broadcasted_iota(jnp.int32, s.shape, 0) // GROUP
    s = jnp.where(_head_lane_mask(s.shape, kv_head), s, NEG_INF)
    e, l = _softmax_rows(s)
    e_v = pltpu.roll(e, N_KV_HEADS, 1)
    o_win = jnp.dot(e_v.astype(BF16), rows, preferred_element_type=F32) / l
    g = g_ref[0]
    o = g[:, 0:1] * ocmp_ref[0] + g[:, 1:2] * osel_ref[0] + g[:, 2:3] * o_win
    o_ref[0] = o.astype(o_ref.dtype)


def _win_sample(win_rows, new_rows, qr3, o_cmp, o_sel, g3):
    n_b, wr, _ = win_rows.shape
    wb = wr // HEADS_PER_ROW
    hspec = pl.BlockSpec((1, N_HEADS, HEAD_DIM), lambda b: (b, 0, 0))
    wspec = pl.BlockSpec((1, wr, HEAD_DIM), lambda b: (b, 0, 0))
    return pl.pallas_call(
        functools.partial(_win_sample_kernel, wb=wb),
        grid=(n_b,),
        in_specs=[
            wspec,
            pl.BlockSpec((1, HEADS_PER_ROW, HEAD_DIM), lambda b: (b, 0, 0)),
            hspec,
            hspec,
            hspec,
            pl.BlockSpec((1, N_HEADS, 3), lambda b: (b, 0, 0)),
        ],
        out_specs=[wspec, hspec],
        out_shape=[jax.ShapeDtypeStruct((n_b, wr, HEAD_DIM), F32), jax.ShapeDtypeStruct((n_b, N_HEADS, HEAD_DIM), BF16)],
        compiler_params=_params("parallel"),
        name="win_sample",
    )(win_rows, new_rows, qr3, o_cmp, o_sel, g3)


def _rope_tables(pos):
    half = HEAD_DIM // 2
    inv = ROPE_THETA ** (-jnp.arange(half, dtype=F32) / half)
    ang = pos.astype(F32)[:, None] * inv
    cos = jnp.cos(ang)
    sin = jnp.sin(ang)
    return jnp.concatenate([cos, cos], axis=-1), jnp.concatenate([-sin, sin], axis=-1)


def _prep_weights(lw):
    w_in = lw["w_in"]
    c = 0
    parts = {}
    conv_ch = lw["w_conv_dw"].shape[1]
    d_model = w_in.shape[0]
    for name, width in (("q", Q_W), ("cmp", ROW_W), ("sel", ROW_W), ("win", ROW_W), ("gnsa", 3 * N_HEADS),
                        ("glu", 2 * conv_ch), ("gm", 2 * d_model)):
        parts[name] = w_in[:, c : c + width]
        c += width
    wg = parts["gnsa"].reshape(d_model, 3, N_KV_HEADS, GROUP).transpose(0, 2, 1, 3).reshape(d_model, N_KV_HEADS, 3 * GROUP)
    wg = jnp.pad(wg, ((0, 0), (0, 0), (0, LANES - 3 * GROUP))).reshape(d_model, N_KV_HEADS * LANES)
    w1 = lw["w_cmp1"]
    hid = w1.shape[-1]
    r = CMP_BLOCK // CMP_STRIDE
    w1all = w1.reshape(2, r, CMP_STRIDE, HEAD_DIM, hid).transpose(2, 3, 0, 1, 4).reshape(CMP_STRIDE * HEAD_DIM, 2 * r * hid)
    dff = lw["w_up"].shape[1] // 2
    return {
        "w_q": parts["q"].astype(BF16),
        "w_main": jnp.concatenate([parts["glu"], parts["gm"]], axis=1).astype(BF16),
        "w_cmp_gate": jnp.concatenate([parts["cmp"], wg], axis=1).astype(BF16),
        "w_sel": parts["sel"].astype(BF16),
        "w_win": parts["win"].astype(BF16),
        "w1all": w1all.astype(BF16),
        "w2all": jnp.concatenate([lw["w_cmp2"][0], lw["w_cmp2"][1]], axis=1).astype(BF16),
        "w_attn_out": lw["w_attn_out"].astype(BF16),
        "w_conv_out": lw["w_conv_out"].astype(BF16),
        "w_out": lw["w_out"].astype(BF16),
        "w_up_a": lw["w_up"][:, :dff].astype(BF16),
        "w_up_b": lw["w_up"][:, dff:].astype(BF16),
        "w_down": lw["w_down"].astype(BF16),
        "w_ple_gate": lw["w_ple_gate"].astype(BF16),
        "w_ple_proj": lw["w_ple_proj"].astype(BF16),
    }


def _dense_tail(x, mixed, pe, lw, pw, *, tm, t_len=None, ffn_state=None):
    h1 = _out_proj(mixed, pw["w_out"], x, tm)
    h2, a_tail = _conv_ffn(h1, lw["g_ffn"], pw["w_up_a"], pw["w_up_b"], lw["w_ffn_dw"], lw["b_ffn_dw"], pw["w_down"],
                           tm=tm, t_len=t_len, state=ffn_state)
    h3 = _ple(h2, lw["g_ple"], pw["w_ple_gate"], pe.astype(BF16), pw["w_ple_proj"], tm)
    return h3, a_tail


def _layer_prompt(x3, pe3, lw, pw, bias_cmp):
    n_b, t_len, d = x3.shape
    m = n_b * t_len
    x = x3.reshape(m, d)
    cos, sin = _rope_tables(jnp.arange(t_len, dtype=jnp.int32))
    xn = _rmsnorm_bf16(x, lw["g_mix"], 512)
    zmain = _matmul(xn, pw["w_main"], 1024, 1024)
    rows_cmp, gates = _cmp_gate_proj(xn, pw["w_cmp_gate"], 512)
    rows_sel, sel_bf = _kv_proj(xn, pw["w_sel"], lw["g_k"][1], cos, sin, 512)
    rows_win, win_bf = _kv_proj(xn, pw["w_win"], lw["g_k"][2], cos, sin, 512)
    n_pp = t_len // PAGE_SIZE
    ident = jnp.arange(n_b * n_pp, dtype=jnp.int32).reshape(n_b, n_pp)
    kc, vc = _compress(rows_cmp.reshape(n_b * n_pp, PAGE_SIZE, HEADS_PER_ROW, HEAD_DIM), ident, pw["w1all"], bias_cmp,
                       pw["w2all"], lw["b_cmp2"], lw["g_k"][0])
    q_bf, qr_bf = _q_proj(xn, pw["w_q"], lw["g_q"], cos, sin, 512, BF16)
    o = _nsa_prompt(q_bf, qr_bf, gates, kc, vc, sel_bf, win_bf, n_b, t_len)
    mixed, u_tail = _conv_mix_prompt(zmain, o, lw["w_conv_dw"], lw["b_conv_dw"], lw["g_conv_ln"], lw["b_conv_ln"],
                                     pw["w_conv_out"], pw["w_attn_out"], n_b, t_len)
    h3, a_tail = _dense_tail(x, mixed, pe3.reshape(m, -1), lw, pw, tm=512, t_len=t_len)
    kvshape = (n_b, t_len, 2, N_KV_HEADS, HEAD_DIM)
    w_keep = min(WINDOW, t_len)
    conv_w = lw["w_conv_dw"].shape[0]
    state = (
        rows_cmp.reshape(kvshape),
        rows_sel.reshape(kvshape),
        rows_win.reshape(n_b, t_len * HEADS_PER_ROW, HEAD_DIM)[:, (t_len - w_keep) * HEADS_PER_ROW :].reshape(
            n_b, w_keep, 2, N_KV_HEADS, HEAD_DIM),
        u_tail[:, CONV_HALO - (conv_w - 1) :],
        a_tail.reshape(n_b, -1, SUBLANES, a_tail.shape[-1])[:, -1, SUBLANES - (lw["w_ffn_dw"].shape[0] - 1) :],
    )
    return h3.reshape(n_b, t_len, d), state


def _layer_sample(x3, pe3, lw, pw, bias_cmp, cache_cmp, cache_sel, cache_win, conv_state, ffn_state, page_table):
    n_b, t_new, d = x3.shape
    assert t_new == 1
    n_pp = page_table.shape[1]
    past_len = n_pp * PAGE_SIZE
    pos = past_len
    wb = cache_win.shape[1]
    assert wb == WINDOW and past_len % SEL_BLOCK == 0
    x = x3.reshape(n_b, d)
    cos, sin = _rope_tables(jnp.full((n_b,), pos, dtype=jnp.int32))
    xn = _rmsnorm_bf16(x, lw["g_mix"], n_b)
    zmain = _matmul(xn, pw["w_main"], n_b, 1024)
    rows_cmp, gates = _cmp_gate_proj(xn, pw["w_cmp_gate"], n_b)
    rows_sel, _ = _kv_proj(xn, pw["w_sel"], lw["g_k"][1], cos, sin, n_b)
    rows_win, _ = _kv_proj(xn, pw["w_win"], lw["g_k"][2], cos, sin, n_b)
    n_pool = cache_cmp.shape[0]
    kc, vc = _compress(cache_cmp.reshape(n_pool, PAGE_SIZE, HEADS_PER_ROW, HEAD_DIM), page_table, pw["w1all"], bias_cmp,
                       pw["w2all"], lw["b_cmp2"], lw["g_k"][0])
    n_past_blocks = past_len // SEL_BLOCK
    nb = n_past_blocks + 1
    n_sel = min(N_SEL, nb)
    q_s, qr_s = _q_proj(xn, pw["w_q"], lw["g_q"], cos, sin, n_b, F32)
    q3 = q_s.reshape(n_b, N_HEADS, HEAD_DIM)
    qr3 = qr_s.reshape(n_b, N_HEADS, HEAD_DIM)
    score, o_cmp = _cmp_sample(q3, kc, vc, pos, nb)
    idx = _topk(score.reshape(n_b * SUBLANES, -1), n_sel)
    idx = idx.reshape(n_b, SUBLANES, LANES)[:, :N_KV_HEADS, :n_sel]
    o_sel = _sel_sample(idx.reshape(-1), page_table.reshape(-1),
                        cache_sel.reshape(n_pool * PAGE_SIZE * HEADS_PER_ROW, HEAD_DIM),
                        qr3.reshape(n_b, N_KV_HEADS, GROUP, HEAD_DIM), rows_sel.reshape(n_b, HEADS_PER_ROW, HEAD_DIM),
                        n_sel, n_past_blocks, n_pp)
    g3 = gates.reshape(n_b, N_KV_HEADS, LANES)[:, :, : 3 * GROUP].reshape(n_b, N_KV_HEADS, 3, GROUP)
    g3 = g3.transpose(0, 1, 3, 2).reshape(n_b, N_HEADS, 3)
    new_win, o = _win_sample(cache_win.reshape(n_b, wb * HEADS_PER_ROW, HEAD_DIM),
                             rows_win.reshape(n_b, HEADS_PER_ROW, HEAD_DIM), qr3, o_cmp,
                             o_sel.reshape(n_b, N_HEADS, HEAD_DIM), g3)
    y_conv, u = _conv_sample(zmain, conv_state, lw["w_conv_dw"], lw["b_conv_dw"], lw["g_conv_ln"], lw["b_conv_ln"],
                             pw["w_conv_out"])
    mixed = _attn_mix(o.reshape(n_b, Q_W), pw["w_attn_out"], zmain, y_conv, n_b)
    h3, a_new = _dense_tail(x, mixed, pe3.reshape(n_b, -1), lw, pw, tm=n_b, ffn_state=ffn_state)
    kvshape = (n_b, 1, 2, N_KV_HEADS, HEAD_DIM)
    state = (
        rows_cmp.reshape(kvshape),
        rows_sel.reshape(kvshape),
        new_win.reshape(n_b, wb, 2, N_KV_HEADS, HEAD_DIM),
        jnp.concatenate([conv_state[:, 1:], u[:, None, :]], axis=1),
        jnp.concatenate([ffn_state[:, 1:], a_new[:, None, :]], axis=1),
    )
    return h3.reshape(n_b, 1, d), state


_LAYER_WEIGHTS = ("g_mix", "w_in", "g_q", "g_k", "w_cmp1", "b_cmp1", "w_cmp2", "b_cmp2", "pe_cmp", "w_attn_out",
                  "w_conv_dw", "b_conv_dw", "g_conv_ln", "b_conv_ln", "w_conv_out", "w_out", "g_ffn", "w_up",
                  "w_ffn_dw", "b_ffn_dw", "w_down", "g_ple", "w_ple_gate", "w_ple_proj")


def kernel(x_prompt, x_sample, p_prompt, p_sample, cache_cmp_kv, cache_sel_kv, cache_win_kv, state_conv, state_ffn_conv, page_table, g_mix, w_in, g_q, g_k, w_cmp1, b_cmp1, w_cmp2, b_cmp2, pe_cmp, w_attn_out, w_conv_dw, b_conv_dw, g_conv_ln, b_conv_ln, w_conv_out, w_out, g_ffn, w_up, w_ffn_dw, b_ffn_dw, w_down, g_ple, w_ple_gate, w_ple_proj):
    stacked = dict(zip(_LAYER_WEIGHTS, (g_mix, w_in, g_q, g_k, w_cmp1, b_cmp1, w_cmp2, b_cmp2, pe_cmp, w_attn_out,
                                        w_conv_dw, b_conv_dw, g_conv_ln, b_conv_ln, w_conv_out, w_out, g_ffn, w_up,
                                        w_ffn_dw, b_ffn_dw, w_down, g_ple, w_ple_gate, w_ple_proj)))
    depth = w_in.shape[0]
    hp, hs = x_prompt, x_sample
    states_p, states_s = [], []
    for i in range(depth):
        lw = {k: v[i] for k, v in stacked.items()}
        pw = _prep_weights(lw)
        bias_cmp = _pe_bias(lw["pe_cmp"], lw["w_cmp1"], lw["b_cmp1"])
        hp, st_p = _layer_prompt(hp, p_prompt[i], lw, pw, bias_cmp)
        hs, st_s = _layer_sample(hs, p_sample[i], lw, pw, bias_cmp, cache_cmp_kv[i], cache_sel_kv[i], cache_win_kv[i],
                                 state_conv[i], state_ffn_conv[i], page_table)
        states_p.append(st_p)
        states_s.append(st_s)
    outs = [hp, hs]
    for k in range(5):
        outs.append(jnp.stack([s[k] for s in states_p]))
        outs.append(jnp.stack([s[k] for s in states_s]))
    return tuple(outs)
```

```python
import functools

import numpy as np
import jax
import jax.numpy as jnp
from jax import lax
from jax.experimental import pallas as pl
from jax.experimental.pallas import tpu as pltpu

N_HEADS = 16
HEAD_DIM = 128
N_KV_HEADS = 4
GROUP = N_HEADS // N_KV_HEADS
Q_W = N_HEADS * HEAD_DIM
KV_W = N_KV_HEADS * HEAD_DIM
ROW_W = 2 * KV_W
HEADS_PER_ROW = 2 * N_KV_HEADS
CMP_BLOCK = 32
CMP_STRIDE = 16
SEL_BLOCK = 64
SEL_SHIFT = 6
N_SEL = 16
WINDOW = 512
PAGE_SIZE = 128
ROPE_THETA = 10000.0
EPS = 1e-6
NEG_INF = -1e30
SEL_FORCE = 1e6
MASK_BIG = 2.0 ** 100
CONV_HALO = 32
FFN_HALO = 16
SCORE_CHUNK = 512
ROW_TILE = 128
MAX_PAGES_PER_STEP = 32

LANES = 128
SUBLANES = 8
VMEM_LIMIT = 56 * 1024 * 1024

F32 = jnp.float32
BF16 = jnp.bfloat16
NT_DIMS = (((1,), (1,)), ((), ()))


def _params(*sem):
    return pltpu.CompilerParams(dimension_semantics=sem, vmem_limit_bytes=VMEM_LIMIT)


def _rms(x):
    return x * lax.rsqrt(jnp.mean(x * x, axis=-1, keepdims=True) + EPS)


def _rope(x, cos, sin_signed):
    return x * cos + pltpu.roll(x, HEAD_DIM // 2, 1) * sin_signed


def _sigmoid(x):
    return jax.nn.sigmoid(x)


def _softmax_rows(s):
    m = jnp.max(s, axis=-1, keepdims=True)
    e = jnp.exp(s - m)
    return e, jnp.sum(e, axis=-1, keepdims=True)


def _softmax_rows_scaled(s, scale):
    m = jnp.max(s, axis=-1, keepdims=True)
    e = jnp.exp2((s - m) * np.float32(scale * np.log2(np.e)))
    return e, jnp.sum(e, axis=-1, keepdims=True)


def _rmsnorm_kernel(x_ref, g_ref, o_ref):
    o_ref[...] = (_rms(x_ref[...]) * g_ref[...]).astype(o_ref.dtype)


def _rmsnorm_bf16(x, g, tm):
    m, d = x.shape
    return pl.pallas_call(
        _rmsnorm_kernel,
        grid=(m // tm,),
        in_specs=[pl.BlockSpec((tm, d), lambda i: (i, 0)), pl.BlockSpec((1, d), lambda i: (0, 0))],
        out_specs=pl.BlockSpec((tm, d), lambda i: (i, 0)),
        out_shape=jax.ShapeDtypeStruct((m, d), BF16),
        compiler_params=_params("parallel"),
        name="rmsnorm_in",
    )(x, g.reshape(1, d))


def _matmul_kernel(x_ref, w_ref, o_ref):
    o_ref[...] = jnp.dot(x_ref[...], w_ref[...], preferred_element_type=F32)


def _matmul(x, w, tm, tn):
    m, k = x.shape
    n = w.shape[1]
    return pl.pallas_call(
        _matmul_kernel,
        grid=(m // tm, n // tn),
        in_specs=[pl.BlockSpec((tm, k), lambda i, j: (i, 0)), pl.BlockSpec((k, tn), lambda i, j: (0, j))],
        out_specs=pl.BlockSpec((tm, tn), lambda i, j: (i, j)),
        out_shape=jax.ShapeDtypeStruct((m, n), F32),
        compiler_params=_params("parallel", "arbitrary"),
        name="in_proj_main",
    )(x, w)


def _q_epilogue(acc, gq_ref, cos_ref, sin_ref, q_ref, qr_ref):
    cos = cos_ref[...]
    sin = sin_ref[...]
    gq = gq_ref[...]
    for h0 in range(0, acc.shape[1], HEAD_DIM):
        sl = slice(h0, h0 + HEAD_DIM)
        qn = _rms(acc[:, sl]) * gq
        q_ref[:, sl] = qn.astype(q_ref.dtype)
        qr_ref[:, sl] = _rope(qn, cos, sin).astype(qr_ref.dtype)


def _q_kernel(x_ref, w_ref, gq_ref, cos_ref, sin_ref, q_ref, qr_ref):
    acc = jnp.dot(x_ref[...], w_ref[...], preferred_element_type=F32)
    _q_epilogue(acc, gq_ref, cos_ref, sin_ref, q_ref, qr_ref)


def _q_kernel_skewed(x_ref, w_ref, gq_ref, cos_ref, sin_ref, q_ref, qr_ref, acc_scr):
    i = pl.program_id(0)

    @pl.when(i == 0)
    def _():
        acc_scr[1] = jnp.zeros(acc_scr.shape[1:], F32)

    for parity in (0, 1):

        @pl.when(i % 2 == parity)
        def _(parity=parity):
            _q_epilogue(acc_scr[1 - parity], gq_ref, cos_ref, sin_ref, q_ref, qr_ref)
            acc_scr[parity] = jnp.dot(x_ref[...], w_ref[...], preferred_element_type=F32)


def _q_proj(xn, w, gq, cos, sin, tm, out_dtype):
    m, k = xn.shape
    n = w.shape[1]
    nt = cos.shape[0] // tm
    steps = m // tm
    skew = steps > 1
    prev = (lambda i: jnp.maximum(i - 1, 0)) if skew else (lambda i: i)
    ospec = pl.BlockSpec((tm, n), lambda i: (prev(i), 0))
    tspec = pl.BlockSpec((tm, HEAD_DIM), lambda i: (prev(i) % nt, 0))
    return pl.pallas_call(
        _q_kernel_skewed if skew else _q_kernel,
        grid=(steps + 1 if skew else steps,),
        in_specs=[pl.BlockSpec((tm, k), lambda i: (jnp.minimum(i, steps - 1), 0)),
                  pl.BlockSpec((k, n), lambda i: (0, 0)),
                  pl.BlockSpec((1, HEAD_DIM), lambda i: (0, 0)), tspec, tspec],
        out_specs=[ospec, ospec],
        out_shape=[jax.ShapeDtypeStruct((m, n), out_dtype), jax.ShapeDtypeStruct((m, n), out_dtype)],
        scratch_shapes=[pltpu.VMEM((2, tm, n), F32)] if skew else [],
        compiler_params=_params("arbitrary"),
        name="in_proj_q",
    )(xn, w, gq.reshape(1, HEAD_DIM), cos, sin)


def _store_rows(rows_ref, acc):
    tm = acc.shape[0]
    for j in range(HEADS_PER_ROW):
        rows_ref[pl.ds(j, tm, stride=HEADS_PER_ROW), :] = acc[:, j * HEAD_DIM : (j + 1) * HEAD_DIM]


def _cmp_gate_kernel(x_ref, w_ref, rows_ref, gate_ref):
    acc = jnp.dot(x_ref[...], w_ref[...], preferred_element_type=F32)
    _store_rows(rows_ref, acc[:, :ROW_W])
    gate_ref[...] = _sigmoid(acc[:, ROW_W:])


def _cmp_gate_proj(xn, w, tm):
    m, k = xn.shape
    n = w.shape[1]
    ng = n - ROW_W
    return pl.pallas_call(
        _cmp_gate_kernel,
        grid=(m // tm,),
        in_specs=[pl.BlockSpec((tm, k), lambda i: (i, 0)), pl.BlockSpec((k, n), lambda i: (0, 0))],
        out_specs=[pl.BlockSpec((tm * HEADS_PER_ROW, HEAD_DIM), lambda i: (i, 0)),
                   pl.BlockSpec((tm, ng), lambda i: (i, 0))],
        out_shape=[jax.ShapeDtypeStruct((m * HEADS_PER_ROW, HEAD_DIM), F32), jax.ShapeDtypeStruct((m, ng), F32)],
        compiler_params=_params("parallel"),
        name="in_proj_cmp_gates",
    )(xn, w)


def _kv_epilogue(acc, gk_ref, cos_ref, sin_ref, rows_ref, rows_bf_ref, write_rows=None):
    cos = cos_ref[...]
    sin = sin_ref[...]
    gk = gk_ref[...]
    tm = acc.shape[0]
    ks = []
    for h in range(N_KV_HEADS):
        sl = slice(h * HEAD_DIM, (h + 1) * HEAD_DIM)
        k = _rope(_rms(acc[:, sl]) * gk, cos, sin)
        rows_bf_ref[:, sl] = k.astype(BF16)
        ks.append(k)
    rows_bf_ref[:, KV_W:] = acc[:, KV_W:].astype(BF16)

    def store_rows():
        for h in range(N_KV_HEADS):
            sv = slice(KV_W + h * HEAD_DIM, KV_W + (h + 1) * HEAD_DIM)
            rows_ref[pl.ds(h, tm, stride=HEADS_PER_ROW), :] = ks[h]
            rows_ref[pl.ds(N_KV_HEADS + h, tm, stride=HEADS_PER_ROW), :] = acc[:, sv]

    if write_rows is None:
        store_rows()
    else:
        pl.when(write_rows)(store_rows)


def _kv_kernel(x_ref, w_ref, gk_ref, cos_ref, sin_ref, rows_ref, rows_bf_ref, *, tiles_per_seq=None):
    acc = jnp.dot(x_ref[...], w_ref[...], preferred_element_type=F32)
    last = None if tiles_per_seq is None else (pl.program_id(0) % tiles_per_seq) == tiles_per_seq - 1
    _kv_epilogue(acc, gk_ref, cos_ref, sin_ref, rows_ref, rows_bf_ref, last)


def _kv_proj(xn, w, gk, cos, sin, tm, last_tile_only=False):
    m, k = xn.shape
    nt = cos.shape[0] // tm
    rows_map = (lambda i: (i // nt, 0)) if last_tile_only else (lambda i: (i, 0))
    rows_len = (m // nt if last_tile_only else m) * HEADS_PER_ROW
    return pl.pallas_call(
        functools.partial(_kv_kernel, tiles_per_seq=nt if last_tile_only else None),
        grid=(m // tm,),
        in_specs=[
            pl.BlockSpec((tm, k), lambda i: (i, 0)),
            pl.BlockSpec((k, ROW_W), lambda i: (0, 0)),
            pl.BlockSpec((1, HEAD_DIM), lambda i: (0, 0)),
            pl.BlockSpec((tm, HEAD_DIM), lambda i: (i % nt, 0)),
            pl.BlockSpec((tm, HEAD_DIM), lambda i: (i % nt, 0)),
        ],
        out_specs=[pl.BlockSpec((tm * HEADS_PER_ROW, HEAD_DIM), rows_map),
                   pl.BlockSpec((tm, ROW_W), lambda i: (i, 0))],
        out_shape=[jax.ShapeDtypeStruct((rows_len, HEAD_DIM), F32), jax.ShapeDtypeStruct((m, ROW_W), BF16)],
        compiler_params=_params("arbitrary" if last_tile_only else "parallel"),
        name="in_proj_kv",
    )(xn, w, gk.reshape(1, HEAD_DIM), cos, sin)


def _pe_bias_kernel(pe_ref, w1_ref, b1_ref, o_ref):
    for kv in range(2):
        acc = jnp.zeros((SUBLANES, HEAD_DIM), F32)
        for r in range(CMP_BLOCK):
            row = jnp.broadcast_to(pe_ref[kv, r : r + 1, :], (SUBLANES, HEAD_DIM)).astype(BF16)
            acc = acc + jnp.dot(
                row, w1_ref[kv, r * HEAD_DIM : (r + 1) * HEAD_DIM, :].astype(BF16), preferred_element_type=F32
            )
        o_ref[kv] = acc[0:1, :] + b1_ref[kv]


def _pe_bias(pe, w1, b1):
    hid = w1.shape[-1]
    return pl.pallas_call(
        _pe_bias_kernel,
        out_shape=jax.ShapeDtypeStruct((2, 1, hid), F32),
        compiler_params=pltpu.CompilerParams(vmem_limit_bytes=VMEM_LIMIT),
        name="cmp_pe_bias",
    )(pe, w1, b1.reshape(2, 1, hid))


def _compress_kernel(pt_ref, *refs, n_pg):
    del pt_ref
    page_refs = refs[:n_pg]
    w1_ref, bias_ref, w2_ref, b2_ref, gk_ref, kc_ref, vc_ref, carry_ref, out_scr = refs[n_pg:]
    sub_per_page = PAGE_SIZE // CMP_STRIDE
    nsub = n_pg * sub_per_page
    rows = nsub * HEADS_PER_ROW
    page_rows = sub_per_page * HEADS_PER_ROW

    @pl.when(pl.program_id(1) == 0)
    def _():
        carry_ref[...] = jnp.zeros_like(carry_ref)

    half = rows // 2
    hid2 = w1_ref.shape[1] // 2
    hid_w = hid2 // 2
    top = lax.broadcasted_iota(jnp.int32, (nsub // 2, HEADS_PER_ROW, 2 * HEAD_DIM), 1) < N_KV_HEADS
    acc = [None, None]
    for pp in range(0, CMP_STRIDE, 2):
        pieces = []
        for pg in range(n_pg):
            a = page_refs[pg][0, pl.ds(pp, sub_per_page, stride=CMP_STRIDE), :, :].reshape(page_rows, HEAD_DIM)
            b = page_refs[pg][0, pl.ds(pp + 1, sub_per_page, stride=CMP_STRIDE), :, :].reshape(page_rows, HEAD_DIM)
            pieces.append(jnp.concatenate([a, b], axis=1))
        x3 = jnp.concatenate(pieces, axis=0).reshape(nsub // 2, 2 * HEADS_PER_ROW, 2 * HEAD_DIM)
        even = x3[:, :HEADS_PER_ROW]
        odd = x3[:, HEADS_PER_ROW:]
        xs = (jnp.where(top, even, pltpu.roll(odd, N_KV_HEADS, 1)),
              jnp.where(top, pltpu.roll(even, N_KV_HEADS, 1), odd))
        w1 = w1_ref[pp * HEAD_DIM : (pp + 2) * HEAD_DIM, :]
        for kv in range(2):
            d = jnp.dot(xs[kv].reshape(half, 2 * HEAD_DIM).astype(BF16), w1[:, kv * hid2 : (kv + 1) * hid2],
                        preferred_element_type=F32)
            acc[kv] = d if acc[kv] is None else acc[kv] + d
    first_rows = lax.broadcasted_iota(jnp.int32, (half, hid_w), 0) < N_KV_HEADS
    pad = jnp.zeros((half - SUBLANES, hid_w), F32)
    for kv in range(2):
        p_first = acc[kv][:, :hid_w]
        p_second = acc[kv][:, hid_w:]
        rolled = pltpu.roll(p_first, N_KV_HEADS, 0)
        shifted = jnp.where(first_rows, jnp.concatenate([carry_ref[kv], pad], axis=0), rolled)
        carry_ref[kv] = rolled[0:SUBLANES, :]
        hid = shifted + p_second + bias_ref[kv]
        hid = hid * _sigmoid(hid)
        out = jnp.dot(hid.astype(BF16), w2_ref[:, kv * HEAD_DIM : (kv + 1) * HEAD_DIM],
                      preferred_element_type=F32) + b2_ref[kv]
        if kv == 0:
            out = _rms(out) * gk_ref[...]
        out_scr[kv] = out
        dst = kc_ref if kv == 0 else vc_ref
        for h in range(N_KV_HEADS):
            dst[0, h, :, :] = out_scr[kv, pl.ds(h, nsub, stride=N_KV_HEADS), :].astype(BF16)


def _compress(pages, page_table, w1all, bias, w2all, b2, gk):
    n_seq, n_pp = page_table.shape
    n_pg = max(d for d in range(1, MAX_PAGES_PER_STEP + 1) if n_pp % d == 0)
    steps = n_pp // n_pg
    nsub = n_pg * (PAGE_SIZE // CMP_STRIDE)
    n_slots = steps * nsub
    hid = bias.shape[-1]

    def page_spec(k):
        return pl.BlockSpec((1, PAGE_SIZE, HEADS_PER_ROW, HEAD_DIM),
                            lambda b, s, pt: (pt[b * n_pp + s * n_pg + k], 0, 0, 0))

    const2 = lambda b, s, pt: (0, 0)
    const3 = lambda b, s, pt: (0, 0, 0)
    out_spec = pl.BlockSpec((1, N_KV_HEADS, nsub, HEAD_DIM), lambda b, s, pt: (b, 0, s, 0))
    grid_spec = pltpu.PrefetchScalarGridSpec(
        num_scalar_prefetch=1,
        grid=(n_seq, steps),
        in_specs=[page_spec(k) for k in range(n_pg)]
        + [
            pl.BlockSpec(w1all.shape, const2),
            pl.BlockSpec((2, 1, hid), const3),
            pl.BlockSpec(w2all.shape, const2),
            pl.BlockSpec((2, 1, HEAD_DIM), const3),
            pl.BlockSpec((1, HEAD_DIM), const2),
        ],
        out_specs=[out_spec, out_spec],
        scratch_shapes=[pltpu.VMEM((2, SUBLANES, hid), F32), pltpu.VMEM((2, nsub * N_KV_HEADS, HEAD_DIM), F32)],
    )
    shape = jax.ShapeDtypeStruct((n_seq, N_KV_HEADS, n_slots, HEAD_DIM), BF16)
    return pl.pallas_call(
        functools.partial(_compress_kernel, n_pg=n_pg),
        grid_spec=grid_spec,
        out_shape=[shape, shape],
        compiler_params=_params("parallel", "arbitrary"),
        name="compress",
    )(page_table.reshape(-1), *([pages] * n_pg), w1all, bias, w2all, b2.reshape(2, 1, HEAD_DIM), gk.reshape(1, HEAD_DIM))


def _nsa_prompt_kernel(
    q_ref, qr_ref, gate_ref, kc_ref, vc_ref, ks_ref, vs_ref, kw_ref, vw_ref, onehot_ref,
    o_ref, osel_scr, s_scr, p_scr, m_scr, owin_scr, sw_scr, pw_scr, mw_scr,
    *, tq, t_len, nb, n_sel, n_slots, kt):
    i = pl.program_id(2)
    q0 = i * tq
    rows = GROUP * tq
    scale = HEAD_DIM ** -0.5
    heads = [slice(g * HEAD_DIM, (g + 1) * HEAD_DIM) for g in range(GROUP)]
    q = jnp.concatenate([q_ref[:, sl] for sl in heads], axis=0)
    qr = jnp.concatenate([qr_ref[:, sl] for sl in heads], axis=0)

    def attend(lhs, rhs_chunk, v_ref, k0, n_keys, mask_chunk, s_buf, p_buf, m_buf):
        n_chunks = -(-n_keys // SCORE_CHUNK)
        for c in range(n_chunks):
            c0 = c * SCORE_CHUNK
            w = min(SCORE_CHUNK, n_keys - c0)
            s = lax.dot_general(lhs, rhs_chunk(c0, w), NT_DIMS, preferred_element_type=F32)
            s = mask_chunk(c0, w, s.reshape(GROUP, tq, w)).reshape(rows, w)
            s_buf[:, c0 : c0 + w] = s
            pm = s[:, 0:LANES]
            for j0 in range(LANES, w, LANES):
                pm = jnp.maximum(pm, s[:, j0 : j0 + LANES])
            m_buf[:, c * LANES : (c + 1) * LANES] = pm
        coef = np.float32(scale * np.log2(np.e))
        outs = []
        for g in range(GROUP):
            sums = []
            for r0 in range(g * tq, (g + 1) * tq, ROW_TILE):
                r = slice(r0, r0 + ROW_TILE)
                m_acc = m_buf[r, 0:LANES]
                for c in range(1, n_chunks):
                    m_acc = jnp.maximum(m_acc, m_buf[r, c * LANES : (c + 1) * LANES])
                m_b = jnp.broadcast_to(jnp.max(m_acc, axis=-1, keepdims=True), (ROW_TILE, LANES))
                l_acc = jnp.zeros((ROW_TILE, LANES), F32)
                for c0 in range(0, n_keys, LANES):
                    p = jnp.exp2((s_buf[r, c0 : c0 + LANES] - m_b) * coef)
                    l_acc = l_acc + p
                    p_buf[r, c0 : c0 + LANES] = p.astype(BF16)
                sums.append(jnp.sum(l_acc, axis=-1, keepdims=True))
            rg = slice(g * tq, (g + 1) * tq)
            pv = jnp.dot(p_buf[rg, 0:n_keys], v_ref[0, pl.ds(k0, n_keys), :], preferred_element_type=F32)
            outs.append(pv / jnp.concatenate(sums, axis=0))
        return jnp.concatenate(outs, axis=0)

    s_t = lax.dot_general(kc_ref[0, 0], q, NT_DIMS, preferred_element_type=F32) * scale
    slot = lax.broadcasted_iota(jnp.int32, (n_slots, rows), 0)
    tok = q0 + (lax.broadcasted_iota(jnp.int32, (n_slots, rows), 1) & (tq - 1))
    valid = (slot >= 1) & (slot * CMP_STRIDE + CMP_STRIDE <= tok + 1)
    s_m = jnp.where(valid, s_t, NEG_INF)
    e = jnp.exp(s_m - jnp.max(s_m, axis=0, keepdims=True))
    p_t = jnp.where(valid, e / jnp.sum(e, axis=0, keepdims=True), 0.0)
    o_cmp = jnp.dot(p_t.T.astype(BF16), vc_ref[0, 0], preferred_element_type=F32)

    p_sum = p_t[:, 0:tq]
    for g in range(1, GROUP):
        p_sum = p_sum + p_t[:, g * tq : (g + 1) * tq]
    ratio = SEL_BLOCK // CMP_STRIDE
    jj = lax.broadcasted_iota(jnp.int32, (nb, n_slots), 0)
    mm = lax.broadcasted_iota(jnp.int32, (nb, n_slots), 1)
    overlap = jnp.where((mm >= ratio * jj) & (mm <= ratio * jj + ratio), 1.0, 0.0)
    imp = jnp.dot(overlap, p_sum, precision=lax.Precision.HIGHEST, preferred_element_type=F32)
    jb = lax.broadcasted_iota(jnp.int32, (nb, tq), 0)
    pos = q0 + lax.broadcasted_iota(jnp.int32, (nb, tq), 1)
    cur = lax.shift_right_logical(pos, SEL_SHIFT)
    causal = jb * SEL_BLOCK <= pos
    forced = (jb == 0) | (jb == cur) | (jb == cur - 1)
    score = jnp.where(causal, jnp.where(forced, SEL_FORCE, imp), -1.0)
    cnts = [jnp.zeros((nb, tq), F32) for _ in range(4)]
    for i2 in range(nb):
        r = score[i2 : i2 + 1, :]
        cnts[i2 % 4] = cnts[i2 % 4] + jnp.where(r > score, 1.0, jnp.where((r == score) & (jb > i2), 1.0, 0.0))
    cnt = (cnts[0] + cnts[1]) + (cnts[2] + cnts[3])
    sel = jnp.where((cnt < n_sel) & causal, 1.0, 0.0)
    if nb < LANES:
        sel = jnp.concatenate([sel, jnp.zeros((LANES - nb, tq), F32)], axis=0)
    sel_q = ((sel.T - 1.0) * MASK_BIG).astype(BF16)
    lhs_sel = jnp.concatenate([qr, jnp.concatenate([sel_q] * GROUP, axis=0)], axis=1)

    def sel_rhs(c0, w):
        return jnp.concatenate([ks_ref[0, c0 : c0 + w, :], onehot_ref[c0 : c0 + w, :]], axis=1)

    def sel_mask(c0, w, s, first_diag_key):
        if c0 + w <= first_diag_key:
            return s
        kpos = c0 + lax.broadcasted_iota(jnp.int32, (tq, w), 1)
        qpos = q0 + lax.broadcasted_iota(jnp.int32, (tq, w), 0)
        return jnp.where((kpos <= qpos)[None], s, -MASK_BIG)

    lw = WINDOW + tq
    kw0 = pl.multiple_of(jnp.maximum(q0 - WINDOW, 0), tq)

    def window_mask(c0, w, s):
        diff = (q0 + lax.broadcasted_iota(jnp.int32, (tq, w), 0)) - (
            kw0 + c0 + lax.broadcasted_iota(jnp.int32, (tq, w), 1))
        return s + jnp.where((diff >= 0) & (diff < WINDOW), 0.0, NEG_INF)[None]

    owin_scr[...] = attend(qr, lambda c0, w: kw_ref[0, pl.ds(kw0 + c0, w), :], vw_ref, kw0, lw, window_mask,
                           sw_scr, pw_scr, mw_scr)

    n_tiles = (q0 + tq + kt - 1) // kt
    for nt in range(1, t_len // kt + 1):

        @pl.when(n_tiles == nt)
        def _(nt=nt):
            osel_scr[...] = attend(lhs_sel, sel_rhs, vs_ref, 0, nt * kt,
                                   functools.partial(sel_mask, first_diag_key=(nt - 1) * kt), s_scr, p_scr, m_scr)

    o_sel = osel_scr[...]
    o_win = owin_scr[...]
    gt = gate_ref[...]
    outs = []
    for g in range(GROUP):
        sl = slice(g * tq, (g + 1) * tq)
        outs.append(
            gt[:, g : g + 1] * o_cmp[sl]
            + gt[:, GROUP + g : GROUP + g + 1] * o_sel[sl]
            + gt[:, 2 * GROUP + g : 2 * GROUP + g + 1] * o_win[sl]
        )
    o_ref[...] = jnp.concatenate(outs, axis=1).astype(o_ref.dtype)


def _nsa_prompt(q_bf, qr_bf, gates, kc, vc, sel_bf, win_bf, n_b, t_len):
    tq = 128
    kt = min(512, t_len)
    nb = t_len // SEL_BLOCK
    n_slots = kc.shape[2]
    n_sel = min(N_SEL, nb)
    assert t_len % kt == 0 and t_len >= WINDOW + tq and nb <= LANES and n_slots == t_len // CMP_STRIDE
    ntq = t_len // tq
    m = n_b * t_len
    rows = GROUP * tq
    lw = WINDOW + tq
    onehot = (np.arange(t_len)[:, None] // SEL_BLOCK == np.arange(LANES)[None, :]).astype(np.float32)
    sel3 = sel_bf.reshape(n_b, t_len, ROW_W)
    win3 = win_bf.reshape(n_b, t_len, ROW_W)
    kspec = pl.BlockSpec((1, t_len, HEAD_DIM), lambda b, h, i: (b, 0, h))
    vspec = pl.BlockSpec((1, t_len, HEAD_DIM), lambda b, h, i: (b, 0, N_KV_HEADS + h))
    cspec = pl.BlockSpec((1, 1, n_slots, HEAD_DIM), lambda b, h, i: (b, h, 0, 0))
    qspec = pl.BlockSpec((tq, GROUP * HEAD_DIM), lambda b, h, i: (b * ntq + i, h))
    kern = functools.partial(
        _nsa_prompt_kernel, tq=tq, t_len=t_len, nb=nb, n_sel=n_sel, n_slots=n_slots, kt=kt)
    return pl.pallas_call(
        kern,
        grid=(n_b, N_KV_HEADS, ntq),
        in_specs=[
            qspec,
            qspec,
            pl.BlockSpec((tq, LANES), lambda b, h, i: (b * ntq + i, h)),
            cspec,
            cspec,
            kspec,
            vspec,
            kspec,
            vspec,
            pl.BlockSpec((t_len, LANES), lambda b, h, i: (0, 0)),
        ],
        out_specs=qspec,
        out_shape=jax.ShapeDtypeStruct((m, Q_W), BF16),
        scratch_shapes=[pltpu.VMEM((rows, HEAD_DIM), F32),
                        pltpu.VMEM((rows, t_len), F32), pltpu.VMEM((rows, t_len), BF16),
                        pltpu.VMEM((rows, -(-t_len // SCORE_CHUNK) * LANES), F32),
                        pltpu.VMEM((rows, HEAD_DIM), F32),
                        pltpu.VMEM((rows, lw), F32), pltpu.VMEM((rows, lw), BF16),
                        pltpu.VMEM((rows, -(-lw // SCORE_CHUNK) * LANES), F32)],
        compiler_params=_params("parallel", "parallel", "arbitrary"),
        name="nsa_prompt",
    )(q_bf, qr_bf, gates, kc, vc, sel3, sel3, win3, win3, jnp.asarray(onehot, BF16))


def _layer_norm(y, g, b):
    yc = y - jnp.mean(y, axis=-1, keepdims=True)
    var = jnp.mean(yc * yc, axis=-1, keepdims=True)
    return yc * lax.rsqrt(var + EPS) * g + b


def _glu(z):
    c = z.shape[1] // 2
    return z[:, :c] * _sigmoid(z[:, c:])


def _conv_prompt_kernel(z_ref, halo_ref, wdw_ref, bdw_ref, gln_ref, bln_ref, wout_ref, o_ref, wattn_ref,
                        gm_a_ref, gm_c_ref, mixed_ref, tail_ref, ufull, yconv, yattn, *, tc, width):
    i = pl.program_id(1)
    u = _glu(z_ref[...])
    uh = jnp.where(i == 0, 0.0, _glu(halo_ref[...]))
    ufull[0:CONV_HALO, :] = uh
    ufull[CONV_HALO:, :] = u
    tail_ref[0] = u[tc - CONV_HALO :, :]
    ch = u.shape[1]
    rc, cc = 64, 256
    off = CONV_HALO - (width - 1)
    d_out = wattn_ref.shape[1]
    n_conv_chunks = (tc // rc) * (ch // cc)
    aw = d_out // n_conv_chunks * 2
    step = 0
    for r0 in range(0, tc, rc):
        for c0 in range(0, ch, cc):
            if step % 2 == 0:
                a0 = step // 2 * aw
                yattn[:, a0 : a0 + aw] = jnp.dot(o_ref[...], wattn_ref[:, a0 : a0 + aw], preferred_element_type=F32)
            step += 1
            acc = jnp.zeros((rc, cc), F32)
            span_all = rc + CONV_HALO
            u_rows = ufull[r0 : r0 + span_all, c0 : c0 + cc]
            for b in range(min(SUBLANES, width)):
                n_a = (width - 1 - b) // SUBLANES + 1
                ub = pltpu.roll(u_rows, (span_all - off - b) % span_all, 0)
                for a in range(n_a):
                    k = SUBLANES * a + b
                    acc = acc + wdw_ref[k : k + 1, c0 : c0 + cc] * ub[SUBLANES * a : SUBLANES * a + rc, :]
            yconv[r0 : r0 + rc, c0 : c0 + cc] = acc + bdw_ref[:, c0 : c0 + cc]
    y = _layer_norm(yconv[...], gln_ref[...], bln_ref[...])
    y = y * _sigmoid(y)
    y_conv = jnp.dot(y.astype(BF16), wout_ref[...], preferred_element_type=F32)
    mixed_ref[...] = (_sigmoid(gm_a_ref[...]) * yattn[...] + _sigmoid(gm_c_ref[...]) * y_conv).astype(mixed_ref.dtype)


def _conv_mix_prompt(zmain, o, w_dw, b_dw, g_ln, b_ln, w_out, w_attn, n_b, t_len):
    tc = 256
    width, ch = w_dw.shape
    d_out = w_out.shape[1]
    ntc = t_len // tc
    hb = tc // CONV_HALO
    m = n_b * t_len
    glu_blk = 0
    gm_blk = (zmain.shape[1] - 2 * d_out) // d_out
    row1 = lambda b, i: (0, 0)
    tile = lambda blk: pl.BlockSpec((tc, d_out), lambda b, i: (b * ntc + i, blk))
    kern = functools.partial(_conv_prompt_kernel, tc=tc, width=width)
    return pl.pallas_call(
        kern,
        grid=(n_b, ntc),
        in_specs=[
            pl.BlockSpec((tc, 2 * ch), lambda b, i: (b * ntc + i, glu_blk)),
            pl.BlockSpec((CONV_HALO, 2 * ch), lambda b, i: (jnp.maximum((b * ntc + i) * hb - 1, 0), glu_blk)),
            pl.BlockSpec((width, ch), row1),
            pl.BlockSpec((1, ch), row1),
            pl.BlockSpec((1, ch), row1),
            pl.BlockSpec((1, ch), row1),
            pl.BlockSpec((ch, d_out), row1),
            pl.BlockSpec((tc, o.shape[1]), lambda b, i: (b * ntc + i, 0)),
            pl.BlockSpec(w_attn.shape, row1),
            tile(gm_blk),
            tile(gm_blk + 1),
        ],
        out_specs=[
            tile(0),
            pl.BlockSpec((1, CONV_HALO, ch), lambda b, i: (b, 0, 0)),
        ],
        out_shape=[jax.ShapeDtypeStruct((m, d_out), BF16), jax.ShapeDtypeStruct((n_b, CONV_HALO, ch), F32)],
        scratch_shapes=[pltpu.VMEM((tc + CONV_HALO, ch), F32), pltpu.VMEM((tc, ch), F32), pltpu.VMEM((tc, d_out), F32)],
        compiler_params=_params("parallel", "arbitrary"),
        name="conv_mix_prompt",
    )(zmain, zmain, w_dw, b_dw.reshape(1, ch), g_ln.reshape(1, ch), b_ln.reshape(1, ch), w_out, o, w_attn,
      zmain, zmain)


def _conv_sample_kernel(z_ref, st_ref, wdw_ref, bdw_ref, gln_ref, bln_ref, wout_ref, y_ref, u_ref, *, width):
    u = _glu(z_ref[...])
    u_ref[...] = u
    y = jnp.sum(st_ref[...] * wdw_ref[0 : width - 1, :][None], axis=1) + wdw_ref[width - 1 : width, :] * u
    y = _layer_norm(y + bdw_ref[...], gln_ref[...], bln_ref[...])
    y = y * _sigmoid(y)
    y_ref[...] = jnp.dot(y.astype(BF16), wout_ref[...], preferred_element_type=F32)


def _conv_sample(zmain, state, w_dw, b_dw, g_ln, b_ln, w_out):
    width, ch = w_dw.shape
    n_b = state.shape[0]
    d_out = w_out.shape[1]
    glu_blk = 0
    c2 = lambda i: (0, 0)
    return pl.pallas_call(
        functools.partial(_conv_sample_kernel, width=width),
        grid=(1,),
        in_specs=[
            pl.BlockSpec((n_b, 2 * ch), lambda i: (0, glu_blk)),
            pl.BlockSpec((n_b, width - 1, ch), lambda i: (0, 0, 0)),
            pl.BlockSpec((width, ch), c2),
            pl.BlockSpec((1, ch), c2),
            pl.BlockSpec((1, ch), c2),
            pl.BlockSpec((1, ch), c2),
            pl.BlockSpec((ch, d_out), c2),
        ],
        out_specs=[pl.BlockSpec((n_b, d_out), c2), pl.BlockSpec((n_b, ch), c2)],
        out_shape=[jax.ShapeDtypeStruct((n_b, d_out), F32), jax.ShapeDtypeStruct((n_b, ch), F32)],
        compiler_params=_params("arbitrary"),
        name="conv_sample",
    )(zmain, state, w_dw, b_dw.reshape(1, ch), g_ln.reshape(1, ch), b_ln.reshape(1, ch), w_out)


def _attn_mix_kernel(o_ref, w_ref, gm_a_ref, gm_c_ref, yc_ref, mixed_ref):
    ya = jnp.dot(o_ref[...], w_ref[...], preferred_element_type=F32)
    mixed_ref[...] = (_sigmoid(gm_a_ref[...]) * ya + _sigmoid(gm_c_ref[...]) * yc_ref[...]).astype(mixed_ref.dtype)


def _attn_mix(o, w_attn, zmain, y_conv, tm):
    m, k = o.shape
    d = w_attn.shape[1]
    gm_blk = (zmain.shape[1] - 2 * d) // d
    return pl.pallas_call(
        _attn_mix_kernel,
        grid=(m // tm,),
        in_specs=[
            pl.BlockSpec((tm, k), lambda i: (i, 0)),
            pl.BlockSpec((k, d), lambda i: (0, 0)),
            pl.BlockSpec((tm, d), lambda i: (i, gm_blk)),
            pl.BlockSpec((tm, d), lambda i: (i, gm_blk + 1)),
            pl.BlockSpec((tm, d), lambda i: (i, 0)),
        ],
        out_specs=pl.BlockSpec((tm, d), lambda i: (i, 0)),
        out_shape=jax.ShapeDtypeStruct((m, d), BF16),
        compiler_params=_params("parallel"),
        name="attn_out_mix",
    )(o, w_attn, zmain, zmain, y_conv)


def _out_proj_kernel(mixed_ref, w_ref, x_ref, h_ref):
    h_ref[...] = x_ref[...] + jnp.dot(mixed_ref[...], w_ref[...], preferred_element_type=F32)


def _out_proj(mixed, w_out, x, tm):
    m, k = mixed.shape
    d = w_out.shape[1]
    return pl.pallas_call(
        _out_proj_kernel,
        grid=(m // tm,),
        in_specs=[
            pl.BlockSpec((tm, k), lambda i: (i, 0)),
            pl.BlockSpec((k, d), lambda i: (0, 0)),
            pl.BlockSpec((tm, d), lambda i: (i, 0)),
        ],
        out_specs=pl.BlockSpec((tm, d), lambda i: (i, 0)),
        out_shape=jax.ShapeDtypeStruct((m, d), F32),
        compiler_params=_params("parallel"),
        name="out_proj",
    )(mixed, w_out, x)


def _gelu(x):
    return 0.5 * x * (1.0 + lax.erf(x * np.float32(np.sqrt(0.5))))


def _ffn_kernel(*refs, tm, tiles_per_seq, sample):
    if sample:
        h_ref, g_ref, wa_ref, wb_ref, wdw_ref, bdw_ref, wd_ref, s0_ref, s1_ref, out_ref, a_ref, hn, acc = refs
    else:
        h_ref, halo_ref, g_ref, wa_ref, wb_ref, wdw_ref, bdw_ref, wd_ref, out_ref, a_ref, hn, acc = refs
    i = pl.program_id(0)
    j = pl.program_id(1)
    lead = 0 if sample else FFN_HALO

    @pl.when(j == 0)
    def _():
        hn[lead:, :] = (_rms(h_ref[...]) * g_ref[...]).astype(BF16)
        if not sample:
            hn[0:lead, :] = (_rms(halo_ref[...]) * g_ref[...]).astype(BF16)
        acc[...] = jnp.zeros_like(acc)

    a_ext = jnp.dot(hn[...], wa_ref[...], preferred_element_type=F32)
    gate = jnp.dot(hn[lead:, :], wb_ref[...], preferred_element_type=F32)
    a0 = a_ext[lead:, :]
    if sample:
        a_m2 = s0_ref[...]
        a_m1 = s1_ref[...]
        a_ref[...] = a0
    else:
        first = (i % tiles_per_seq) == 0
        r = lax.broadcasted_iota(jnp.int32, a0.shape, 0)
        a_m2 = jnp.where(first & (r < 2), 0.0, a_ext[lead - 2 : lead - 2 + tm, :])
        a_m1 = jnp.where(first & (r < 1), 0.0, a_ext[lead - 1 : lead - 1 + tm, :])
        a_ref[0] = a0[tm - SUBLANES :, :]
    conv = wdw_ref[0:1, :] * a_m2 + wdw_ref[1:2, :] * a_m1 + wdw_ref[2:3, :] * a0 + bdw_ref[...]
    y = (_gelu(conv) * gate).astype(BF16)
    acc[...] += jnp.dot(y, wd_ref[...], preferred_element_type=F32)

    @pl.when(j == pl.num_programs(1) - 1)
    def _():
        out_ref[...] = h_ref[...] + acc[...]


def _conv_ffn(h, g, w_up_a, w_up_b, w_dw, b_dw, w_down, *, tm, t_len=None, state=None):
    m, d = h.shape
    dff = w_up_a.shape[1]
    tf = 512
    nf = dff // tf
    sample = state is not None
    assert w_dw.shape[0] == 3
    col = lambda i, j: (0, j)
    common = [
        pl.BlockSpec((1, d), lambda i, j: (0, 0)),
        pl.BlockSpec((d, tf), col),
        pl.BlockSpec((d, tf), col),
        pl.BlockSpec((3, tf), col),
        pl.BlockSpec((1, tf), col),
        pl.BlockSpec((tf, d), lambda i, j: (j, 0)),
    ]
    args = [g.reshape(1, d), w_up_a, w_up_b, w_dw, b_dw.reshape(1, dff), w_down]
    hspec = pl.BlockSpec((tm, d), lambda i, j: (i, 0))
    if sample:
        assert m == tm
        st = state.reshape(m, 2 * dff)
        in_specs = [hspec] + common + [pl.BlockSpec((tm, tf), col), pl.BlockSpec((tm, tf), lambda i, j: (0, nf + j))]
        args = [h] + args + [st, st]
        a_spec = pl.BlockSpec((tm, tf), col)
        a_shape = jax.ShapeDtypeStruct((m, dff), F32)
        tiles_per_seq = 1
        lead = 0
    else:
        tiles_per_seq = t_len // tm
        hb = tm // FFN_HALO
        in_specs = [hspec, pl.BlockSpec((FFN_HALO, d), lambda i, j: (jnp.maximum(i * hb - 1, 0), 0))] + common
        args = [h, h] + args
        a_spec = pl.BlockSpec((1, SUBLANES, tf), lambda i, j: (i, 0, j))
        a_shape = jax.ShapeDtypeStruct((m // tm, SUBLANES, dff), F32)
        lead = FFN_HALO
    kern = functools.partial(_ffn_kernel, tm=tm, tiles_per_seq=tiles_per_seq, sample=sample)
    return pl.pallas_call(
        kern,
        grid=(m // tm, nf),
        in_specs=in_specs,
        out_specs=[hspec, a_spec],
        out_shape=[jax.ShapeDtypeStruct((m, d), F32), a_shape],
        scratch_shapes=[pltpu.VMEM((tm + lead, d), BF16), pltpu.VMEM((tm, d), F32)],
        compiler_params=_params("parallel", "arbitrary"),
        name="conv_ffn",
    )(*args)


def _ple_kernel(h_ref, g_ref, wg_ref, pe_ref, wp_ref, o_ref):
    h = h_ref[...]
    hn = (_rms(h) * g_ref[...]).astype(BF16)
    gate = _sigmoid(jnp.dot(hn, wg_ref[...], preferred_element_type=F32))
    proj = jnp.dot(pe_ref[...], wp_ref[...], preferred_element_type=F32)
    o_ref[...] = h + gate * proj


def _ple(h, g, w_gate, pe, w_proj, tm):
    m, d = h.shape
    pd = pe.shape[1]
    return pl.pallas_call(
        _ple_kernel,
        grid=(m // tm,),
        in_specs=[
            pl.BlockSpec((tm, d), lambda i: (i, 0)),
            pl.BlockSpec((1, d), lambda i: (0, 0)),
            pl.BlockSpec((d, d), lambda i: (0, 0)),
            pl.BlockSpec((tm, pd), lambda i: (i, 0)),
            pl.BlockSpec((pd, d), lambda i: (0, 0)),
        ],
        out_specs=pl.BlockSpec((tm, d), lambda i: (i, 0)),
        out_shape=jax.ShapeDtypeStruct((m, d), F32),
        compiler_params=_params("parallel"),
        name="ple",
    )(h, g.reshape(1, d), w_gate, pe, w_proj)


def _cmp_sample_kernel(q_ref, kc_ref, vc_ref, ov_ref, score_ref, ocmp_ref, *, pos, nb, n_slots):
    scale = HEAD_DIM ** -0.5
    qn = q_ref[0]
    slot = lax.broadcasted_iota(jnp.int32, (GROUP, n_slots), 1)
    valid = (slot >= 1) & (slot * CMP_STRIDE + CMP_STRIDE <= pos + 1)
    sums = []
    for h in range(N_KV_HEADS):
        qh = qn[h * GROUP : (h + 1) * GROUP, :].astype(BF16)
        s = lax.dot_general(qh, kc_ref[0, h], NT_DIMS, preferred_element_type=F32) * scale
        e, l = _softmax_rows(jnp.where(valid, s, NEG_INF))
        p = jnp.where(valid, e / l, 0.0)
        ocmp_ref[0, h * GROUP : (h + 1) * GROUP, :] = jnp.dot(p.astype(BF16), vc_ref[0, h], preferred_element_type=F32)
        sums.append(jnp.sum(p, axis=0, keepdims=True))
    p_sum = jnp.concatenate(sums + [jnp.zeros((SUBLANES - N_KV_HEADS, n_slots), F32)], axis=0)
    imp = jnp.dot(p_sum, ov_ref[...], precision=lax.Precision.HIGHEST, preferred_element_type=F32)
    jb = lax.broadcasted_iota(jnp.int32, imp.shape, 1)
    cur = pos // SEL_BLOCK
    causal = (jb * SEL_BLOCK <= pos) & (jb < nb)
    forced = (jb == 0) | (jb == cur) | (jb == cur - 1)
    score_ref[0] = jnp.where(jb < nb, jnp.where(causal, jnp.where(forced, SEL_FORCE, imp), -1.0), -2.0)


def _cmp_sample(q3, kc, vc, pos, nb):
    n_b = q3.shape[0]
    n_slots = kc.shape[2]
    nbp = -(-nb // LANES) * LANES
    ratio = SEL_BLOCK // CMP_STRIDE
    mm = np.arange(n_slots)[:, None]
    jj = np.arange(nbp)[None, :]
    overlap = ((mm >= ratio * jj) & (mm <= ratio * jj + ratio) & (jj < nb)).astype(np.float32)
    qspec = pl.BlockSpec((1, N_HEADS, HEAD_DIM), lambda b: (b, 0, 0))
    cspec = pl.BlockSpec((1, N_KV_HEADS, n_slots, HEAD_DIM), lambda b: (b, 0, 0, 0))
    return pl.pallas_call(
        functools.partial(_cmp_sample_kernel, pos=pos, nb=nb, n_slots=n_slots),
        grid=(n_b,),
        in_specs=[qspec, cspec, cspec, pl.BlockSpec((n_slots, nbp), lambda b: (0, 0))],
        out_specs=[pl.BlockSpec((1, SUBLANES, nbp), lambda b: (b, 0, 0)), qspec],
        out_shape=[
            jax.ShapeDtypeStruct((n_b, SUBLANES, nbp), F32),
            jax.ShapeDtypeStruct((n_b, N_HEADS, HEAD_DIM), F32),
        ],
        compiler_params=_params("parallel"),
        name="cmp_sample",
    )(q3, kc, vc, jnp.asarray(overlap))


def _topk_kernel(score_ref, idx_ref, *, n_sel):
    sc = score_ref[...]
    lane = lax.broadcasted_iota(jnp.int32, sc.shape, 1)
    out_lane = lax.broadcasted_iota(jnp.int32, idx_ref.shape, 1)
    out = jnp.zeros(idx_ref.shape, jnp.int32)
    for it in range(n_sel):
        mx = jnp.max(sc, axis=-1, keepdims=True)
        idx = jnp.min(jnp.where(sc == mx, lane, sc.shape[1]), axis=-1, keepdims=True)
        out = jnp.where(out_lane == it, idx, out)
        sc = jnp.where(lane == idx, -3.0, sc)
    idx_ref[...] = out


def _topk(score, n_sel):
    rows = score.shape[0]
    return pl.pallas_call(
        functools.partial(_topk_kernel, n_sel=n_sel),
        out_shape=jax.ShapeDtypeStruct((rows, LANES), jnp.int32),
        compiler_params=pltpu.CompilerParams(vmem_limit_bytes=VMEM_LIMIT),
        name="topk_sample",
    )(score)


def _sel_sample_kernel(idx_ref, pt_ref, *refs, n_sel, n_past_blocks):
    del pt_ref
    blk_refs = refs[:n_sel]
    qr_ref, new_ref, o_ref = refs[n_sel:]
    b = pl.program_id(0)
    h = pl.program_id(1)
    scale = HEAD_DIM ** -0.5
    blk_rows = SEL_BLOCK * HEADS_PER_ROW
    qr = qr_ref[0, 0].astype(BF16)
    rows = jnp.concatenate([r[...] for r in blk_refs], axis=0).astype(BF16)
    s = lax.dot_general(qr, rows, NT_DIMS, preferred_element_type=F32) * scale
    lane = lax.broadcasted_iota(jnp.int32, s.shape, 1)
    slot = lane // blk_rows
    bias = jnp.where((lane & (HEADS_PER_ROW - 1)) == h, 0.0, NEG_INF)
    n_new = jnp.int32(0)
    for j in range(n_sel):
        is_new = idx_ref[(b * N_KV_HEADS + h) * n_sel + j] >= n_past_blocks
        bias = jnp.where(slot == j, jnp.where(is_new, NEG_INF, bias), bias)
        n_new = n_new + jnp.where(is_new, 1, 0)
    s = s + bias
    k_new = new_ref[0, pl.ds(h, 1), :].astype(BF16).astype(F32)
    v_new = new_ref[0, pl.ds(N_KV_HEADS + h, 1), :].astype(BF16).astype(F32)
    s_new = jnp.sum(qr.astype(F32) * k_new, axis=-1, keepdims=True) * scale
    s_new = jnp.where(n_new > 0, s_new, NEG_INF)
    m = jnp.maximum(jnp.max(s, axis=-1, keepdims=True), s_new)
    e = jnp.exp(s - m)
    e_new = jnp.exp(s_new - m)
    l = jnp.sum(e, axis=-1, keepdims=True) + e_new
    e_v = pltpu.roll(e, N_KV_HEADS, 1)
    o = jnp.dot(e_v.astype(BF16), rows, preferred_element_type=F32) + e_new.astype(BF16).astype(F32) * v_new
    o_ref[0, 0] = o / l


def _sel_sample(idx_flat, pt_flat, cache_rows, qr4, new_rows, n_sel, n_past_blocks, n_pp):
    n_b = qr4.shape[0]
    per_page = PAGE_SIZE // SEL_BLOCK
    blk_rows = SEL_BLOCK * HEADS_PER_ROW

    def bspec(j):
        def imap(b, h, idx, pt):
            blk_id = jnp.minimum(idx[(b * N_KV_HEADS + h) * n_sel + j], n_past_blocks - 1)
            return (pt[b * n_pp + blk_id // per_page] * per_page + blk_id % per_page, 0)
        return pl.BlockSpec((blk_rows, HEAD_DIM), imap)

    grid_spec = pltpu.PrefetchScalarGridSpec(
        num_scalar_prefetch=2,
        grid=(n_b, N_KV_HEADS),
        in_specs=[bspec(j) for j in range(n_sel)]
        + [
            pl.BlockSpec((1, 1, GROUP, HEAD_DIM), lambda b, h, idx, pt: (b, h, 0, 0)),
            pl.BlockSpec((1, HEADS_PER_ROW, HEAD_DIM), lambda b, h, idx, pt: (b, 0, 0)),
        ],
        out_specs=pl.BlockSpec((1, 1, GROUP, HEAD_DIM), lambda b, h, idx, pt: (b, h, 0, 0)),
    )
    return pl.pallas_call(
        functools.partial(_sel_sample_kernel, n_sel=n_sel, n_past_blocks=n_past_blocks),
        grid_spec=grid_spec,
        out_shape=jax.ShapeDtypeStruct((n_b, N_KV_HEADS, GROUP, HEAD_DIM), F32),
        compiler_params=_params("parallel", "arbitrary"),
        name="sel_sample",
    )(idx_flat, pt_flat, *([cache_rows] * n_sel), qr4, new_rows)


def _head_lane_mask(shape, kv_head_of_row):
    lane = lax.broadcasted_iota(jnp.int32, shape, 1)
    return (lane & (HEADS_PER_ROW - 1)) == kv_head_of_row


def _win_sample_kernel(win_ref, new_ref, qr_ref, ocmp_ref, osel_ref, g_ref, nw_ref, o_ref, *, wb):
    scale = HEAD_DIM ** -0.5
    keep = (wb - 1) * HEADS_PER_ROW
    nw_ref[0, 0:keep, :] = win_ref[0, HEADS_PER_ROW:, :]
    nw_ref[0, keep:, :] = new_ref[0]
    rows = nw_ref[0].astype(BF16)
    qr = qr_ref[0].astype(BF16)
    s = lax.dot_general(qr, rows, NT_DIMS, preferred_element_type=F32) * scale
    kv_head = lax.broadcasted_iota(jnp.int32, s.shape, 0) // GROUP
    s = jnp.where(_head_lane_mask(s.shape, kv_head), s, NEG_INF)
    e, l = _softmax_rows(s)
    e_v = pltpu.roll(e, N_KV_HEADS, 1)
    o_win = jnp.dot(e_v.astype(BF16), rows, preferred_element_type=F32) / l
    g = g_ref[0]
    o = g[:, 0:1] * ocmp_ref[0] + g[:, 1:2] * osel_ref[0] + g[:, 2:3] * o_win
    o_ref[0] = o.astype(o_ref.dtype)


def _win_sample(win_rows, new_rows, qr3, o_cmp, o_sel, g3):
    n_b, wr, _ = win_rows.shape
    wb = wr // HEADS_PER_ROW
    hspec = pl.BlockSpec((1, N_HEADS, HEAD_DIM), lambda b: (b, 0, 0))
    wspec = pl.BlockSpec((1, wr, HEAD_DIM), lambda b: (b, 0, 0))
    return pl.pallas_call(
        functools.partial(_win_sample_kernel, wb=wb),
        grid=(n_b,),
        in_specs=[
            wspec,
            pl.BlockSpec((1, HEADS_PER_ROW, HEAD_DIM), lambda b: (b, 0, 0)),
            hspec,
            hspec,
            hspec,
            pl.BlockSpec((1, N_HEADS, 3), lambda b: (b, 0, 0)),
        ],
        out_specs=[wspec, hspec],
        out_shape=[jax.ShapeDtypeStruct((n_b, wr, HEAD_DIM), F32), jax.ShapeDtypeStruct((n_b, N_HEADS, HEAD_DIM), BF16)],
        compiler_params=_params("parallel"),
        name="win_sample",
    )(win_rows, new_rows, qr3, o_cmp, o_sel, g3)


def _rope_tables(pos):
    half = HEAD_DIM // 2
    inv = ROPE_THETA ** (-jnp.arange(half, dtype=F32) / half)
    ang = pos.astype(F32)[:, None] * inv
    cos = jnp.cos(ang)
    sin = jnp.sin(ang)
    return jnp.concatenate([cos, cos], axis=-1), jnp.concatenate([-sin, sin], axis=-1)


def _prep_weights(lw):
    w_in = lw["w_in"]
    c = 0
    parts = {}
    conv_ch = lw["w_conv_dw"].shape[1]
    d_model = w_in.shape[0]
    for name, width in (("q", Q_W), ("cmp", ROW_W), ("sel", ROW_W), ("win", ROW_W), ("gnsa", 3 * N_HEADS),
                        ("glu", 2 * conv_ch), ("gm", 2 * d_model)):
        parts[name] = w_in[:, c : c + width]
        c += width
    wg = parts["gnsa"].reshape(d_model, 3, N_KV_HEADS, GROUP).transpose(0, 2, 1, 3).reshape(d_model, N_KV_HEADS, 3 * GROUP)
    wg = jnp.pad(wg, ((0, 0), (0, 0), (0, LANES - 3 * GROUP))).reshape(d_model, N_KV_HEADS * LANES)
    w1 = lw["w_cmp1"]
    hid = w1.shape[-1]
    r = CMP_BLOCK // CMP_STRIDE
    w1all = w1.reshape(2, r, CMP_STRIDE, HEAD_DIM, hid).transpose(2, 3, 0, 1, 4).reshape(CMP_STRIDE * HEAD_DIM, 2 * r * hid)
    dff = lw["w_up"].shape[1] // 2
    return {
        "w_q": parts["q"].astype(BF16),
        "w_main": jnp.concatenate([parts["glu"], parts["gm"]], axis=1).astype(BF16),
        "w_cmp_gate": jnp.concatenate([parts["cmp"], wg], axis=1).astype(BF16),
        "w_sel": parts["sel"].astype(BF16),
        "w_win": parts["win"].astype(BF16),
        "w1all": w1all.astype(BF16),
        "w2all": jnp.concatenate([lw["w_cmp2"][0], lw["w_cmp2"][1]], axis=1).astype(BF16),
        "w_attn_out": lw["w_attn_out"].astype(BF16),
        "w_conv_out": lw["w_conv_out"].astype(BF16),
        "w_out": lw["w_out"].astype(BF16),
        "w_up_a": lw["w_up"][:, :dff].astype(BF16),
        "w_up_b": lw["w_up"][:, dff:].astype(BF16),
        "w_down": lw["w_down"].astype(BF16),
        "w_ple_gate": lw["w_ple_gate"].astype(BF16),
        "w_ple_proj": lw["w_ple_proj"].astype(BF16),
    }


def _dense_tail(x, mixed, pe, lw, pw, *, tm, t_len=None, ffn_state=None):
    h1 = _out_proj(mixed, pw["w_out"], x, tm)
    h2, a_tail = _conv_ffn(h1, lw["g_ffn"], pw["w_up_a"], pw["w_up_b"], lw["w_ffn_dw"], lw["b_ffn_dw"], pw["w_down"],
                           tm=tm, t_len=t_len, state=ffn_state)
    h3 = _ple(h2, lw["g_ple"], pw["w_ple_gate"], pe.astype(BF16), pw["w_ple_proj"], tm)
    return h3, a_tail


def _layer_prompt(x3, pe3, lw, pw, bias_cmp):
    n_b, t_len, d = x3.shape
    m = n_b * t_len
    x = x3.reshape(m, d)
    cos, sin = _rope_tables(jnp.arange(t_len, dtype=jnp.int32))
    xn = _rmsnorm_bf16(x, lw["g_mix"], 512)
    zmain = _matmul(xn, pw["w_main"], 1024, 1024)
    rows_cmp, gates = _cmp_gate_proj(xn, pw["w_cmp_gate"], 512)
    rows_sel, sel_bf = _kv_proj(xn, pw["w_sel"], lw["g_k"][1], cos, sin, 512)
    w_keep = min(WINDOW, t_len)
    assert w_keep == 512
    rows_win, win_bf = _kv_proj(xn, pw["w_win"], lw["g_k"][2], cos, sin, w_keep, last_tile_only=True)
    n_pp = t_len // PAGE_SIZE
    ident = jnp.arange(n_b * n_pp, dtype=jnp.int32).reshape(n_b, n_pp)
    kc, vc = _compress(rows_cmp.reshape(n_b * n_pp, PAGE_SIZE, HEADS_PER_ROW, HEAD_DIM), ident, pw["w1all"], bias_cmp,
                       pw["w2all"], lw["b_cmp2"], lw["g_k"][0])
    q_bf, qr_bf = _q_proj(xn, pw["w_q"], lw["g_q"], cos, sin, 512, BF16)
    o = _nsa_prompt(q_bf, qr_bf, gates, kc, vc, sel_bf, win_bf, n_b, t_len)
    mixed, u_tail = _conv_mix_prompt(zmain, o, lw["w_conv_dw"], lw["b_conv_dw"], lw["g_conv_ln"], lw["b_conv_ln"],
                                     pw["w_conv_out"], pw["w_attn_out"], n_b, t_len)
    h3, a_tail = _dense_tail(x, mixed, pe3.reshape(m, -1), lw, pw, tm=512, t_len=t_len)
    kvshape = (n_b, t_len, 2, N_KV_HEADS, HEAD_DIM)
    w_keep = min(WINDOW, t_len)
    conv_w = lw["w_conv_dw"].shape[0]
    state = (
        rows_cmp.reshape(kvshape),
        rows_sel.reshape(kvshape),
        rows_win.reshape(n_b, w_keep, 2, N_KV_HEADS, HEAD_DIM),
        u_tail[:, CONV_HALO - (conv_w - 1) :],
        a_tail.reshape(n_b, -1, SUBLANES, a_tail.shape[-1])[:, -1, SUBLANES - (lw["w_ffn_dw"].shape[0] - 1) :],
    )
    return h3.reshape(n_b, t_len, d), state


def _layer_sample(x3, pe3, lw, pw, bias_cmp, cache_cmp, cache_sel, cache_win, conv_state, ffn_state, page_table):
    n_b, t_new, d = x3.shape
    assert t_new == 1
    n_pp = page_table.shape[1]
    past_len = n_pp * PAGE_SIZE
    pos = past_len
    wb = cache_win.shape[1]
    assert wb == WINDOW and past_len % SEL_BLOCK == 0
    x = x3.reshape(n_b, d)
    cos, sin = _rope_tables(jnp.full((n_b,), pos, dtype=jnp.int32))
    xn = _rmsnorm_bf16(x, lw["g_mix"], n_b)
    zmain = _matmul(xn, pw["w_main"], n_b, 1024)
    rows_cmp, gates = _cmp_gate_proj(xn, pw["w_cmp_gate"], n_b)
    rows_sel, _ = _kv_proj(xn, pw["w_sel"], lw["g_k"][1], cos, sin, n_b)
    rows_win, _ = _kv_proj(xn, pw["w_win"], lw["g_k"][2], cos, sin, n_b)
    n_pool = cache_cmp.shape[0]
    kc, vc = _compress(cache_cmp.reshape(n_pool, PAGE_SIZE, HEADS_PER_ROW, HEAD_DIM), page_table, pw["w1all"], bias_cmp,
                       pw["w2all"], lw["b_cmp2"], lw["g_k"][0])
    n_past_blocks = past_len // SEL_BLOCK
    nb = n_past_blocks + 1
    n_sel = min(N_SEL, nb)
    q_s, qr_s = _q_proj(xn, pw["w_q"], lw["g_q"], cos, sin, n_b, F32)
    q3 = q_s.reshape(n_b, N_HEADS, HEAD_DIM)
    qr3 = qr_s.reshape(n_b, N_HEADS, HEAD_DIM)
    score, o_cmp = _cmp_sample(q3, kc, vc, pos, nb)
    idx = _topk(score.reshape(n_b * SUBLANES, -1), n_sel)
    idx = idx.reshape(n_b, SUBLANES, LANES)[:, :N_KV_HEADS, :n_sel]
    o_sel = _sel_sample(idx.reshape(-1), page_table.reshape(-1),
                        cache_sel.reshape(n_pool * PAGE_SIZE * HEADS_PER_ROW, HEAD_DIM),
                        qr3.reshape(n_b, N_KV_HEADS, GROUP, HEAD_DIM), rows_sel.reshape(n_b, HEADS_PER_ROW, HEAD_DIM),
                        n_sel, n_past_blocks, n_pp)
    g3 = gates.reshape(n_b, N_KV_HEADS, LANES)[:, :, : 3 * GROUP].reshape(n_b, N_KV_HEADS, 3, GROUP)
    g3 = g3.transpose(0, 1, 3, 2).reshape(n_b, N_HEADS, 3)
    new_win, o = _win_sample(cache_win.reshape(n_b, wb * HEADS_PER_ROW, HEAD_DIM),
                             rows_win.reshape(n_b, HEADS_PER_ROW, HEAD_DIM), qr3, o_cmp,
                             o_sel.reshape(n_b, N_HEADS, HEAD_DIM), g3)
    y_conv, u = _conv_sample(zmain, conv_state, lw["w_conv_dw"], lw["b_conv_dw"], lw["g_conv_ln"], lw["b_conv_ln"],
                             pw["w_conv_out"])
    mixed = _attn_mix(o.reshape(n_b, Q_W), pw["w_attn_out"], zmain, y_conv, n_b)
    h3, a_new = _dense_tail(x, mixed, pe3.reshape(n_b, -1), lw, pw, tm=n_b, ffn_state=ffn_state)
    kvshape = (n_b, 1, 2, N_KV_HEADS, HEAD_DIM)
    state = (
        rows_cmp.reshape(kvshape),
        rows_sel.reshape(kvshape),
        new_win.reshape(n_b, wb, 2, N_KV_HEADS, HEAD_DIM),
        jnp.concatenate([conv_state[:, 1:], u[:, None, :]], axis=1),
        jnp.concatenate([ffn_state[:, 1:], a_new[:, None, :]], axis=1),
    )
    return h3.reshape(n_b, 1, d), state


_LAYER_WEIGHTS = ("g_mix", "w_in", "g_q", "g_k", "w_cmp1", "b_cmp1", "w_cmp2", "b_cmp2", "pe_cmp", "w_attn_out",
                  "w_conv_dw", "b_conv_dw", "g_conv_ln", "b_conv_ln", "w_conv_out", "w_out", "g_ffn", "w_up",
                  "w_ffn_dw", "b_ffn_dw", "w_down", "g_ple", "w_ple_gate", "w_ple_proj")


def kernel(x_prompt, x_sample, p_prompt, p_sample, cache_cmp_kv, cache_sel_kv, cache_win_kv, state_conv, state_ffn_conv, page_table, g_mix, w_in, g_q, g_k, w_cmp1, b_cmp1, w_cmp2, b_cmp2, pe_cmp, w_attn_out, w_conv_dw, b_conv_dw, g_conv_ln, b_conv_ln, w_conv_out, w_out, g_ffn, w_up, w_ffn_dw, b_ffn_dw, w_down, g_ple, w_ple_gate, w_ple_proj):
    stacked = dict(zip(_LAYER_WEIGHTS, (g_mix, w_in, g_q, g_k, w_cmp1, b_cmp1, w_cmp2, b_cmp2, pe_cmp, w_attn_out,
                                        w_conv_dw, b_conv_dw, g_conv_ln, b_conv_ln, w_conv_out, w_out, g_ffn, w_up,
                                        w_ffn_dw, b_ffn_dw, w_down, g_ple, w_ple_gate, w_ple_proj)))
    depth = w_in.shape[0]
    hp, hs = x_prompt, x_sample
    states_p, states_s = [], []
    for i in range(depth):
        lw = {k: v[i] for k, v in stacked.items()}
        pw = _prep_weights(lw)
        bias_cmp = _pe_bias(lw["pe_cmp"], lw["w_cmp1"], lw["b_cmp1"])
        hp, st_p = _layer_prompt(hp, p_prompt[i], lw, pw, bias_cmp)
        hs, st_s = _layer_sample(hs, p_sample[i], lw, pw, bias_cmp, cache_cmp_kv[i], cache_sel_kv[i], cache_win_kv[i],
                                 state_conv[i], state_ffn_conv[i], page_table)
        states_p.append(st_p)
        states_s.append(st_s)
    outs = [hp, hs]
    for k in range(5):
        outs.append(jnp.stack([s[k] for s in states_p]))
        outs.append(jnp.stack([s[k] for s in states_s]))
    return tuple(outs)
```
